```python
import math
import jax
import jax.numpy as jnp
from jax import lax
import numpy as np

D_MODEL = 1024
BATCH = 4
SEQ = 4096
DEPTH = 2

CTX_LEN = 256
GRID_W = 64
EPS = 1e-6

SSD_HEADS = 16
SSD_HEAD_DIM = 64
SSD_WIDTH = SSD_HEADS * SSD_HEAD_DIM
SSD_GROUPS = 2
SSD_STATE = 128
SSD_CONV = 5
SSD_CHUNK = 128
SSD_XBC = SSD_WIDTH + 2 * SSD_GROUPS * SSD_STATE
DT_MIN = 0.001
DT_MAX = 0.1

ATT_HEADS = 8
ATT_KV_HEADS = 2
ATT_HEAD_DIM = 64
ATT_WIDTH = ATT_HEADS * ATT_HEAD_DIM
ATT_KV_WIDTH = ATT_KV_HEADS * ATT_HEAD_DIM
WINDOW = 128
ATT_BLOCK = 128
ROPE_BASE = 10000.0

CONV_WIDTH = 512
CONV_KERNEL = 31

MIX_WIDTH = SSD_WIDTH + ATT_WIDTH + CONV_WIDTH
IN_SIZES = (SSD_XBC, SSD_WIDTH, 2 * SSD_HEADS, ATT_WIDTH, ATT_KV_WIDTH, ATT_KV_WIDTH, ATT_WIDTH, 2 * CONV_WIDTH, CONV_WIDTH)
IN_WIDTH = sum(IN_SIZES)

kernel_name = 'hybrid_ssd_swa_conformer_prefix_block'


def rmsnorm(x, w):
    xf = x.astype(jnp.float32)
    y = xf * lax.rsqrt(jnp.mean(xf * xf, axis=-1, keepdims=True) + EPS)
    return (y * w.astype(jnp.float32)).astype(x.dtype)


def group_rmsnorm(x, w, groups):
    shp = x.shape
    xf = x.astype(jnp.float32).reshape(shp[:-1] + (groups, shp[-1] // groups))
    xf = xf * lax.rsqrt(jnp.mean(xf * xf, axis=-1, keepdims=True) + EPS)
    return (xf.reshape(shp) * w.astype(jnp.float32)).astype(x.dtype)


def layernorm(x, w, b):
    xf = x.astype(jnp.float32)
    xc = xf - jnp.mean(xf, axis=-1, keepdims=True)
    var = jnp.mean(xc * xc, axis=-1, keepdims=True)
    return (xc * lax.rsqrt(var + EPS) * w.astype(jnp.float32) + b.astype(jnp.float32)).astype(x.dtype)


def split_cols(u):
    parts, off = [], 0
    for size in IN_SIZES:
        parts.append(u[..., off:off + size])
        off += size
    return parts


def dwconv_centred(x, w, b):
    k, ch = w.shape
    pad = (k - 1) // 2
    y = lax.conv_general_dilated(x, w[:, None, :].astype(x.dtype), window_strides=(1,), padding=[(pad, pad)],
                                 dimension_numbers=('NWC', 'WIO', 'NWC'), feature_group_count=ch)
    return y + b.astype(x.dtype)


def axial_rope_tables(n_tokens):
    rows = n_tokens // GRID_W
    row = jnp.repeat(jnp.arange(rows, dtype=jnp.float32), GRID_W)
    col = jnp.tile(jnp.arange(GRID_W, dtype=jnp.float32), rows)
    axis_dim = ATT_HEAD_DIM // 2
    inv_freq = ROPE_BASE ** (-jnp.arange(0, axis_dim, 2, dtype=jnp.float32) / axis_dim)
    ang = jnp.stack([row[:, None] * inv_freq, col[:, None] * inv_freq], axis=1)
    return jnp.cos(ang), jnp.sin(ang)


def apply_axial_rope(t, cos, sin):
    shp = t.shape
    tr = t.astype(jnp.float32).reshape(shp[:-1] + (2, 2, ATT_HEAD_DIM // 4))
    t1, t2 = tr[..., 0, :], tr[..., 1, :]
    cs, sn = cos[None, :, None], sin[None, :, None]
    out = jnp.stack([t1 * cs - t2 * sn, t2 * cs + t1 * sn], axis=-2)
    return out.reshape(shp).astype(t.dtype)


def ssd_chunked_scan(x, dt, a, b_in, c_in, init_state):
    bsz, n, nh, hp = x.shape
    ng, ns = b_in.shape[-2:]
    rep = nh // ng
    q = SSD_CHUNK
    nc = n // q
    f32 = jnp.float32
    xd = (x.astype(f32) * dt[..., None]).reshape(bsz, nc, q, ng, rep, hp)
    da = (dt * a).reshape(bsz, nc, q, ng, rep)
    bc = b_in.astype(f32).reshape(bsz, nc, q, ng, ns)
    cc = c_in.astype(f32).reshape(bsz, nc, q, ng, ns)
    cs = jnp.moveaxis(jnp.cumsum(da, axis=2), 2, -1)
    tril = jnp.tril(jnp.ones((q, q), dtype=bool))
    seg = cs[..., :, None] - cs[..., None, :]
    decay = jnp.exp(jnp.where(tril, seg, -jnp.inf))
    cb = jnp.einsum('bclgn,bcsgn->bcgls', cc, bc)
    y_diag = jnp.einsum('bcgrls,bcsgrp->bclgrp', cb[:, :, :, None] * decay, xd)
    decay_to_end = jnp.exp(cs[..., -1:] - cs)
    states = jnp.einsum('bclgn,bcgrl,bclgrp->bcgrpn', bc, decay_to_end, xd)
    chunk_decay = jnp.exp(cs[..., -1])

    def step(s, inp):
        st, dec = inp
        return s * dec[..., None, None] + st, s

    s0 = init_state.astype(f32).reshape(bsz, ng, rep, hp, ns)
    final, entering = lax.scan(step, s0, (jnp.moveaxis(states, 1, 0), jnp.moveaxis(chunk_decay, 1, 0)))
    entering = jnp.moveaxis(entering, 0, 1)
    y_off = jnp.einsum('bclgn,bcgrpn,bcgrl->bclgrp', cc, entering, jnp.exp(cs))
    y = (y_diag + y_off).reshape(bsz, n, nh, hp)
    return y, final.reshape(bsz, nh, hp, ns)


def ssd_bidirectional(xs, dt, bm, cm, a, init_f, init_b):
    y_f, fin_f = ssd_chunked_scan(xs, dt[:, :, 0], a[0], bm, cm, init_f)
    rev = lambda t: jnp.flip(t, axis=1)
    y_b, fin_b = ssd_chunked_scan(rev(xs), rev(dt[:, :, 1]), a[1], rev(bm), rev(cm), init_b)
    return y_f + rev(y_b), fin_f, fin_b


def window_attention(q, k, v, kc, vc, sink):
    bsz, seq = q.shape[:2]
    nb = seq // ATT_BLOCK
    rep = ATT_HEADS // ATT_KV_HEADS
    scale = ATT_HEAD_DIM ** -0.5
    n_ctx = kc.shape[1]
    qb = q.reshape(bsz, nb, ATT_BLOCK, ATT_KV_HEADS, rep, ATT_HEAD_DIM)

    def band(t):
        tp = jnp.pad(t, ((0, 0), (ATT_BLOCK, ATT_BLOCK), (0, 0), (0, 0)))
        tp = tp.reshape(bsz, nb + 2, ATT_BLOCK, ATT_KV_HEADS, ATT_HEAD_DIM)
        return jnp.concatenate([tp[:, :-2], tp[:, 1:-1], tp[:, 2:]], axis=2)

    kw, vw = band(k), band(v)
    s_win = jnp.einsum('bnqgrd,bnkgd->bngrqk', qb, kw).astype(jnp.float32) * scale
    qpos = jnp.arange(nb)[:, None] * ATT_BLOCK + jnp.arange(ATT_BLOCK)[None]
    kpos = jnp.arange(nb)[:, None] * ATT_BLOCK - ATT_BLOCK + jnp.arange(3 * ATT_BLOCK)[None]
    valid = ((jnp.abs(qpos[:, :, None] - kpos[:, None, :]) <= WINDOW)
             & (kpos[:, None, :] >= 0) & (kpos[:, None, :] < seq))
    s_win = jnp.where(valid[None, :, None, None], s_win, jnp.finfo(jnp.float32).min)
    s_ctx = jnp.einsum('bnqgrd,btgd->bngrqt', qb, kc).astype(jnp.float32) * scale
    sink_col = jnp.broadcast_to(sink.astype(jnp.float32).reshape(1, 1, ATT_KV_HEADS, rep, 1, 1), s_ctx.shape[:-1] + (1,))
    p = jax.nn.softmax(jnp.concatenate([sink_col, s_ctx, s_win], axis=-1), axis=-1).astype(v.dtype)
    out = (jnp.einsum('bngrqt,btgd->bnqgrd', p[..., 1:1 + n_ctx], vc)
           + jnp.einsum('bngrqk,bnkgd->bnqgrd', p[..., 1 + n_ctx:], vw))
    return out.reshape(bsz, seq, ATT_WIDTH)


def context_attention(qc, kc, vc, sink):
    bsz, n_ctx = qc.shape[:2]
    rep = ATT_HEADS // ATT_KV_HEADS
    qg = qc.reshape(bsz, n_ctx, ATT_KV_HEADS, rep, ATT_HEAD_DIM)
    s = jnp.einsum('bqgrd,btgd->bgrqt', qg, kc).astype(jnp.float32) * ATT_HEAD_DIM ** -0.5
    sink_col = jnp.broadcast_to(sink.astype(jnp.float32).reshape(1, ATT_KV_HEADS, rep, 1, 1), s.shape[:-1] + (1,))
    p = jax.nn.softmax(jnp.concatenate([sink_col, s], axis=-1), axis=-1).astype(vc.dtype)
    out = jnp.einsum('bgrqt,btgd->bqgrd', p[..., 1:], vc)
    return out.reshape(bsz, n_ctx, ATT_WIDTH)


def conformer_conv(u, dw_w, dw_b, ln_w, ln_b, pw_w, pw_b):
    val, gt = u[..., :CONV_WIDTH], u[..., CONV_WIDTH:]
    hdn = val * jax.nn.sigmoid(gt)
    hdn = dwconv_centred(hdn, dw_w, dw_b)
    hdn = jax.nn.silu(layernorm(hdn, ln_w, ln_b))
    return hdn @ pw_w + pw_b


def hybrid_layer(x, xc, c, c_ctx, w_mod, b_mod, norm_w, w_in, ssd_conv_w, ssd_conv_b, ssd_dt_bias, ssd_a_log,
                 ssd_d, ssd_norm_w, attn_sink, conv_dw_w, conv_dw_b, conv_ln_w, conv_ln_b, conv_pw_w, conv_pw_b,
                 w_out, update_ctx):
    bsz, seq = x.shape[:2]
    n_ctx = xc.shape[1]
    f32 = jnp.float32
    shift, scale, gate = jnp.split((jax.nn.silu(c) @ w_mod + b_mod)[:, None, :], 3, axis=-1)
    shift_c, scale_c, gate_c = jnp.split(jax.nn.silu(c_ctx) @ w_mod + b_mod, 3, axis=-1)
    h = rmsnorm(x, norm_w) * (1 + scale) + shift
    hc = rmsnorm(xc, norm_w) * (1 + scale_c) + shift_c
    p_lat = split_cols(h @ w_in)
    p_ctx = split_cols(hc @ w_in)

    a = -jnp.exp(ssd_a_log.astype(f32))

    def ssd_inputs(xbc, dt_raw):
        xbc = jax.nn.silu(dwconv_centred(xbc, ssd_conv_w, ssd_conv_b))
        n = xbc.shape[1]
        gn = SSD_GROUPS * SSD_STATE
        xs = xbc[..., :SSD_WIDTH].reshape(bsz, n, SSD_HEADS, SSD_HEAD_DIM)
        bm = xbc[..., SSD_WIDTH:SSD_WIDTH + gn].reshape(bsz, n, SSD_GROUPS, SSD_STATE)
        cm = xbc[..., SSD_WIDTH + gn:].reshape(bsz, n, SSD_GROUPS, SSD_STATE)
        dt = jax.nn.softplus(dt_raw.astype(f32).reshape(bsz, n, 2, SSD_HEADS) + ssd_dt_bias.astype(f32))
        return xs, dt, bm, cm

    def ssd_out(y, xs, z):
        n = xs.shape[1]
        y = y + ssd_d.astype(f32)[:, None] * xs.astype(f32)
        y = y.reshape(bsz, n, SSD_WIDTH).astype(z.dtype)
        return group_rmsnorm(y * jax.nn.silu(z), ssd_norm_w, SSD_GROUPS)

    zero = jnp.zeros((bsz, SSD_HEADS, SSD_HEAD_DIM, SSD_STATE), f32)
    xs_c, dt_c, bm_c, cm_c = ssd_inputs(p_ctx[0], p_ctx[2])
    y_c, fin_f, fin_b = ssd_bidirectional(xs_c, dt_c, bm_c, cm_c, a, zero, zero)
    xs_l, dt_l, bm_l, cm_l = ssd_inputs(p_lat[0], p_lat[2])
    y_l, _, _ = ssd_bidirectional(xs_l, dt_l, bm_l, cm_l, a, fin_f, fin_b)
    o_ssd = ssd_out(y_l, xs_l, p_lat[1])

    cos, sin = axial_rope_tables(seq)
    q = apply_axial_rope(p_lat[3].reshape(bsz, seq, ATT_HEADS, ATT_HEAD_DIM), cos, sin)
    k = apply_axial_rope(p_lat[4].reshape(bsz, seq, ATT_KV_HEADS, ATT_HEAD_DIM), cos, sin)
    v = p_lat[5].reshape(bsz, seq, ATT_KV_HEADS, ATT_HEAD_DIM)
    kc = p_ctx[4].reshape(bsz, n_ctx, ATT_KV_HEADS, ATT_HEAD_DIM)
    vc = p_ctx[5].reshape(bsz, n_ctx, ATT_KV_HEADS, ATT_HEAD_DIM)
    o_att = window_attention(q, k, v, kc, vc, attn_sink) * jax.nn.silu(p_lat[6])

    o_conv = conformer_conv(p_lat[7], conv_dw_w, conv_dw_b, conv_ln_w, conv_ln_b, conv_pw_w, conv_pw_b) * jax.nn.silu(p_lat[8])

    x = x + gate * (jnp.concatenate([o_ssd, o_att, o_conv], axis=-1) @ w_out)

    if update_ctx:
        qc = p_ctx[3].reshape(bsz, n_ctx, ATT_HEADS, ATT_HEAD_DIM)
        o_ssd_c = ssd_out(y_c, xs_c, p_ctx[1])
        o_att_c = context_attention(qc, kc, vc, attn_sink) * jax.nn.silu(p_ctx[6])
        o_conv_c = conformer_conv(p_ctx[7], conv_dw_w, conv_dw_b, conv_ln_w, conv_ln_b, conv_pw_w, conv_pw_b) * jax.nn.silu(p_ctx[8])
        xc = xc + gate_c * (jnp.concatenate([o_ssd_c, o_att_c, o_conv_c], axis=-1) @ w_out)
    return x, xc


def setup_inputs(seed: int = 0) -> dict:
    key = jax.random.key(seed)
    ks = jax.random.split(key, 24)
    f32 = jnp.float32

    def nrm(k, shape, s):
        return jax.random.normal(k, shape, f32) * s

    u = jax.random.uniform(ks[10], (DEPTH, 2, SSD_HEADS), f32)
    dt0 = jnp.exp(u * (math.log(DT_MAX) - math.log(DT_MIN)) + math.log(DT_MIN))
    return {
        'x': nrm(ks[0], (BATCH, SEQ, D_MODEL), 1.0),
        'c': nrm(ks[1], (BATCH, D_MODEL), 1.0),
        'ctx': nrm(ks[2], (BATCH, CTX_LEN, D_MODEL), 1.0),
        'c_ctx': nrm(ks[3], (D_MODEL,), 1.0),
        'w_mod': nrm(ks[4], (DEPTH, D_MODEL, 3 * D_MODEL), 0.5 * D_MODEL ** -0.5),
        'b_mod': nrm(ks[5], (DEPTH, 3 * D_MODEL), 0.02),
        'norm_w': 1.0 + nrm(ks[6], (DEPTH, D_MODEL), 0.02),
        'w_in': nrm(ks[7], (DEPTH, D_MODEL, IN_WIDTH), D_MODEL ** -0.5),
        'ssd_conv_w': nrm(ks[8], (DEPTH, SSD_CONV, SSD_XBC), SSD_CONV ** -0.5),
        'ssd_conv_b': nrm(ks[9], (DEPTH, SSD_XBC), 0.02),
        'ssd_dt_bias': dt0 + jnp.log(-jnp.expm1(-dt0)),
        'ssd_a_log': jnp.log(jax.random.uniform(ks[11], (DEPTH, 2, SSD_HEADS), f32, 1.0, 16.0)),
        'ssd_d': 1.0 + nrm(ks[12], (DEPTH, SSD_HEADS), 0.1),
        'ssd_norm_w': 1.0 + nrm(ks[13], (DEPTH, SSD_WIDTH), 0.02),
        'attn_sink': nrm(ks[14], (DEPTH, ATT_HEADS), 0.5),
        'conv_dw_w': nrm(ks[15], (DEPTH, CONV_KERNEL, CONV_WIDTH), CONV_KERNEL ** -0.5),
        'conv_dw_b': nrm(ks[16], (DEPTH, CONV_WIDTH), 0.02),
        'conv_ln_w': 1.0 + nrm(ks[17], (DEPTH, CONV_WIDTH), 0.02),
        'conv_ln_b': nrm(ks[18], (DEPTH, CONV_WIDTH), 0.02),
        'conv_pw_w': nrm(ks[19], (DEPTH, CONV_WIDTH, CONV_WIDTH), CONV_WIDTH ** -0.5),
        'conv_pw_b': nrm(ks[20], (DEPTH, CONV_WIDTH), 0.02),
        'w_out': nrm(ks[21], (DEPTH, MIX_WIDTH, D_MODEL), MIX_WIDTH ** -0.5),
        'final_norm_w': 1.0 + nrm(ks[22], (D_MODEL,), 0.02),
    }


def reference(x, c, ctx, c_ctx, w_mod, b_mod, norm_w, w_in, ssd_conv_w, ssd_conv_b, ssd_dt_bias, ssd_a_log,
              ssd_d, ssd_norm_w, attn_sink, conv_dw_w, conv_dw_b, conv_ln_w, conv_ln_b, conv_pw_w, conv_pw_b,
              w_out, final_norm_w):
    xc = ctx
    for l in range(DEPTH):
        x, xc = hybrid_layer(x, xc, c, c_ctx, w_mod[l], b_mod[l], norm_w[l], w_in[l], ssd_conv_w[l], ssd_conv_b[l],
                             ssd_dt_bias[l], ssd_a_log[l], ssd_d[l], ssd_norm_w[l], attn_sink[l], conv_dw_w[l],
                             conv_dw_b[l], conv_ln_w[l], conv_ln_b[l], conv_pw_w[l], conv_pw_b[l], w_out[l],
                             l < DEPTH - 1)
    return rmsnorm(x, final_norm_w)
```

```python
import functools

import jax
import jax.numpy as jnp
from jax import lax
from jax.experimental import pallas as pl
from jax.experimental.pallas import tpu as pltpu

F32 = jnp.float32
BF16 = jnp.bfloat16

GRID_W = 64
EPS = 1e-6

SSD_HEADS = 16
SSD_HEAD_DIM = 64
SSD_WIDTH = SSD_HEADS * SSD_HEAD_DIM
SSD_GROUPS = 2
SSD_STATE = 128
SSD_CONV = 5
SSD_CHUNK = 128
SSD_GN = SSD_GROUPS * SSD_STATE
SSD_XBC = SSD_WIDTH + 2 * SSD_GN
GROUP_COLS = SSD_WIDTH // SSD_GROUPS

ATT_HEADS = 8
ATT_KV_HEADS = 2
ATT_HEAD_DIM = 64
ATT_WIDTH = ATT_HEADS * ATT_HEAD_DIM
ATT_KV_WIDTH = ATT_KV_HEADS * ATT_HEAD_DIM
WINDOW = 128
ROPE_BASE = 10000.0
ATT_SCALE = ATT_HEAD_DIM ** -0.5

CONV_WIDTH = 512
CONV_KERNEL = 31
CONV_PAD = (CONV_KERNEL - 1) // 2

LANES = 128
SUBLANES = 8
VMEM_LIMIT = 56 * 1024 * 1024

ROW_TILE = 256
ATT_TILE = 256
CONF_TILE = 256
CONF_HALO = 16
CONF_SUB = 32
SSD_HALO = 8


def _dot(a, b):
    return jnp.dot(a, b, preferred_element_type=F32)


def _dot_nt(a, b):
    return lax.dot_general(a, b, (((1,), (1,)), ((), ())), preferred_element_type=F32)


def _split3(x):
    hi = x.astype(BF16)
    r1 = x - hi.astype(F32)
    mid = r1.astype(BF16)
    lo = (r1 - mid.astype(F32)).astype(BF16)
    return hi, mid, lo


def _exact_rdot(x, e):
    hi, mid, lo = _split3(x)
    return _dot(hi, e) + _dot(mid, e) + _dot(lo, e)


def _exact_ldot(m, x):
    hi, mid, lo = _split3(x)
    return _dot(m, hi) + _dot(m, mid) + _dot(m, lo)


def _cparams(*sem):
    return pltpu.CompilerParams(dimension_semantics=sem, vmem_limit_bytes=VMEM_LIMIT)


def _mod_kernel(c_ref, w_ref, b_ref, o_ref):
    a = jax.nn.silu(c_ref[...]).astype(BF16)
    o_ref[...] = _dot(a, w_ref[...].astype(BF16)) + b_ref[...]


def _modulation(cond, w_mod, b_mod):
    depth, d, n = w_mod.shape
    rows = cond.shape[0]
    tn = 512
    return pl.pallas_call(
        _mod_kernel,
        grid=(depth, n // tn),
        in_specs=[
            pl.BlockSpec((rows, d), lambda l, j: (0, 0)),
            pl.BlockSpec((None, d, tn), lambda l, j: (l, 0, j)),
            pl.BlockSpec((None, 1, tn), lambda l, j: (l, 0, j)),
        ],
        out_specs=pl.BlockSpec((None, rows, tn), lambda l, j: (l, 0, j)),
        out_shape=jax.ShapeDtypeStruct((depth, rows, n), F32),
        compiler_params=_cparams("arbitrary", "arbitrary"),
        name="modulation",
    )(cond, w_mod, b_mod.reshape(depth, 1, n))


def _rope(t, cos, sin):
    width = t.shape[-1]
    lane = lax.broadcasted_iota(jnp.int32, t.shape, 1)
    first = (lane % 32) < 16
    rot = jnp.where(first, pltpu.roll(t, width - 16, 1), pltpu.roll(t, 16, 1))
    return t * cos + rot * sin


def _inproj_kernel(x_ref, mod_ref, nw_ref, cos_ref, sin_ref,
                   wxbc, wz, wdt, wq, wk, wv, wag, wglu, wcg,
                   oxbc, oz, odt, oq, ok, ov, oag, oglu, ocg):
    x = x_ref[...]
    y = x * lax.rsqrt(jnp.mean(x * x, axis=-1, keepdims=True) + EPS) * nw_ref[...]
    h = (y * (1.0 + mod_ref[1:2, :]) + mod_ref[0:1, :]).astype(BF16)
    oxbc[...] = _dot(h, wxbc[...])
    oz[...] = _dot(h, wz[...])
    odt[...] = _dot(h, wdt[...])
    oag[...] = _dot(h, wag[...])
    oglu[...] = _dot(h, wglu[...])
    ocg[...] = _dot(h, wcg[...])
    ov[...] = _dot(h, wv[...]).astype(BF16)
    cos = cos_ref[...]
    sin = sin_ref[...]
    ok[...] = _rope(_dot(h, wk[...]), cos, sin).astype(BF16)
    reps = ATT_WIDTH // LANES
    cos_q = jnp.concatenate([cos] * reps, axis=1)
    sin_q = jnp.concatenate([sin] * reps, axis=1)
    oq[...] = (_rope(_dot(h, wq[...]), cos_q, sin_q) * ATT_SCALE).astype(BF16)


def _inproj(xa, modsel, norm_w, cos_t, sin_t, weights, nct):
    bsz, t, d = xa.shape
    tm = ROW_TILE
    widths = [w.shape[1] for w in weights]
    dtypes = [F32, F32, F32, BF16, BF16, BF16, F32, F32, F32]
    row_map = lambda b, i: (b, i, 0)
    const2 = lambda b, i: (0, 0)
    in_specs = [
        pl.BlockSpec((None, tm, d), row_map),
        pl.BlockSpec((None, None, 3, d), lambda b, i: (b, jnp.where(i >= nct, 1, 0), 0, 0)),
        pl.BlockSpec((1, d), const2),
        pl.BlockSpec((tm, LANES), lambda b, i: (i, 0)),
        pl.BlockSpec((tm, LANES), lambda b, i: (i, 0)),
    ] + [pl.BlockSpec((d, w), const2) for w in widths]
    out_specs = [pl.BlockSpec((None, tm, w), row_map) for w in widths]
    out_shape = [jax.ShapeDtypeStruct((bsz, t, w), dt) for w, dt in zip(widths, dtypes)]
    return pl.pallas_call(
        _inproj_kernel,
        grid=(bsz, t // tm),
        in_specs=in_specs,
        out_specs=out_specs,
        out_shape=out_shape,
        compiler_params=_cparams("parallel", "arbitrary"),
        name="inproj",
    )(xa, modsel, norm_w, cos_t, sin_t, *weights)


def _ssd_chunk_of(s, ncc, nc):
    bwd = jnp.where(s < ncc, ncc - 1 - s, nc + ncc - 1 - s)
    return jnp.where(s < nc, bwd, s - nc)


def _ssd_kernel(xm_ref, xp_ref, xn_ref, dt_ref, z_ref, cw_ref, cb_ref, dtb_ref, alog_ref, dvec_ref, nw_ref,
                o_ref, yb_ref, st_ref, *, ncc, nc):
    q = SSD_CHUNK
    s = pl.program_id(1)
    fwd = s >= nc
    c = _ssd_chunk_of(s, ncc, nc)

    @pl.when((s == 0) | (s == nc))
    def _():
        st_ref[...] = jnp.zeros_like(st_ref)

    first = (c == 0) | (c == ncc)
    last = (c == ncc - 1) | (c == nc - 1)
    xp = jnp.where(first, 0.0, xp_ref[...])
    xn = jnp.where(last, 0.0, xn_ref[...])
    xext = jnp.concatenate([xp, xm_ref[...], xn], axis=0)
    acc = cb_ref[...] + cw_ref[0:1, :] * xext[SSD_HALO - 2:SSD_HALO - 2 + q]
    for k in range(1, SSD_CONV):
        off = SSD_HALO - 2 + k
        acc = acc + cw_ref[k:k + 1, :] * xext[off:off + q]
    u = jax.nn.silu(acc)
    xs = u[:, :SSD_WIDTH]
    bm = u[:, SSD_WIDTH:SSD_WIDTH + SSD_GN]
    cm = u[:, SSD_WIDTH + SSD_GN:]

    lane = lax.broadcasted_iota(jnp.int32, (q, LANES), 1)
    head_lane = lane < SSD_HEADS
    dt = jnp.where(head_lane, jax.nn.softplus(dt_ref[...] + dtb_ref[...]), 0.0)
    da = dt * (-jnp.exp(alog_ref[...]))
    li = lax.broadcasted_iota(jnp.int32, (q, q), 0)
    si = lax.broadcasted_iota(jnp.int32, (q, q), 1)
    causal = (si - li) * jnp.where(fwd, -1, 1) >= 0
    cs = _exact_ldot(jnp.where(causal, 1.0, 0.0).astype(BF16), da)
    total = jnp.sum(da, axis=0, keepdims=True)
    to_end = jnp.exp(total - cs)
    from_start = jnp.exp(cs)
    cs_t = cs.T

    er = lax.broadcasted_iota(jnp.int32, (LANES, SSD_WIDTH), 0)
    ec = lax.broadcasted_iota(jnp.int32, (LANES, SSD_WIDTH), 1)
    spread = jnp.where(ec // SSD_HEAD_DIM == er, 1.0, 0.0).astype(BF16)
    dt_e = _exact_rdot(dt, spread)
    w_e = _exact_rdot(dt * to_end, spread)
    fs_e = _exact_rdot(from_start, spread)
    cd_e = _exact_rdot(jnp.broadcast_to(jnp.exp(total), (SUBLANES, LANES)), spread)[0:1]

    xd = (xs * dt_e).astype(BF16)
    xdw = (xs * w_e).astype(BF16)
    bb = bm.astype(BF16)
    cb16 = cm.astype(BF16)
    low = lax.broadcasted_iota(jnp.int32, (q, LANES), 1) < SSD_HEAD_DIM
    zero16 = jnp.zeros((q, LANES), BF16)
    ys = []
    for g in range(SSD_GROUPS):
        gs = slice(g * SSD_STATE, (g + 1) * SSD_STATE)
        gc = slice(g * GROUP_COLS, (g + 1) * GROUP_COLS)
        scores = _dot_nt(cb16[:, gs], bb[:, gs])
        entering = st_ref[g]
        y_off = _dot(cb16[:, gs], entering.astype(BF16))
        new_state = _dot(bm[:, gs].T.astype(BF16), xdw[:, gc])
        st_ref[g] = entering * cd_e[:, gc] + new_state
        for pr in range(GROUP_COLS // LANES):
            h0 = (g * GROUP_COLS + pr * LANES) // SSD_HEAD_DIM
            cols = slice(h0 * SSD_HEAD_DIM, h0 * SSD_HEAD_DIM + LANES)
            xpair = xd[:, cols]
            m0 = (scores * jnp.exp(jnp.where(causal, cs[:, h0:h0 + 1] - cs_t[h0:h0 + 1, :], -jnp.inf))).astype(BF16)
            m1 = (scores * jnp.exp(jnp.where(causal, cs[:, h0 + 1:h0 + 2] - cs_t[h0 + 1:h0 + 2, :], -jnp.inf))).astype(BF16)
            y_diag = _dot(m0, jnp.where(low, xpair, zero16)) + _dot(m1, jnp.where(low, zero16, xpair))
            ys.append(y_diag + y_off[:, pr * LANES:(pr + 1) * LANES] * fs_e[:, cols])
    y = jnp.concatenate(ys, axis=1)

    row0 = pl.multiple_of(c * q, q)

    @pl.when(jnp.logical_not(fwd))
    def _():
        yb_ref[pl.ds(row0, q), :] = y

    @pl.when(fwd)
    def _():
        tot = y + yb_ref[pl.ds(row0, q), :] + dvec_ref[...] * xs
        gated = tot * jax.nn.silu(z_ref[...])
        outs = []
        for g in range(SSD_GROUPS):
            gc = slice(g * GROUP_COLS, (g + 1) * GROUP_COLS)
            v = gated[:, gc]
            outs.append(v * lax.rsqrt(jnp.mean(v * v, axis=-1, keepdims=True) + EPS) * nw_ref[:, gc])
        o_ref[...] = jnp.concatenate(outs, axis=1).astype(BF16)


def _ssd(xbc, dtp, z, conv_w, conv_b, dt_bias, a_log, dvec, norm_w, ncc):
    bsz, t, _ = xbc.shape
    q = SSD_CHUNK
    nc = t // q
    hb = q // SSD_HALO
    nhalo = t // SSD_HALO
    chunk = lambda s: _ssd_chunk_of(s, ncc, nc)
    const2 = lambda b, s: (0, 0)
    phase = lambda s: jnp.where(s < nc, 1, 0)
    in_specs = [
        pl.BlockSpec((None, q, SSD_XBC), lambda b, s: (b, chunk(s), 0)),
        pl.BlockSpec((None, SSD_HALO, SSD_XBC), lambda b, s: (b, jnp.maximum(chunk(s) * hb - 1, 0), 0)),
        pl.BlockSpec((None, SSD_HALO, SSD_XBC), lambda b, s: (b, jnp.minimum(chunk(s) * hb + hb, nhalo - 1), 0)),
        pl.BlockSpec((None, q, LANES), lambda b, s: (b, chunk(s), phase(s))),
        pl.BlockSpec((None, q, SSD_WIDTH), lambda b, s: (b, jnp.maximum(s - nc, 0), 0)),
        pl.BlockSpec((SSD_CONV, SSD_XBC), const2),
        pl.BlockSpec((1, SSD_XBC), const2),
        pl.BlockSpec((None, 1, LANES), lambda b, s: (phase(s), 0, 0)),
        pl.BlockSpec((None, 1, LANES), lambda b, s: (phase(s), 0, 0)),
        pl.BlockSpec((1, SSD_WIDTH), const2),
        pl.BlockSpec((1, SSD_WIDTH), const2),
    ]
    return pl.pallas_call(
        functools.partial(_ssd_kernel, ncc=ncc, nc=nc),
        grid=(bsz, 2 * nc),
        in_specs=in_specs,
        out_specs=pl.BlockSpec((None, q, SSD_WIDTH), lambda b, s: (b, jnp.maximum(s - nc, 0), 0)),
        out_shape=jax.ShapeDtypeStruct((bsz, t, SSD_WIDTH), BF16),
        scratch_shapes=[
            pltpu.VMEM((t, SSD_WIDTH), F32),
            pltpu.VMEM((SSD_GROUPS, SSD_STATE, GROUP_COLS), F32),
        ],
        compiler_params=_cparams("parallel", "arbitrary"),
        name="ssd",
    )(xbc, xbc, xbc, dtp, z, conv_w, conv_b, dt_bias, a_log, dvec, norm_w)


def _swap_halves(blk):
    half = blk.shape[-1] // 2
    return jnp.concatenate([blk[:, half:], blk[:, :half]], axis=1)


def _head_pads(blk, g, low):
    zero = jnp.zeros_like(blk)
    sw = _swap_halves(blk)
    if g == 0:
        return jnp.where(low, blk, zero), jnp.where(low, zero, sw)
    return jnp.where(low, sw, zero), jnp.where(low, zero, blk)


def _attn_kernel(q_ref, km_ref, kp_ref, kn_ref, vm_ref, vp_ref, vn_ref, kc_ref, vc_ref, ag_ref, sink_ref, o_ref,
                 *, tile_off, nct, seq):
    qb = ATT_TILE
    nkw = qb + 2 * WINDOW
    tile = pl.program_id(1) + tile_off
    lat0 = (tile - nct) * qb
    q = q_ref[...]
    kw = jnp.concatenate([kp_ref[...], km_ref[...], kn_ref[...]], axis=0)
    vw = jnp.concatenate([vp_ref[...], vm_ref[...], vn_ref[...]], axis=0)
    kc = kc_ref[...]
    vc = vc_ref[...]
    r = lax.broadcasted_iota(jnp.int32, (qb, nkw), 0)
    col = lax.broadcasted_iota(jnp.int32, (qb, nkw), 1)
    kpos = lat0 - WINDOW + col
    valid = (jnp.abs(col - WINDOW - r) <= WINDOW) & (kpos >= 0) & (kpos < seq) & (lat0 >= 0)
    neg = jnp.finfo(F32).min
    low_k = lax.broadcasted_iota(jnp.int32, (1, LANES), 1) < ATT_HEAD_DIM
    rep = ATT_HEADS // ATT_KV_HEADS
    pairs_per_group = rep * ATT_HEAD_DIM // LANES
    outs = []
    for g in range(ATT_KV_HEADS):
        kcs = _head_pads(kc, g, low_k)
        vcs = _head_pads(vc, g, low_k)
        kws = _head_pads(kw, g, low_k)
        vws = _head_pads(vw, g, low_k)
        for pr in range(pairs_per_group):
            p = g * pairs_per_group + pr
            qp = q[:, p * LANES:(p + 1) * LANES]
            acc = jnp.zeros((qb, LANES), F32)
            invs = []
            for j in range(2):
                head = 2 * p + j
                sink = sink_ref[head:head + 1, 0:1]
                sc = _dot_nt(qp, kcs[j])
                sw = jnp.where(valid, _dot_nt(qp, kws[j]), neg)
                m = jnp.maximum(jnp.maximum(jnp.max(sc, axis=-1, keepdims=True),
                                            jnp.max(sw, axis=-1, keepdims=True)), sink)
                pc = jnp.exp(sc - m)
                pw = jnp.exp(sw - m)
                den = (jnp.sum(pc, axis=-1, keepdims=True) + jnp.sum(pw, axis=-1, keepdims=True)
                       + jnp.exp(sink - m))
                acc = acc + _dot(pc.astype(BF16), vcs[j]) + _dot(pw.astype(BF16), vws[j])
                invs.append(1.0 / den)
            outs.append(acc * jnp.where(low_k, invs[0], invs[1]))
    o = jnp.concatenate(outs, axis=1)
    o_ref[...] = (o * jax.nn.silu(ag_ref[...])).astype(BF16)


def _attention(q, k, v, ag, sink_tab, ctx_len, tile_off):
    bsz, t, _ = q.shape
    qb = ATT_TILE
    nct = ctx_len // qb
    hb = qb // WINDOW
    ctx_hb = ctx_len // WINDOW
    n_hb = t // WINDOW
    ntiles = t // qb - tile_off
    main = lambda b, i: (b, i + tile_off, 0)
    prev = lambda b, i: (b, jnp.maximum((i + tile_off) * hb - 1, ctx_hb), 0)
    nxt = lambda b, i: (b, jnp.minimum((i + tile_off) * hb + hb, n_hb - 1), 0)
    ctx = lambda b, i: (b, 0, 0)
    kvw = ATT_KV_WIDTH
    in_specs = [
        pl.BlockSpec((None, qb, ATT_WIDTH), main),
        pl.BlockSpec((None, qb, kvw), main),
        pl.BlockSpec((None, WINDOW, kvw), prev),
        pl.BlockSpec((None, WINDOW, kvw), nxt),
        pl.BlockSpec((None, qb, kvw), main),
        pl.BlockSpec((None, WINDOW, kvw), prev),
        pl.BlockSpec((None, WINDOW, kvw), nxt),
        pl.BlockSpec((None, ctx_len, kvw), ctx),
        pl.BlockSpec((None, ctx_len, kvw), ctx),
        pl.BlockSpec((None, qb, ATT_WIDTH), main),
        pl.BlockSpec((ATT_HEADS, LANES), lambda b, i: (0, 0)),
    ]
    return pl.pallas_call(
        functools.partial(_attn_kernel, tile_off=tile_off, nct=nct, seq=t - ctx_len),
        grid=(bsz, ntiles),
        in_specs=in_specs,
        out_specs=pl.BlockSpec((None, qb, ATT_WIDTH), main),
        out_shape=jax.ShapeDtypeStruct((bsz, t, ATT_WIDTH), BF16),
        compiler_params=_cparams("parallel", "arbitrary"),
        name="attention",
    )(q, k, k, k, v, v, v, k, v, ag, sink_tab)


def _conf_kernel(gm_ref, gp_ref, gn_ref, cg_ref, dww_ref, dwb_ref, lnw_ref, lnb_ref, pww_ref, pwb_ref, o_ref,
                 hbuf, abuf, *, tile_off, nct, nt):
    tc = CONF_TILE
    tile = pl.program_id(1) + tile_off
    first = (tile == 0) | (tile == nct)
    last = (tile == nct - 1) | (tile == nt - 1)

    def glu(v):
        return v[:, :CONV_WIDTH] * jax.nn.sigmoid(v[:, CONV_WIDTH:])

    hbuf[0:CONF_HALO, :] = jnp.where(first, 0.0, glu(gp_ref[...]))
    hbuf[CONF_HALO:CONF_HALO + tc, :] = glu(gm_ref[...])
    hbuf[CONF_HALO + tc:, :] = jnp.where(last, 0.0, glu(gn_ref[...]))
    base = CONF_HALO - CONV_PAD
    for r in range(tc // CONF_SUB):
        acc = dwb_ref[...] + dww_ref[0:1, :] * hbuf[pl.ds(r * CONF_SUB + base, CONF_SUB), :]
        for k in range(1, CONV_KERNEL):
            acc = acc + dww_ref[k:k + 1, :] * hbuf[pl.ds(r * CONF_SUB + base + k, CONF_SUB), :]
        xc = acc - jnp.mean(acc, axis=-1, keepdims=True)
        var = jnp.mean(xc * xc, axis=-1, keepdims=True)
        ln = xc * lax.rsqrt(var + EPS) * lnw_ref[...] + lnb_ref[...]
        abuf[pl.ds(r * CONF_SUB, CONF_SUB), :] = jax.nn.silu(ln)
    out = _dot(abuf[...].astype(BF16), pww_ref[...]) + pwb_ref[...]
    o_ref[...] = (out * jax.nn.silu(cg_ref[...])).astype(BF16)


def _conformer(glu, cg, dw_w, dw_b, ln_w, ln_b, pw_w, pw_b, ctx_len, tile_off):
    bsz, t, _ = glu.shape
    tc = CONF_TILE
    nt = t // tc
    nct = ctx_len // tc
    hb = tc // CONF_HALO
    nhalo = t // CONF_HALO
    main = lambda b, i: (b, i + tile_off, 0)
    const2 = lambda b, i: (0, 0)
    in_specs = [
        pl.BlockSpec((None, tc, 2 * CONV_WIDTH), main),
        pl.BlockSpec((None, CONF_HALO, 2 * CONV_WIDTH), lambda b, i: (b, jnp.maximum((i + tile_off) * hb - 1, 0), 0)),
        pl.BlockSpec((None, CONF_HALO, 2 * CONV_WIDTH),
                     lambda b, i: (b, jnp.minimum((i + tile_off) * hb + hb, nhalo - 1), 0)),
        pl.BlockSpec((None, tc, CONV_WIDTH), main),
        pl.BlockSpec((CONV_KERNEL, CONV_WIDTH), const2),
        pl.BlockSpec((1, CONV_WIDTH), const2),
        pl.BlockSpec((1, CONV_WIDTH), const2),
        pl.BlockSpec((1, CONV_WIDTH), const2),
        pl.BlockSpec((CONV_WIDTH, CONV_WIDTH), const2),
        pl.BlockSpec((1, CONV_WIDTH), const2),
    ]
    return pl.pallas_call(
        functools.partial(_conf_kernel, tile_off=tile_off, nct=nct, nt=nt),
        grid=(bsz, nt - tile_off),
        in_specs=in_specs,
        out_specs=pl.BlockSpec((None, tc, CONV_WIDTH), main),
        out_shape=jax.ShapeDtypeStruct((bsz, t, CONV_WIDTH), BF16),
        scratch_shapes=[
            pltpu.VMEM((tc + 2 * CONF_HALO, CONV_WIDTH), F32),
            pltpu.VMEM((tc, CONV_WIDTH), F32),
        ],
        compiler_params=_cparams("parallel", "arbitrary"),
        name="conformer",
    )(glu, glu, glu, cg, dw_w, dw_b, ln_w, ln_b, pw_w, pw_b)


def _outproj_kernel(x_ref, os_ref, oa_ref, oc_ref, mod_ref, ws_ref, wa_ref, wc_ref, fnw_ref, o_ref, *, final):
    u = _dot(os_ref[...], ws_ref[...]) + _dot(oa_ref[...], wa_ref[...]) + _dot(oc_ref[...], wc_ref[...])
    xn = x_ref[...] + mod_ref[2:3, :] * u
    if final:
        xn = xn * lax.rsqrt(jnp.mean(xn * xn, axis=-1, keepdims=True) + EPS) * fnw_ref[...]
    o_ref[...] = xn


def _outproj(xa, o_ssd, o_att, o_conv, modsel, w_s, w_a, w_c, final_norm_w, nct, final):
    bsz, t, d = xa.shape
    tm = ROW_TILE
    tile_off = nct if final else 0
    main = lambda b, i: (b, i + tile_off, 0)
    const2 = lambda b, i: (0, 0)
    in_specs = [
        pl.BlockSpec((None, tm, d), main),
        pl.BlockSpec((None, tm, SSD_WIDTH), main),
        pl.BlockSpec((None, tm, ATT_WIDTH), main),
        pl.BlockSpec((None, tm, CONV_WIDTH), main),
        pl.BlockSpec((None, None, 3, d), lambda b, i: (b, jnp.where(i + tile_off >= nct, 1, 0), 0, 0)),
        pl.BlockSpec((SSD_WIDTH, d), const2),
        pl.BlockSpec((ATT_WIDTH, d), const2),
        pl.BlockSpec((CONV_WIDTH, d), const2),
        pl.BlockSpec((1, d), const2),
    ]
    rows_out = t - tile_off * tm
    return pl.pallas_call(
        functools.partial(_outproj_kernel, final=final),
        grid=(bsz, rows_out // tm),
        in_specs=in_specs,
        out_specs=pl.BlockSpec((None, tm, d), lambda b, i: (b, i, 0)),
        out_shape=jax.ShapeDtypeStruct((bsz, rows_out, d), F32),
        compiler_params=_cparams("parallel", "arbitrary"),
        name="outproj_final" if final else "outproj",
    )(xa, o_ssd, o_att, o_conv, modsel, w_s, w_a, w_c, final_norm_w)


def _rope_tables(seq, ctx_len):
    rows = seq // GRID_W
    row = jnp.repeat(jnp.arange(rows, dtype=F32), GRID_W)
    col = jnp.tile(jnp.arange(GRID_W, dtype=F32), rows)
    axis_dim = ATT_HEAD_DIM // 2
    inv_freq = ROPE_BASE ** (-jnp.arange(0, axis_dim, 2, dtype=F32) / axis_dim)
    ar = row[:, None] * inv_freq
    ac = col[:, None] * inv_freq
    cos = jnp.concatenate([jnp.cos(ar), jnp.cos(ar), jnp.cos(ac), jnp.cos(ac)], axis=1)
    sin = jnp.concatenate([-jnp.sin(ar), jnp.sin(ar), -jnp.sin(ac), jnp.sin(ac)], axis=1)
    reps = LANES // ATT_HEAD_DIM
    cos = jnp.concatenate([jnp.ones((ctx_len, LANES), F32), jnp.tile(cos, (1, reps))], axis=0)
    sin = jnp.concatenate([jnp.zeros((ctx_len, LANES), F32), jnp.tile(sin, (1, reps))], axis=0)
    return cos, sin


def _pad_lanes(a, width):
    return jnp.pad(a, [(0, 0)] * (a.ndim - 1) + [(0, width - a.shape[-1])])


def _split_w_in(w_in):
    sizes = (SSD_XBC, SSD_WIDTH, 2 * SSD_HEADS, ATT_WIDTH, ATT_KV_WIDTH, ATT_KV_WIDTH, ATT_WIDTH,
             2 * CONV_WIDTH, CONV_WIDTH)
    parts, off = [], 0
    for size in sizes:
        parts.append(w_in[:, off:off + size])
        off += size
    dt_w = parts[2]
    parts[2] = jnp.concatenate([_pad_lanes(dt_w[:, :SSD_HEADS], LANES), _pad_lanes(dt_w[:, SSD_HEADS:], LANES)], axis=1)
    return [p.astype(BF16) for p in parts]


def kernel(x, c, ctx, c_ctx, w_mod, b_mod, norm_w, w_in, ssd_conv_w, ssd_conv_b, ssd_dt_bias, ssd_a_log, ssd_d,
           ssd_norm_w, attn_sink, conv_dw_w, conv_dw_b, conv_ln_w, conv_ln_b, conv_pw_w, conv_pw_b, w_out,
           final_norm_w):
    bsz, seq, d = x.shape
    ctx_len = ctx.shape[1]
    depth = w_in.shape[0]
    assert ctx_len % ROW_TILE == 0 and seq % ROW_TILE == 0 and seq % GRID_W == 0
    assert ROW_TILE == ATT_TILE == CONF_TILE
    nct = ctx_len // ROW_TILE
    ncc = ctx_len // SSD_CHUNK

    cond_rows = -(-(bsz + 1) // SUBLANES) * SUBLANES
    cond = jnp.zeros((cond_rows, d), F32).at[:bsz].set(c).at[bsz].set(c_ctx)
    mod = _modulation(cond, w_mod, b_mod)
    cos_t, sin_t = _rope_tables(seq, ctx_len)
    xa = jnp.concatenate([ctx, x], axis=1)

    out = None
    for l in range(depth):
        final = l == depth - 1
        lat = mod[l, :bsz]
        ctxm = jnp.broadcast_to(mod[l, bsz], lat.shape)
        modsel = jnp.stack([ctxm, lat], axis=1).reshape(bsz, 2, 3, d)
        weights = _split_w_in(w_in[l])
        xbc, z, dtp, q, k, v, ag, glu, cg = _inproj(xa, modsel, norm_w[l][None], cos_t, sin_t, weights, nct)

        o_ssd = _ssd(xbc, dtp, z, ssd_conv_w[l], ssd_conv_b[l][None],
                     _pad_lanes(ssd_dt_bias[l], LANES)[:, None], _pad_lanes(ssd_a_log[l], LANES)[:, None],
                     jnp.repeat(ssd_d[l], SSD_HEAD_DIM)[None], ssd_norm_w[l][None], ncc)
        tile_off = nct if final else 0
        sink_tab = jnp.broadcast_to(attn_sink[l][:, None], (ATT_HEADS, LANES))
        o_att = _attention(q, k, v, ag, sink_tab, ctx_len, tile_off)
        o_conv = _conformer(glu, cg, conv_dw_w[l], conv_dw_b[l][None], conv_ln_w[l][None], conv_ln_b[l][None],
                            conv_pw_w[l].astype(BF16), conv_pw_b[l][None], ctx_len, tile_off)
        w_o = w_out[l].astype(BF16)
        out = _outproj(xa, o_ssd, o_att, o_conv, modsel, w_o[:SSD_WIDTH], w_o[SSD_WIDTH:SSD_WIDTH + ATT_WIDTH],
                       w_o[SSD_WIDTH + ATT_WIDTH:], final_norm_w[None], nct, final)
        xa = out
    return out
```

```python
import functools

import jax
import jax.numpy as jnp
from jax import lax
from jax.experimental import pallas as pl
from jax.experimental.pallas import tpu as pltpu

F32 = jnp.float32
BF16 = jnp.bfloat16

GRID_W = 64
EPS = 1e-6

SSD_HEADS = 16
SSD_HEAD_DIM = 64
SSD_WIDTH = SSD_HEADS * SSD_HEAD_DIM
SSD_GROUPS = 2
SSD_STATE = 128
SSD_CONV = 5
SSD_CHUNK = 128
SSD_GN = SSD_GROUPS * SSD_STATE
SSD_XBC = SSD_WIDTH + 2 * SSD_GN
GROUP_COLS = SSD_WIDTH // SSD_GROUPS

ATT_HEADS = 8
ATT_KV_HEADS = 2
ATT_HEAD_DIM = 64
ATT_WIDTH = ATT_HEADS * ATT_HEAD_DIM
ATT_KV_WIDTH = ATT_KV_HEADS * ATT_HEAD_DIM
WINDOW = 128
ROPE_BASE = 10000.0
ATT_SCALE = ATT_HEAD_DIM ** -0.5

CONV_WIDTH = 512
CONV_KERNEL = 31
CONV_PAD = (CONV_KERNEL - 1) // 2

LANES = 128
SUBLANES = 8
VMEM_LIMIT = 56 * 1024 * 1024

ROW_TILE = 256
ATT_TILE = 256
CONF_TILE = 256
CONF_HALO = 16
CONF_SUB = 32
SSD_HALO = 8


def _dot(a, b):
    return jnp.dot(a, b, preferred_element_type=F32)


def _dot_nt(a, b):
    return lax.dot_general(a, b, (((1,), (1,)), ((), ())), preferred_element_type=F32)


def _split3(x):
    hi = x.astype(BF16)
    r1 = x - hi.astype(F32)
    mid = r1.astype(BF16)
    lo = (r1 - mid.astype(F32)).astype(BF16)
    return hi, mid, lo


def _pack3(x, width):
    hi, mid, lo = _split3(x)
    packed = hi.astype(F32) + pltpu.roll(mid.astype(F32), width, 1) + pltpu.roll(lo.astype(F32), 2 * width, 1)
    return packed.astype(BF16)


def _shift_rows(x, j):
    rows, w = x.shape
    g = x.reshape(rows // SUBLANES, SUBLANES, w)
    r = pltpu.roll(g, SUBLANES - j, 1)
    sub = lax.broadcasted_iota(jnp.int32, (1, SUBLANES, w), 1)
    return jnp.where(sub < SUBLANES - j, r[:-1], r[1:]).reshape(rows - SUBLANES, w)


def _cparams(*sem):
    return pltpu.CompilerParams(dimension_semantics=sem, vmem_limit_bytes=VMEM_LIMIT)


def _mod_kernel(c_ref, w_ref, b_ref, o_ref):
    a = jax.nn.silu(c_ref[...]).astype(BF16)
    o_ref[...] = _dot(a, w_ref[...].astype(BF16)) + b_ref[...]


def _modulation(cond, w_mod, b_mod):
    depth, d, n = w_mod.shape
    rows = cond.shape[0]
    tn = 512
    return pl.pallas_call(
        _mod_kernel,
        grid=(depth, n // tn),
        in_specs=[
            pl.BlockSpec((rows, d), lambda l, j: (0, 0)),
            pl.BlockSpec((None, d, tn), lambda l, j: (l, 0, j)),
            pl.BlockSpec((None, 1, tn), lambda l, j: (l, 0, j)),
        ],
        out_specs=pl.BlockSpec((None, rows, tn), lambda l, j: (l, 0, j)),
        out_shape=jax.ShapeDtypeStruct((depth, rows, n), F32),
        compiler_params=_cparams("arbitrary", "arbitrary"),
        name="modulation",
    )(cond, w_mod, b_mod.reshape(depth, 1, n))


def _rope(t, cos, sin):
    width = t.shape[-1]
    lane = lax.broadcasted_iota(jnp.int32, t.shape, 1)
    first = (lane % 32) < 16
    rot = jnp.where(first, pltpu.roll(t, width - 16, 1), pltpu.roll(t, 16, 1))
    return t * cos + rot * sin


def _inproj_kernel(x_ref, mod_ref, nw_ref, cos_ref, sin_ref,
                   wxbc, wz, wdt, wq, wk, wv, wag, wglu, wcg,
                   oxbc, oz, odt, oq, ok, ov, oag, oglu, ocg):
    x = x_ref[...]
    y = x * lax.rsqrt(jnp.mean(x * x, axis=-1, keepdims=True) + EPS) * nw_ref[...]
    h = (y * (1.0 + mod_ref[1:2, :]) + mod_ref[0:1, :]).astype(BF16)
    oxbc[...] = _dot(h, wxbc[...])
    oz[...] = _dot(h, wz[...])
    odt[...] = _dot(h, wdt[...])
    oag[...] = _dot(h, wag[...])
    oglu[...] = _dot(h, wglu[...])
    ocg[...] = _dot(h, wcg[...])
    ov[...] = _dot(h, wv[...]).astype(BF16)
    cos = cos_ref[...]
    sin = sin_ref[...]
    ok[...] = _rope(_dot(h, wk[...]), cos, sin).astype(BF16)
    reps = ATT_WIDTH // LANES
    cos_q = jnp.concatenate([cos] * reps, axis=1)
    sin_q = jnp.concatenate([sin] * reps, axis=1)
    oq[...] = (_rope(_dot(h, wq[...]), cos_q, sin_q) * ATT_SCALE).astype(BF16)


def _inproj(xa, modsel, norm_w, cos_t, sin_t, weights, nct):
    bsz, t, d = xa.shape
    tm = ROW_TILE
    widths = [w.shape[1] for w in weights]
    dtypes = [F32, F32, F32, BF16, BF16, BF16, F32, F32, F32]
    row_map = lambda b, i: (b, i, 0)
    const2 = lambda b, i: (0, 0)
    in_specs = [
        pl.BlockSpec((None, tm, d), row_map),
        pl.BlockSpec((None, None, 3, d), lambda b, i: (b, jnp.where(i >= nct, 1, 0), 0, 0)),
        pl.BlockSpec((1, d), const2),
        pl.BlockSpec((tm, LANES), lambda b, i: (i, 0)),
        pl.BlockSpec((tm, LANES), lambda b, i: (i, 0)),
    ] + [pl.BlockSpec((d, w), const2) for w in widths]
    out_specs = [pl.BlockSpec((None, tm, w), row_map) for w in widths]
    out_shape = [jax.ShapeDtypeStruct((bsz, t, w), dt) for w, dt in zip(widths, dtypes)]
    return pl.pallas_call(
        _inproj_kernel,
        grid=(bsz, t // tm),
        in_specs=in_specs,
        out_specs=out_specs,
        out_shape=out_shape,
        compiler_params=_cparams("parallel", "arbitrary"),
        name="inproj",
    )(xa, modsel, norm_w, cos_t, sin_t, *weights)


def _ssd_chunk_of(s, ncc, nc):
    bwd = jnp.where(s < ncc, ncc - 1 - s, nc + ncc - 1 - s)
    return jnp.where(s < nc, bwd, s - nc)


def _ssd_kernel(xm_ref, xp_ref, xn_ref, dt_ref, z_ref, cw_ref, cb_ref, dtb_ref, alog_ref, dvec_ref, nw_ref,
                o_ref, yb_ref, st_ref, xs_ref, bc_ref, *, ncc, nc):
    q = SSD_CHUNK
    s = pl.program_id(1)
    fwd = s >= nc
    c = _ssd_chunk_of(s, ncc, nc)

    @pl.when((s == 0) | (s == nc))
    def _():
        st_ref[...] = jnp.zeros_like(st_ref)

    row0 = pl.multiple_of(c * q, q)

    @pl.when(jnp.logical_not(fwd))
    def _():
        first = (c == 0) | (c == ncc)
        last = (c == ncc - 1) | (c == nc - 1)
        xp = jnp.where(first, 0.0, xp_ref[...])
        xn = jnp.where(last, 0.0, xn_ref[...])
        xext = jnp.concatenate([xp, xm_ref[...], xn], axis=0)
        acc = cb_ref[...]
        for k in range(SSD_CONV):
            d = k - (SSD_CONV - 1) // 2
            if d == 0:
                tap = xext[SSD_HALO:SSD_HALO + q]
            elif d > 0:
                tap = _shift_rows(xext, d)[SSD_HALO:SSD_HALO + q]
            else:
                tap = _shift_rows(xext, SSD_HALO + d)[:q]
            acc = acc + cw_ref[k:k + 1, :] * tap
        u = jax.nn.silu(acc)
        xs_ref[pl.ds(row0, q), :] = u[:, :SSD_WIDTH]
        bc_ref[pl.ds(row0, q), :] = u[:, SSD_WIDTH:].astype(BF16)

    xs = xs_ref[pl.ds(row0, q), :]
    bc = bc_ref[pl.ds(row0, q), :]
    bb = bc[:, :SSD_GN]
    cb16 = bc[:, SSD_GN:]

    lane = lax.broadcasted_iota(jnp.int32, (q, LANES), 1)
    head_lane = lane < SSD_HEADS
    dt = jnp.where(head_lane, jax.nn.softplus(dt_ref[...] + dtb_ref[...]), 0.0)
    da = dt * (-jnp.exp(alog_ref[...]))
    li = lax.broadcasted_iota(jnp.int32, (q, q), 0)
    si = lax.broadcasted_iota(jnp.int32, (q, q), 1)
    causal = (si - li) * jnp.where(fwd, -1, 1) >= 0
    cs3 = _dot(jnp.where(causal, 1.0, 0.0).astype(BF16), _pack3(da, SSD_HEADS))
    cs = jnp.where(head_lane, cs3 + pltpu.roll(cs3, LANES - SSD_HEADS, 1) + pltpu.roll(cs3, LANES - 2 * SSD_HEADS, 1), 0.0)
    total = jnp.sum(da, axis=0, keepdims=True)
    to_end = jnp.exp(total - cs)
    from_start = jnp.where(head_lane, jnp.exp(cs), 0.0)
    chunk_decay = jnp.where(head_lane[:SUBLANES], jnp.exp(total), 0.0)
    cs_t = cs.T

    er = lax.broadcasted_iota(jnp.int32, (LANES, SSD_WIDTH), 0)
    ec = lax.broadcasted_iota(jnp.int32, (LANES, SSD_WIDTH), 1)
    spread = jnp.where((ec // SSD_HEAD_DIM == er % SSD_HEADS) & (er < 3 * SSD_HEADS), 1.0, 0.0).astype(BF16)
    dt_e = _dot(_pack3(dt, SSD_HEADS), spread)
    stacked = jnp.concatenate([dt * to_end, from_start, chunk_decay], axis=0)
    stacked_e = _dot(_pack3(stacked, SSD_HEADS), spread)
    w_e = stacked_e[:q]
    fs_e = stacked_e[q:2 * q]
    cd_e = stacked_e[2 * q:2 * q + 1]

    xd = (xs * dt_e).astype(BF16)
    xdw = (xs * w_e).astype(BF16)
    low = lax.broadcasted_iota(jnp.int32, (q, LANES), 1) < SSD_HEAD_DIM
    zero16 = jnp.zeros((q, LANES), BF16)
    ys = []
    for g in range(SSD_GROUPS):
        gs = slice(g * SSD_STATE, (g + 1) * SSD_STATE)
        gc = slice(g * GROUP_COLS, (g + 1) * GROUP_COLS)
        scores = _dot_nt(cb16[:, gs], bb[:, gs])
        entering = st_ref[g]
        y_off = _dot(cb16[:, gs], entering.astype(BF16))
        new_state = _dot(bb[:, gs].astype(F32).T.astype(BF16), xdw[:, gc])
        st_ref[g] = entering * cd_e[:, gc] + new_state
        for pr in range(GROUP_COLS // LANES):
            h0 = (g * GROUP_COLS + pr * LANES) // SSD_HEAD_DIM
            cols = slice(h0 * SSD_HEAD_DIM, h0 * SSD_HEAD_DIM + LANES)
            xpair = xd[:, cols]
            m0 = (scores * jnp.exp(jnp.where(causal, cs[:, h0:h0 + 1] - cs_t[h0:h0 + 1, :], -jnp.inf))).astype(BF16)
            m1 = (scores * jnp.exp(jnp.where(causal, cs[:, h0 + 1:h0 + 2] - cs_t[h0 + 1:h0 + 2, :], -jnp.inf))).astype(BF16)
            block_diag = jnp.concatenate([jnp.where(low, xpair, zero16), jnp.where(low, zero16, xpair)], axis=0)
            y_diag = _dot(jnp.concatenate([m0, m1], axis=1), block_diag)
            ys.append(y_diag + y_off[:, pr * LANES:(pr + 1) * LANES] * fs_e[:, cols])
    y = jnp.concatenate(ys, axis=1)

    @pl.when(jnp.logical_not(fwd))
    def _():
        yb_ref[pl.ds(row0, q), :] = y

    @pl.when(fwd)
    def _():
        tot = y + yb_ref[pl.ds(row0, q), :] + dvec_ref[...] * xs
        gated = tot * jax.nn.silu(z_ref[...])
        outs = []
        for g in range(SSD_GROUPS):
            gc = slice(g * GROUP_COLS, (g + 1) * GROUP_COLS)
            v = gated[:, gc]
            outs.append(v * lax.rsqrt(jnp.mean(v * v, axis=-1, keepdims=True) + EPS) * nw_ref[:, gc])
        o_ref[...] = jnp.concatenate(outs, axis=1).astype(BF16)


def _ssd(xbc, dtp, z, conv_w, conv_b, dt_bias, a_log, dvec, norm_w, ncc):
    bsz, t, _ = xbc.shape
    q = SSD_CHUNK
    nc = t // q
    hb = q // SSD_HALO
    nhalo = t // SSD_HALO
    chunk = lambda s: _ssd_chunk_of(s, ncc, nc)
    xchunk = lambda s: _ssd_chunk_of(jnp.minimum(s, nc - 1), ncc, nc)
    const2 = lambda b, s: (0, 0)
    phase = lambda s: jnp.where(s < nc, 1, 0)
    in_specs = [
        pl.BlockSpec((None, q, SSD_XBC), lambda b, s: (b, xchunk(s), 0)),
        pl.BlockSpec((None, SSD_HALO, SSD_XBC), lambda b, s: (b, jnp.maximum(xchunk(s) * hb - 1, 0), 0)),
        pl.BlockSpec((None, SSD_HALO, SSD_XBC), lambda b, s: (b, jnp.minimum(xchunk(s) * hb + hb, nhalo - 1), 0)),
        pl.BlockSpec((None, q, LANES), lambda b, s: (b, chunk(s), phase(s))),
        pl.BlockSpec((None, q, SSD_WIDTH), lambda b, s: (b, jnp.maximum(s - nc, 0), 0)),
        pl.BlockSpec((SSD_CONV, SSD_XBC), const2),
        pl.BlockSpec((1, SSD_XBC), const2),
        pl.BlockSpec((None, 1, LANES), lambda b, s: (phase(s), 0, 0)),
        pl.BlockSpec((None, 1, LANES), lambda b, s: (phase(s), 0, 0)),
        pl.BlockSpec((1, SSD_WIDTH), const2),
        pl.BlockSpec((1, SSD_WIDTH), const2),
    ]
    return pl.pallas_call(
        functools.partial(_ssd_kernel, ncc=ncc, nc=nc),
        grid=(bsz, 2 * nc),
        in_specs=in_specs,
        out_specs=pl.BlockSpec((None, q, SSD_WIDTH), lambda b, s: (b, jnp.maximum(s - nc, 0), 0)),
        out_shape=jax.ShapeDtypeStruct((bsz, t, SSD_WIDTH), BF16),
        scratch_shapes=[
            pltpu.VMEM((t, SSD_WIDTH), F32),
            pltpu.VMEM((SSD_GROUPS, SSD_STATE, GROUP_COLS), F32),
            pltpu.VMEM((t, SSD_WIDTH), F32),
            pltpu.VMEM((t, 2 * SSD_GN), BF16),
        ],
        compiler_params=_cparams("parallel", "arbitrary"),
        name="ssd",
    )(xbc, xbc, xbc, dtp, z, conv_w, conv_b, dt_bias, a_log, dvec, norm_w)


def _swap_halves(blk):
    half = blk.shape[-1] // 2
    return jnp.concatenate([blk[:, half:], blk[:, :half]], axis=1)


def _head_pads(blk, g, low):
    zero = jnp.zeros_like(blk)
    sw = _swap_halves(blk)
    if g == 0:
        return jnp.where(low, blk, zero), jnp.where(low, zero, sw)
    return jnp.where(low, sw, zero), jnp.where(low, zero, blk)


def _attn_kernel(q_ref, km_ref, kp_ref, kn_ref, vm_ref, vp_ref, vn_ref, kc_ref, vc_ref, ag_ref, sink_ref, o_ref,
                 *, tile_off, nct, seq):
    qb = ATT_TILE
    nkw = qb + 2 * WINDOW
    tile = pl.program_id(1) + tile_off
    lat0 = (tile - nct) * qb
    q = q_ref[...]
    kw = jnp.concatenate([kp_ref[...], km_ref[...], kn_ref[...]], axis=0)
    vw = jnp.concatenate([vp_ref[...], vm_ref[...], vn_ref[...]], axis=0)
    kc = kc_ref[...]
    vc = vc_ref[...]
    r = lax.broadcasted_iota(jnp.int32, (qb, nkw), 0)
    col = lax.broadcasted_iota(jnp.int32, (qb, nkw), 1)
    kpos = lat0 - WINDOW + col
    valid = (jnp.abs(col - WINDOW - r) <= WINDOW) & (kpos >= 0) & (kpos < seq) & (lat0 >= 0)
    neg = jnp.finfo(F32).min
    low_k = lax.broadcasted_iota(jnp.int32, (1, LANES), 1) < ATT_HEAD_DIM
    rep = ATT_HEADS // ATT_KV_HEADS
    pairs_per_group = rep * ATT_HEAD_DIM // LANES
    outs = []
    for g in range(ATT_KV_HEADS):
        kcs = _head_pads(kc, g, low_k)
        vcs = _head_pads(vc, g, low_k)
        kws = _head_pads(kw, g, low_k)
        vws = _head_pads(vw, g, low_k)
        for pr in range(pairs_per_group):
            p = g * pairs_per_group + pr
            qp = q[:, p * LANES:(p + 1) * LANES]
            acc = jnp.zeros((qb, LANES), F32)
            invs = []
            for j in range(2):
                head = 2 * p + j
                sink = sink_ref[head:head + 1, 0:1]
                sc = _dot_nt(qp, kcs[j])
                sw = jnp.where(valid, _dot_nt(qp, kws[j]), neg)
                m = jnp.maximum(jnp.maximum(jnp.max(sc, axis=-1, keepdims=True),
                                            jnp.max(sw, axis=-1, keepdims=True)), sink)
                pc = jnp.exp(sc - m)
                pw = jnp.exp(sw - m)
                den = (jnp.sum(pc, axis=-1, keepdims=True) + jnp.sum(pw, axis=-1, keepdims=True)
                       + jnp.exp(sink - m))
                acc = acc + _dot(pc.astype(BF16), vcs[j]) + _dot(pw.astype(BF16), vws[j])
                invs.append(1.0 / den)
            outs.append(acc * jnp.where(low_k, invs[0], invs[1]))
    o = jnp.concatenate(outs, axis=1)
    o_ref[...] = (o * jax.nn.silu(ag_ref[...])).astype(BF16)


def _attention(q, k, v, ag, sink_tab, ctx_len, tile_off):
    bsz, t, _ = q.shape
    qb = ATT_TILE
    nct = ctx_len // qb
    hb = qb // WINDOW
    ctx_hb = ctx_len // WINDOW
    n_hb = t // WINDOW
    ntiles = t // qb - tile_off
    main = lambda b, i: (b, i + tile_off, 0)
    prev = lambda b, i: (b, jnp.maximum((i + tile_off) * hb - 1, ctx_hb), 0)
    nxt = lambda b, i: (b, jnp.minimum((i + tile_off) * hb + hb, n_hb - 1), 0)
    ctx = lambda b, i: (b, 0, 0)
    kvw = ATT_KV_WIDTH
    in_specs = [
        pl.BlockSpec((None, qb, ATT_WIDTH), main),
        pl.BlockSpec((None, qb, kvw), main),
        pl.BlockSpec((None, WINDOW, kvw), prev),
        pl.BlockSpec((None, WINDOW, kvw), nxt),
        pl.BlockSpec((None, qb, kvw), main),
        pl.BlockSpec((None, WINDOW, kvw), prev),
        pl.BlockSpec((None, WINDOW, kvw), nxt),
        pl.BlockSpec((None, ctx_len, kvw), ctx),
        pl.BlockSpec((None, ctx_len, kvw), ctx),
        pl.BlockSpec((None, qb, ATT_WIDTH), main),
        pl.BlockSpec((ATT_HEADS, LANES), lambda b, i: (0, 0)),
    ]
    return pl.pallas_call(
        functools.partial(_attn_kernel, tile_off=tile_off, nct=nct, seq=t - ctx_len),
        grid=(bsz, ntiles),
        in_specs=in_specs,
        out_specs=pl.BlockSpec((None, qb, ATT_WIDTH), main),
        out_shape=jax.ShapeDtypeStruct((bsz, t, ATT_WIDTH), BF16),
        compiler_params=_cparams("parallel", "arbitrary"),
        name="attention",
    )(q, k, k, k, v, v, v, k, v, ag, sink_tab)


def _conf_kernel(gm_ref, gp_ref, gn_ref, cg_ref, dww_ref, dwb_ref, lnw_ref, lnb_ref, pww_ref, pwb_ref, o_ref,
                 hbuf, abuf, *, tile_off, nct, nt):
    tc = CONF_TILE
    tile = pl.program_id(1) + tile_off
    first = (tile == 0) | (tile == nct)
    last = (tile == nct - 1) | (tile == nt - 1)

    def glu(v):
        return v[:, :CONV_WIDTH] * jax.nn.sigmoid(v[:, CONV_WIDTH:])

    rows = tc + 2 * CONF_HALO
    ext = jnp.concatenate([jnp.where(first, 0.0, glu(gp_ref[...])), glu(gm_ref[...]),
                           jnp.where(last, 0.0, glu(gn_ref[...]))], axis=0)
    hbuf[0] = ext
    for j in range(1, SUBLANES):
        hbuf[j, 0:rows - SUBLANES, :] = _shift_rows(ext, j)
    base = CONF_HALO - CONV_PAD
    for r in range(tc // CONF_SUB):
        acc = jnp.broadcast_to(dwb_ref[...], (CONF_SUB // SUBLANES, SUBLANES, CONV_WIDTH))
        for k in range(CONV_KERNEL):
            blk, res = divmod(base + k, SUBLANES)
            strip = hbuf[res, pl.ds(r * CONF_SUB + blk * SUBLANES, CONF_SUB), :]
            acc = acc + dww_ref[k] * strip.reshape(CONF_SUB // SUBLANES, SUBLANES, CONV_WIDTH)
        acc = acc.reshape(CONF_SUB, CONV_WIDTH)
        xc = acc - jnp.mean(acc, axis=-1, keepdims=True)
        var = jnp.mean(xc * xc, axis=-1, keepdims=True)
        ln = xc * lax.rsqrt(var + EPS) * lnw_ref[...] + lnb_ref[...]
        abuf[pl.ds(r * CONF_SUB, CONF_SUB), :] = jax.nn.silu(ln)
    out = _dot(abuf[...].astype(BF16), pww_ref[...]) + pwb_ref[...]
    o_ref[...] = (out * jax.nn.silu(cg_ref[...])).astype(BF16)


def _conformer(glu, cg, dw_w, dw_b, ln_w, ln_b, pw_w, pw_b, ctx_len, tile_off):
    bsz, t, _ = glu.shape
    tc = CONF_TILE
    nt = t // tc
    nct = ctx_len // tc
    hb = tc // CONF_HALO
    nhalo = t // CONF_HALO
    main = lambda b, i: (b, i + tile_off, 0)
    const2 = lambda b, i: (0, 0)
    in_specs = [
        pl.BlockSpec((None, tc, 2 * CONV_WIDTH), main),
        pl.BlockSpec((None, CONF_HALO, 2 * CONV_WIDTH), lambda b, i: (b, jnp.maximum((i + tile_off) * hb - 1, 0), 0)),
        pl.BlockSpec((None, CONF_HALO, 2 * CONV_WIDTH),
                     lambda b, i: (b, jnp.minimum((i + tile_off) * hb + hb, nhalo - 1), 0)),
        pl.BlockSpec((None, tc, CONV_WIDTH), main),
        pl.BlockSpec((CONV_KERNEL, SUBLANES, CONV_WIDTH), lambda b, i: (0, 0, 0)),
        pl.BlockSpec((SUBLANES, CONV_WIDTH), const2),
        pl.BlockSpec((1, CONV_WIDTH), const2),
        pl.BlockSpec((1, CONV_WIDTH), const2),
        pl.BlockSpec((CONV_WIDTH, CONV_WIDTH), const2),
        pl.BlockSpec((1, CONV_WIDTH), const2),
    ]
    return pl.pallas_call(
        functools.partial(_conf_kernel, tile_off=tile_off, nct=nct, nt=nt),
        grid=(bsz, nt - tile_off),
        in_specs=in_specs,
        out_specs=pl.BlockSpec((None, tc, CONV_WIDTH), main),
        out_shape=jax.ShapeDtypeStruct((bsz, t, CONV_WIDTH), BF16),
        scratch_shapes=[
            pltpu.VMEM((SUBLANES, tc + 2 * CONF_HALO, CONV_WIDTH), F32),
            pltpu.VMEM((tc, CONV_WIDTH), F32),
        ],
        compiler_params=_cparams("parallel", "arbitrary"),
        name="conformer",
    )(glu, glu, glu, cg, dw_w, dw_b, ln_w, ln_b, pw_w, pw_b)


def _outproj_kernel(x_ref, os_ref, oa_ref, oc_ref, mod_ref, ws_ref, wa_ref, wc_ref, fnw_ref, o_ref, *, final):
    u = _dot(os_ref[...], ws_ref[...]) + _dot(oa_ref[...], wa_ref[...]) + _dot(oc_ref[...], wc_ref[...])
    xn = x_ref[...] + mod_ref[2:3, :] * u
    if final:
        xn = xn * lax.rsqrt(jnp.mean(xn * xn, axis=-1, keepdims=True) + EPS) * fnw_ref[...]
    o_ref[...] = xn


def _outproj(xa, o_ssd, o_att, o_conv, modsel, w_s, w_a, w_c, final_norm_w, nct, final):
    bsz, t, d = xa.shape
    tm = ROW_TILE
    tile_off = nct if final else 0
    main = lambda b, i: (b, i + tile_off, 0)
    const2 = lambda b, i: (0, 0)
    in_specs = [
        pl.BlockSpec((None, tm, d), main),
        pl.BlockSpec((None, tm, SSD_WIDTH), main),
        pl.BlockSpec((None, tm, ATT_WIDTH), main),
        pl.BlockSpec((None, tm, CONV_WIDTH), main),
        pl.BlockSpec((None, None, 3, d), lambda b, i: (b, jnp.where(i + tile_off >= nct, 1, 0), 0, 0)),
        pl.BlockSpec((SSD_WIDTH, d), const2),
        pl.BlockSpec((ATT_WIDTH, d), const2),
        pl.BlockSpec((CONV_WIDTH, d), const2),
        pl.BlockSpec((1, d), const2),
    ]
    rows_out = t - tile_off * tm
    return pl.pallas_call(
        functools.partial(_outproj_kernel, final=final),
        grid=(bsz, rows_out // tm),
        in_specs=in_specs,
        out_specs=pl.BlockSpec((None, tm, d), lambda b, i: (b, i, 0)),
        out_shape=jax.ShapeDtypeStruct((bsz, rows_out, d), F32),
        compiler_params=_cparams("parallel", "arbitrary"),
        name="outproj_final" if final else "outproj",
    )(xa, o_ssd, o_att, o_conv, modsel, w_s, w_a, w_c, final_norm_w)


def _rope_tables(seq, ctx_len):
    rows = seq // GRID_W
    row = jnp.repeat(jnp.arange(rows, dtype=F32), GRID_W)
    col = jnp.tile(jnp.arange(GRID_W, dtype=F32), rows)
    axis_dim = ATT_HEAD_DIM // 2
    inv_freq = ROPE_BASE ** (-jnp.arange(0, axis_dim, 2, dtype=F32) / axis_dim)
    ar = row[:, None] * inv_freq
    ac = col[:, None] * inv_freq
    cos = jnp.concatenate([jnp.cos(ar), jnp.cos(ar), jnp.cos(ac), jnp.cos(ac)], axis=1)
    sin = jnp.concatenate([-jnp.sin(ar), jnp.sin(ar), -jnp.sin(ac), jnp.sin(ac)], axis=1)
    reps = LANES // ATT_HEAD_DIM
    cos = jnp.concatenate([jnp.ones((ctx_len, LANES), F32), jnp.tile(cos, (1, reps))], axis=0)
    sin = jnp.concatenate([jnp.zeros((ctx_len, LANES), F32), jnp.tile(sin, (1, reps))], axis=0)
    return cos, sin


def _pad_lanes(a, width):
    return jnp.pad(a, [(0, 0)] * (a.ndim - 1) + [(0, width - a.shape[-1])])


def _split_w_in(w_in):
    sizes = (SSD_XBC, SSD_WIDTH, 2 * SSD_HEADS, ATT_WIDTH, ATT_KV_WIDTH, ATT_KV_WIDTH, ATT_WIDTH,
             2 * CONV_WIDTH, CONV_WIDTH)
    parts, off = [], 0
    for size in sizes:
        parts.append(w_in[:, off:off + size])
        off += size
    dt_w = parts[2]
    parts[2] = jnp.concatenate([_pad_lanes(dt_w[:, :SSD_HEADS], LANES), _pad_lanes(dt_w[:, SSD_HEADS:], LANES)], axis=1)
    return [p.astype(BF16) for p in parts]


def kernel(x, c, ctx, c_ctx, w_mod, b_mod, norm_w, w_in, ssd_conv_w, ssd_conv_b, ssd_dt_bias, ssd_a_log, ssd_d,
           ssd_norm_w, attn_sink, conv_dw_w, conv_dw_b, conv_ln_w, conv_ln_b, conv_pw_w, conv_pw_b, w_out,
           final_norm_w):
    bsz, seq, d = x.shape
    ctx_len = ctx.shape[1]
    depth = w_in.shape[0]
    assert ctx_len % ROW_TILE == 0 and seq % ROW_TILE == 0 and seq % GRID_W == 0
    assert ROW_TILE == ATT_TILE == CONF_TILE
    nct = ctx_len // ROW_TILE
    ncc = ctx_len // SSD_CHUNK

    cond_rows = -(-(bsz + 1) // SUBLANES) * SUBLANES
    cond = jnp.zeros((cond_rows, d), F32).at[:bsz].set(c).at[bsz].set(c_ctx)
    mod = _modulation(cond, w_mod, b_mod)
    cos_t, sin_t = _rope_tables(seq, ctx_len)
    xa = jnp.concatenate([ctx, x], axis=1)

    out = None
    for l in range(depth):
        final = l == depth - 1
        lat = mod[l, :bsz]
        ctxm = jnp.broadcast_to(mod[l, bsz], lat.shape)
        modsel = jnp.stack([ctxm, lat], axis=1).reshape(bsz, 2, 3, d)
        weights = _split_w_in(w_in[l])
        xbc, z, dtp, q, k, v, ag, glu, cg = _inproj(xa, modsel, norm_w[l][None], cos_t, sin_t, weights, nct)

        o_ssd = _ssd(xbc, dtp, z, ssd_conv_w[l], ssd_conv_b[l][None],
                     _pad_lanes(ssd_dt_bias[l], LANES)[:, None], _pad_lanes(ssd_a_log[l], LANES)[:, None],
                     jnp.repeat(ssd_d[l], SSD_HEAD_DIM)[None], ssd_norm_w[l][None], ncc)
        tile_off = nct if final else 0
        sink_tab = jnp.broadcast_to(attn_sink[l][:, None], (ATT_HEADS, LANES))
        o_att = _attention(q, k, v, ag, sink_tab, ctx_len, tile_off)
        dw_w = jnp.broadcast_to(conv_dw_w[l][:, None, :], (CONV_KERNEL, SUBLANES, CONV_WIDTH))
        dw_b = jnp.broadcast_to(conv_dw_b[l][None], (SUBLANES, CONV_WIDTH))
        o_conv = _conformer(glu, cg, dw_w, dw_b, conv_ln_w[l][None], conv_ln_b[l][None],
                            conv_pw_w[l].astype(BF16), conv_pw_b[l][None], ctx_len, tile_off)
        w_o = w_out[l].astype(BF16)
        out = _outproj(xa, o_ssd, o_att, o_conv, modsel, w_o[:SSD_WIDTH], w_o[SSD_WIDTH:SSD_WIDTH + ATT_WIDTH],
                       w_o[SSD_WIDTH + ATT_WIDTH:], final_norm_w[None], nct, final)
        xa = out
    return out
```

```python
import functools

import jax
import jax.numpy as jnp
from jax import lax
from jax.experimental import pallas as pl
from jax.experimental.pallas import tpu as pltpu

F32 = jnp.float32
BF16 = jnp.bfloat16

GRID_W = 64
EPS = 1e-6

SSD_HEADS = 16
SSD_HEAD_DIM = 64
SSD_WIDTH = SSD_HEADS * SSD_HEAD_DIM
SSD_GROUPS = 2
SSD_STATE = 128
SSD_CONV = 5
SSD_CHUNK = 128
SSD_GN = SSD_GROUPS * SSD_STATE
SSD_XBC = SSD_WIDTH + 2 * SSD_GN
GROUP_COLS = SSD_WIDTH // SSD_GROUPS

ATT_HEADS = 8
ATT_KV_HEADS = 2
ATT_HEAD_DIM = 64
ATT_WIDTH = ATT_HEADS * ATT_HEAD_DIM
ATT_KV_WIDTH = ATT_KV_HEADS * ATT_HEAD_DIM
WINDOW = 128
ROPE_BASE = 10000.0
ATT_SCALE = ATT_HEAD_DIM ** -0.5

CONV_WIDTH = 512
CONV_KERNEL = 31
CONV_PAD = (CONV_KERNEL - 1) // 2

LANES = 128
SUBLANES = 8
VMEM_LIMIT = 56 * 1024 * 1024

ROW_TILE = 256
ATT_TILE = 256
CONF_TILE = 256
CONF_HALO = 16
CONF_SUB = 32
SSD_HALO = SUBLANES
SSD_STEP = 2 * SSD_CHUNK


def _dot(a, b):
    return jnp.dot(a, b, preferred_element_type=F32)


def _dot_nt(a, b):
    return lax.dot_general(a, b, (((1,), (1,)), ((), ())), preferred_element_type=F32)


def _split3(x):
    hi = x.astype(BF16)
    r1 = x - hi.astype(F32)
    mid = r1.astype(BF16)
    lo = (r1 - mid.astype(F32)).astype(BF16)
    return hi, mid, lo


def _pack3(x, width):
    hi, mid, lo = _split3(x)
    packed = hi.astype(F32) + pltpu.roll(mid.astype(F32), width, 1) + pltpu.roll(lo.astype(F32), 2 * width, 1)
    return packed.astype(BF16)


def _shift_rows(x, j):
    rows, w = x.shape
    g = x.reshape(rows // SUBLANES, SUBLANES, w)
    r = pltpu.roll(g, SUBLANES - j, 1)
    sub = lax.broadcasted_iota(jnp.int32, (1, SUBLANES, w), 1)
    return jnp.where(sub < SUBLANES - j, r[:-1], r[1:]).reshape(rows - SUBLANES, w)


def _cparams(*sem):
    return pltpu.CompilerParams(dimension_semantics=sem, vmem_limit_bytes=VMEM_LIMIT)


def _mod_kernel(c_ref, w_ref, b_ref, o_ref):
    a = jax.nn.silu(c_ref[...]).astype(BF16)
    o_ref[...] = _dot(a, w_ref[...].astype(BF16)) + b_ref[...]


def _modulation(cond, w_mod, b_mod):
    depth, d, n = w_mod.shape
    rows = cond.shape[0]
    tn = 512
    return pl.pallas_call(
        _mod_kernel,
        grid=(depth, n // tn),
        in_specs=[
            pl.BlockSpec((rows, d), lambda l, j: (0, 0)),
            pl.BlockSpec((None, d, tn), lambda l, j: (l, 0, j)),
            pl.BlockSpec((None, 1, tn), lambda l, j: (l, 0, j)),
        ],
        out_specs=pl.BlockSpec((None, rows, tn), lambda l, j: (l, 0, j)),
        out_shape=jax.ShapeDtypeStruct((depth, rows, n), F32),
        compiler_params=_cparams("arbitrary", "arbitrary"),
        name="modulation",
    )(cond, w_mod, b_mod.reshape(depth, 1, n))


def _rope(t, cos, sin):
    width = t.shape[-1]
    lane = lax.broadcasted_iota(jnp.int32, t.shape, 1)
    first = (lane % 32) < 16
    rot = jnp.where(first, pltpu.roll(t, width - 16, 1), pltpu.roll(t, 16, 1))
    return t * cos + rot * sin


def _inproj_kernel(x_ref, xp_ref, xn_ref, mod_ref, nw_ref, cos_ref, sin_ref, cw_ref, cb_ref,
                   wxbc, wz, wdt, wq, wk, wv, wag, wglu, wcg,
                   oxs, obc, oz, odt, oq, ok, ov, oag, oglu, ocg, *, nct, nt):
    tm = x_ref.shape[0]
    tile = pl.program_id(1)

    def normed(x):
        y = x * lax.rsqrt(jnp.mean(x * x, axis=-1, keepdims=True) + EPS) * nw_ref[...]
        return (y * (1.0 + mod_ref[1:2, :]) + mod_ref[0:1, :]).astype(BF16)

    h_all = normed(jnp.concatenate([x_ref[...], xp_ref[...], xn_ref[...]], axis=0))
    h = h_all[:tm]

    first = (tile == 0) | (tile == nct)
    last = (tile == nct - 1) | (tile == nt - 1)
    xbc = _dot(h_all, wxbc[...])
    xext = jnp.concatenate([jnp.where(first, 0.0, xbc[tm:tm + SSD_HALO]), xbc[:tm],
                            jnp.where(last, 0.0, xbc[tm + SSD_HALO:])], axis=0)
    acc = cb_ref[...]
    for k in range(SSD_CONV):
        d = k - (SSD_CONV - 1) // 2
        if d == 0:
            tap = xext[SSD_HALO:SSD_HALO + tm]
        elif d > 0:
            tap = _shift_rows(xext, d)[SSD_HALO:SSD_HALO + tm]
        else:
            tap = _shift_rows(xext, SSD_HALO + d)[:tm]
        acc = acc + cw_ref[k:k + 1, :] * tap
    u = jax.nn.silu(acc)
    oxs[...] = u[:, :SSD_WIDTH]
    obc[...] = u[:, SSD_WIDTH:].astype(BF16)

    oz[...] = _dot(h, wz[...])
    odt[...] = _dot(h, wdt[...])
    oag[...] = _dot(h, wag[...])
    oglu[...] = _dot(h, wglu[...])
    ocg[...] = _dot(h, wcg[...])
    ov[...] = _dot(h, wv[...]).astype(BF16)
    cos = cos_ref[...]
    sin = sin_ref[...]
    ok[...] = _rope(_dot(h, wk[...]), cos, sin).astype(BF16)
    reps = ATT_WIDTH // LANES
    cos_q = jnp.concatenate([cos] * reps, axis=1)
    sin_q = jnp.concatenate([sin] * reps, axis=1)
    oq[...] = (_rope(_dot(h, wq[...]), cos_q, sin_q) * ATT_SCALE).astype(BF16)


def _inproj(xa, modsel, norm_w, cos_t, sin_t, conv_w, conv_b, weights, nct):
    bsz, t, d = xa.shape
    tm = ROW_TILE
    nt = t // tm
    hb = tm // SSD_HALO
    nhalo = t // SSD_HALO
    widths = [SSD_WIDTH, 2 * SSD_GN] + [w.shape[1] for w in weights[1:]]
    dtypes = [F32, BF16, F32, F32, BF16, BF16, BF16, F32, F32, F32]
    row_map = lambda b, i: (b, i, 0)
    const2 = lambda b, i: (0, 0)
    in_specs = [
        pl.BlockSpec((None, tm, d), row_map),
        pl.BlockSpec((None, SSD_HALO, d), lambda b, i: (b, jnp.maximum(i * hb - 1, 0), 0)),
        pl.BlockSpec((None, SSD_HALO, d), lambda b, i: (b, jnp.minimum(i * hb + hb, nhalo - 1), 0)),
        pl.BlockSpec((None, None, 3, d), lambda b, i: (b, jnp.where(i >= nct, 1, 0), 0, 0)),
        pl.BlockSpec((1, d), const2),
        pl.BlockSpec((tm, LANES), lambda b, i: (i, 0)),
        pl.BlockSpec((tm, LANES), lambda b, i: (i, 0)),
        pl.BlockSpec((SSD_CONV, SSD_XBC), const2),
        pl.BlockSpec((1, SSD_XBC), const2),
    ] + [pl.BlockSpec((d, w.shape[1]), const2) for w in weights]
    out_specs = [pl.BlockSpec((None, tm, w), row_map) for w in widths]
    out_shape = [jax.ShapeDtypeStruct((bsz, t, w), dt) for w, dt in zip(widths, dtypes)]
    return pl.pallas_call(
        functools.partial(_inproj_kernel, nct=nct, nt=nt),
        grid=(bsz, nt),
        in_specs=in_specs,
        out_specs=out_specs,
        out_shape=out_shape,
        compiler_params=_cparams("parallel", "arbitrary"),
        name="inproj",
    )(xa, xa, xa, modsel, norm_w, cos_t, sin_t, conv_w, conv_b, *weights)


def _ssd_chunk_of(s, ncc, nc):
    bwd = jnp.where(s < ncc, ncc - 1 - s, nc + ncc - 1 - s)
    return jnp.where(s < nc, bwd, s - nc)


def _ssd_chunk(xs, bc, dt_raw, dtb, alog, fwd, st_ref):
    q = SSD_CHUNK
    bb = bc[:, :SSD_GN]
    cb16 = bc[:, SSD_GN:]

    lane = lax.broadcasted_iota(jnp.int32, (q, LANES), 1)
    head_lane = lane < SSD_HEADS
    dt = jnp.where(head_lane, jax.nn.softplus(dt_raw + dtb), 0.0)
    da = dt * (-jnp.exp(alog))
    li = lax.broadcasted_iota(jnp.int32, (q, q), 0)
    si = lax.broadcasted_iota(jnp.int32, (q, q), 1)
    causal = (si - li) * jnp.where(fwd, -1, 1) >= 0
    cs3 = _dot(jnp.where(causal, 1.0, 0.0).astype(BF16), _pack3(da, SSD_HEADS))
    cs = jnp.where(head_lane, cs3 + pltpu.roll(cs3, LANES - SSD_HEADS, 1) + pltpu.roll(cs3, LANES - 2 * SSD_HEADS, 1), 0.0)
    total = jnp.sum(da, axis=0, keepdims=True)
    to_end = jnp.exp(total - cs)
    from_start = jnp.where(head_lane, jnp.exp(cs), 0.0)
    chunk_decay = jnp.where(head_lane[:SUBLANES], jnp.exp(total), 0.0)
    cs_t = cs.T

    er = lax.broadcasted_iota(jnp.int32, (LANES, SSD_WIDTH), 0)
    ec = lax.broadcasted_iota(jnp.int32, (LANES, SSD_WIDTH), 1)
    spread = jnp.where((ec // SSD_HEAD_DIM == er % SSD_HEADS) & (er < 3 * SSD_HEADS), 1.0, 0.0).astype(BF16)
    dt_e = _dot(_pack3(dt, SSD_HEADS), spread)
    stacked = jnp.concatenate([dt * to_end, from_start, chunk_decay], axis=0)
    stacked_e = _dot(_pack3(stacked, SSD_HEADS), spread)
    w_e = stacked_e[:q]
    fs_e = stacked_e[q:2 * q]
    cd_e = stacked_e[2 * q:2 * q + 1]

    xd = (xs * dt_e).astype(BF16)
    xdw = (xs * w_e).astype(BF16)
    low = lax.broadcasted_iota(jnp.int32, (q, LANES), 1) < SSD_HEAD_DIM
    zero16 = jnp.zeros((q, LANES), BF16)
    ys = []
    for g in range(SSD_GROUPS):
        gs = slice(g * SSD_STATE, (g + 1) * SSD_STATE)
        gc = slice(g * GROUP_COLS, (g + 1) * GROUP_COLS)
        scores = _dot_nt(cb16[:, gs], bb[:, gs])
        entering = st_ref[g]
        y_off = _dot(cb16[:, gs], entering.astype(BF16))
        new_state = _dot(bb[:, gs].astype(F32).T.astype(BF16), xdw[:, gc])
        st_ref[g] = entering * cd_e[:, gc] + new_state
        for pr in range(GROUP_COLS // LANES):
            h0 = (g * GROUP_COLS + pr * LANES) // SSD_HEAD_DIM
            cols = slice(h0 * SSD_HEAD_DIM, h0 * SSD_HEAD_DIM + LANES)
            xpair = xd[:, cols]
            m0 = (scores * jnp.exp(jnp.where(causal, cs[:, h0:h0 + 1] - cs_t[h0:h0 + 1, :], -jnp.inf))).astype(BF16)
            m1 = (scores * jnp.exp(jnp.where(causal, cs[:, h0 + 1:h0 + 2] - cs_t[h0 + 1:h0 + 2, :], -jnp.inf))).astype(BF16)
            block_diag = jnp.concatenate([jnp.where(low, xpair, zero16), jnp.where(low, zero16, xpair)], axis=0)
            y_diag = _dot(jnp.concatenate([m0, m1], axis=1), block_diag)
            ys.append(y_diag + y_off[:, pr * LANES:(pr + 1) * LANES] * fs_e[:, cols])
    return jnp.concatenate(ys, axis=1)


def _ssd_kernel(xs_ref, bc_ref, dt_ref, z_ref, dtb_ref, alog_ref, dvec_ref, nw_ref, o_ref, yb_ref, st_ref,
                *, ncb, nb):
    q = SSD_CHUNK
    n_sub = SSD_STEP // q
    s = pl.program_id(1)
    fwd = s >= nb
    blk = _ssd_chunk_of(s, ncb, nb)

    @pl.when((s == 0) | (s == nb))
    def _():
        st_ref[...] = jnp.zeros_like(st_ref)

    offs, ys = [], []
    for j in range(n_sub):
        off = pl.multiple_of(jnp.where(fwd, j, n_sub - 1 - j) * q, q)
        offs.append(off)
        ys.append(_ssd_chunk(xs_ref[pl.ds(off, q), :], bc_ref[pl.ds(off, q), :], dt_ref[pl.ds(off, q), :],
                             dtb_ref[...], alog_ref[...], fwd, st_ref))

    @pl.when(jnp.logical_not(fwd))
    def _():
        for off, y in zip(offs, ys):
            yb_ref[pl.ds(pl.multiple_of(blk * SSD_STEP + off, q), q), :] = y

    @pl.when(fwd)
    def _():
        for off, y in zip(offs, ys):
            rows = pl.ds(off, q)
            tot = y + yb_ref[pl.ds(pl.multiple_of(blk * SSD_STEP + off, q), q), :] + dvec_ref[...] * xs_ref[rows, :]
            gated = tot * jax.nn.silu(z_ref[rows, :])
            outs = []
            for g in range(SSD_GROUPS):
                gc = slice(g * GROUP_COLS, (g + 1) * GROUP_COLS)
                v = gated[:, gc]
                outs.append(v * lax.rsqrt(jnp.mean(v * v, axis=-1, keepdims=True) + EPS) * nw_ref[:, gc])
            o_ref[rows, :] = jnp.concatenate(outs, axis=1).astype(BF16)


def _ssd(xs, bc, dtp, z, dt_bias, a_log, dvec, norm_w, ctx_len):
    bsz, t, _ = xs.shape
    rows = SSD_STEP
    nb = t // rows
    ncb = ctx_len // rows
    block = lambda s: _ssd_chunk_of(s, ncb, nb)
    const2 = lambda b, s: (0, 0)
    phase = lambda s: jnp.where(s < nb, 1, 0)
    fwd_block = lambda b, s: (b, jnp.maximum(s - nb, 0), 0)
    in_specs = [
        pl.BlockSpec((None, rows, SSD_WIDTH), lambda b, s: (b, block(s), 0)),
        pl.BlockSpec((None, rows, 2 * SSD_GN), lambda b, s: (b, block(s), 0)),
        pl.BlockSpec((None, rows, LANES), lambda b, s: (b, block(s), phase(s))),
        pl.BlockSpec((None, rows, SSD_WIDTH), fwd_block),
        pl.BlockSpec((None, 1, LANES), lambda b, s: (phase(s), 0, 0)),
        pl.BlockSpec((None, 1, LANES), lambda b, s: (phase(s), 0, 0)),
        pl.BlockSpec((1, SSD_WIDTH), const2),
        pl.BlockSpec((1, SSD_WIDTH), const2),
    ]
    return pl.pallas_call(
        functools.partial(_ssd_kernel, ncb=ncb, nb=nb),
        grid=(bsz, 2 * nb),
        in_specs=in_specs,
        out_specs=pl.BlockSpec((None, rows, SSD_WIDTH), fwd_block),
        out_shape=jax.ShapeDtypeStruct((bsz, t, SSD_WIDTH), BF16),
        scratch_shapes=[
            pltpu.VMEM((t, SSD_WIDTH), F32),
            pltpu.VMEM((SSD_GROUPS, SSD_STATE, GROUP_COLS), F32),
        ],
        compiler_params=_cparams("parallel", "arbitrary"),
        name="ssd",
    )(xs, bc, dtp, z, dt_bias, a_log, dvec, norm_w)


def _swap_halves(blk):
    half = blk.shape[-1] // 2
    return jnp.concatenate([blk[:, half:], blk[:, :half]], axis=1)


def _head_pads(blk, g, low):
    zero = jnp.zeros_like(blk)
    sw = _swap_halves(blk)
    if g == 0:
        return jnp.where(low, blk, zero), jnp.where(low, zero, sw)
    return jnp.where(low, sw, zero), jnp.where(low, zero, blk)


def _attn_kernel(q_ref, km_ref, kp_ref, kn_ref, vm_ref, vp_ref, vn_ref, kc_ref, vc_ref, ag_ref, sink_ref, o_ref,
                 *, tile_off, nct, seq):
    qb = ATT_TILE
    nkw = qb + 2 * WINDOW
    tile = pl.program_id(1) + tile_off
    lat0 = (tile - nct) * qb
    q = q_ref[...]
    kw = jnp.concatenate([kp_ref[...], km_ref[...], kn_ref[...]], axis=0)
    vw = jnp.concatenate([vp_ref[...], vm_ref[...], vn_ref[...]], axis=0)
    kc = kc_ref[...]
    vc = vc_ref[...]
    r = lax.broadcasted_iota(jnp.int32, (qb, nkw), 0)
    col = lax.broadcasted_iota(jnp.int32, (qb, nkw), 1)
    kpos = lat0 - WINDOW + col
    valid = (jnp.abs(col - WINDOW - r) <= WINDOW) & (kpos >= 0) & (kpos < seq) & (lat0 >= 0)
    neg = jnp.finfo(F32).min
    low_k = lax.broadcasted_iota(jnp.int32, (1, LANES), 1) < ATT_HEAD_DIM
    rep = ATT_HEADS // ATT_KV_HEADS
    pairs_per_group = rep * ATT_HEAD_DIM // LANES
    outs = []
    for g in range(ATT_KV_HEADS):
        kcs = _head_pads(kc, g, low_k)
        vcs = _head_pads(vc, g, low_k)
        kws = _head_pads(kw, g, low_k)
        vws = _head_pads(vw, g, low_k)
        for pr in range(pairs_per_group):
            p = g * pairs_per_group + pr
            qp = q[:, p * LANES:(p + 1) * LANES]
            acc = jnp.zeros((qb, LANES), F32)
            invs = []
            for j in range(2):
                head = 2 * p + j
                sink = sink_ref[head:head + 1, 0:1]
                sc = _dot_nt(qp, kcs[j])
                sw = jnp.where(valid, _dot_nt(qp, kws[j]), neg)
                m = jnp.maximum(jnp.maximum(jnp.max(sc, axis=-1, keepdims=True),
                                            jnp.max(sw, axis=-1, keepdims=True)), sink)
                pc = jnp.exp(sc - m)
                pw = jnp.exp(sw - m)
                den = (jnp.sum(pc, axis=-1, keepdims=True) + jnp.sum(pw, axis=-1, keepdims=True)
                       + jnp.exp(sink - m))
                acc = acc + _dot(pc.astype(BF16), vcs[j]) + _dot(pw.astype(BF16), vws[j])
                invs.append(1.0 / den)
            outs.append(acc * jnp.where(low_k, invs[0], invs[1]))
    o = jnp.concatenate(outs, axis=1)
    o_ref[...] = (o * jax.nn.silu(ag_ref[...])).astype(BF16)


def _attention(q, k, v, ag, sink_tab, ctx_len, tile_off):
    bsz, t, _ = q.shape
    qb = ATT_TILE
    nct = ctx_len // qb
    hb = qb // WINDOW
    ctx_hb = ctx_len // WINDOW
    n_hb = t // WINDOW
    ntiles = t // qb - tile_off
    main = lambda b, i: (b, i + tile_off, 0)
    prev = lambda b, i: (b, jnp.maximum((i + tile_off) * hb - 1, ctx_hb), 0)
    nxt = lambda b, i: (b, jnp.minimum((i + tile_off) * hb + hb, n_hb - 1), 0)
    ctx = lambda b, i: (b, 0, 0)
    kvw = ATT_KV_WIDTH
    in_specs = [
        pl.BlockSpec((None, qb, ATT_WIDTH), main),
        pl.BlockSpec((None, qb, kvw), main),
        pl.BlockSpec((None, WINDOW, kvw), prev),
        pl.BlockSpec((None, WINDOW, kvw), nxt),
        pl.BlockSpec((None, qb, kvw), main),
        pl.BlockSpec((None, WINDOW, kvw), prev),
        pl.BlockSpec((None, WINDOW, kvw), nxt),
        pl.BlockSpec((None, ctx_len, kvw), ctx),
        pl.BlockSpec((None, ctx_len, kvw), ctx),
        pl.BlockSpec((None, qb, ATT_WIDTH), main),
        pl.BlockSpec((ATT_HEADS, LANES), lambda b, i: (0, 0)),
    ]
    return pl.pallas_call(
        functools.partial(_attn_kernel, tile_off=tile_off, nct=nct, seq=t - ctx_len),
        grid=(bsz, ntiles),
        in_specs=in_specs,
        out_specs=pl.BlockSpec((None, qb, ATT_WIDTH), main),
        out_shape=jax.ShapeDtypeStruct((bsz, t, ATT_WIDTH), BF16),
        compiler_params=_cparams("parallel", "arbitrary"),
        name="attention",
    )(q, k, k, k, v, v, v, k, v, ag, sink_tab)


def _conf_kernel(gm_ref, gp_ref, gn_ref, cg_ref, dww_ref, dwb_ref, lnw_ref, lnb_ref, pww_ref, pwb_ref, o_ref,
                 hbuf, abuf, *, tile_off, nct, nt):
    tc = CONF_TILE
    tile = pl.program_id(1) + tile_off
    first = (tile == 0) | (tile == nct)
    last = (tile == nct - 1) | (tile == nt - 1)

    def glu(v):
        return v[:, :CONV_WIDTH] * jax.nn.sigmoid(v[:, CONV_WIDTH:])

    rows = tc + 2 * CONF_HALO
    ext = jnp.concatenate([jnp.where(first, 0.0, glu(gp_ref[...])), glu(gm_ref[...]),
                           jnp.where(last, 0.0, glu(gn_ref[...]))], axis=0)
    hbuf[0] = ext
    for j in range(1, SUBLANES):
        hbuf[j, 0:rows - SUBLANES, :] = _shift_rows(ext, j)
    base = CONF_HALO - CONV_PAD
    for r in range(tc // CONF_SUB):
        acc = jnp.broadcast_to(dwb_ref[...], (CONF_SUB // SUBLANES, SUBLANES, CONV_WIDTH))
        for k in range(CONV_KERNEL):
            blk, res = divmod(base + k, SUBLANES)
            strip = hbuf[res, pl.ds(r * CONF_SUB + blk * SUBLANES, CONF_SUB), :]
            acc = acc + dww_ref[k] * strip.reshape(CONF_SUB // SUBLANES, SUBLANES, CONV_WIDTH)
        acc = acc.reshape(CONF_SUB, CONV_WIDTH)
        xc = acc - jnp.mean(acc, axis=-1, keepdims=True)
        var = jnp.mean(xc * xc, axis=-1, keepdims=True)
        ln = xc * lax.rsqrt(var + EPS) * lnw_ref[...] + lnb_ref[...]
        abuf[pl.ds(r * CONF_SUB, CONF_SUB), :] = jax.nn.silu(ln)
    out = _dot(abuf[...].astype(BF16), pww_ref[...]) + pwb_ref[...]
    o_ref[...] = (out * jax.nn.silu(cg_ref[...])).astype(BF16)


def _conformer(glu, cg, dw_w, dw_b, ln_w, ln_b, pw_w, pw_b, ctx_len, tile_off):
    bsz, t, _ = glu.shape
    tc = CONF_TILE
    nt = t // tc
    nct = ctx_len // tc
    hb = tc // CONF_HALO
    nhalo = t // CONF_HALO
    main = lambda b, i: (b, i + tile_off, 0)
    const2 = lambda b, i: (0, 0)
    in_specs = [
        pl.BlockSpec((None, tc, 2 * CONV_WIDTH), main),
        pl.BlockSpec((None, CONF_HALO, 2 * CONV_WIDTH), lambda b, i: (b, jnp.maximum((i + tile_off) * hb - 1, 0), 0)),
        pl.BlockSpec((None, CONF_HALO, 2 * CONV_WIDTH),
                     lambda b, i: (b, jnp.minimum((i + tile_off) * hb + hb, nhalo - 1), 0)),
        pl.BlockSpec((None, tc, CONV_WIDTH), main),
        pl.BlockSpec((CONV_KERNEL, SUBLANES, CONV_WIDTH), lambda b, i: (0, 0, 0)),
        pl.BlockSpec((SUBLANES, CONV_WIDTH), const2),
        pl.BlockSpec((1, CONV_WIDTH), const2),
        pl.BlockSpec((1, CONV_WIDTH), const2),
        pl.BlockSpec((CONV_WIDTH, CONV_WIDTH), const2),
        pl.BlockSpec((1, CONV_WIDTH), const2),
    ]
    return pl.pallas_call(
        functools.partial(_conf_kernel, tile_off=tile_off, nct=nct, nt=nt),
        grid=(bsz, nt - tile_off),
        in_specs=in_specs,
        out_specs=pl.BlockSpec((None, tc, CONV_WIDTH), main),
        out_shape=jax.ShapeDtypeStruct((bsz, t, CONV_WIDTH), BF16),
        scratch_shapes=[
            pltpu.VMEM((SUBLANES, tc + 2 * CONF_HALO, CONV_WIDTH), F32),
            pltpu.VMEM((tc, CONV_WIDTH), F32),
        ],
        compiler_params=_cparams("parallel", "arbitrary"),
        name="conformer",
    )(glu, glu, glu, cg, dw_w, dw_b, ln_w, ln_b, pw_w, pw_b)


def _outproj_kernel(x_ref, os_ref, oa_ref, oc_ref, mod_ref, ws_ref, wa_ref, wc_ref, fnw_ref, o_ref, *, final):
    u = _dot(os_ref[...], ws_ref[...]) + _dot(oa_ref[...], wa_ref[...]) + _dot(oc_ref[...], wc_ref[...])
    xn = x_ref[...] + mod_ref[2:3, :] * u
    if final:
        xn = xn * lax.rsqrt(jnp.mean(xn * xn, axis=-1, keepdims=True) + EPS) * fnw_ref[...]
    o_ref[...] = xn


def _outproj(xa, o_ssd, o_att, o_conv, modsel, w_s, w_a, w_c, final_norm_w, nct, final):
    bsz, t, d = xa.shape
    tm = ROW_TILE
    tile_off = nct if final else 0
    main = lambda b, i: (b, i + tile_off, 0)
    const2 = lambda b, i: (0, 0)
    in_specs = [
        pl.BlockSpec((None, tm, d), main),
        pl.BlockSpec((None, tm, SSD_WIDTH), main),
        pl.BlockSpec((None, tm, ATT_WIDTH), main),
        pl.BlockSpec((None, tm, CONV_WIDTH), main),
        pl.BlockSpec((None, None, 3, d), lambda b, i: (b, jnp.where(i + tile_off >= nct, 1, 0), 0, 0)),
        pl.BlockSpec((SSD_WIDTH, d), const2),
        pl.BlockSpec((ATT_WIDTH, d), const2),
        pl.BlockSpec((CONV_WIDTH, d), const2),
        pl.BlockSpec((1, d), const2),
    ]
    rows_out = t - tile_off * tm
    return pl.pallas_call(
        functools.partial(_outproj_kernel, final=final),
        grid=(bsz, rows_out // tm),
        in_specs=in_specs,
        out_specs=pl.BlockSpec((None, tm, d), lambda b, i: (b, i, 0)),
        out_shape=jax.ShapeDtypeStruct((bsz, rows_out, d), F32),
        compiler_params=_cparams("parallel", "arbitrary"),
        name="outproj_final" if final else "outproj",
    )(xa, o_ssd, o_att, o_conv, modsel, w_s, w_a, w_c, final_norm_w)


def _rope_tables(seq, ctx_len):
    rows = seq // GRID_W
    row = jnp.repeat(jnp.arange(rows, dtype=F32), GRID_W)
    col = jnp.tile(jnp.arange(GRID_W, dtype=F32), rows)
    axis_dim = ATT_HEAD_DIM // 2
    inv_freq = ROPE_BASE ** (-jnp.arange(0, axis_dim, 2, dtype=F32) / axis_dim)
    ar = row[:, None] * inv_freq
    ac = col[:, None] * inv_freq
    cos = jnp.concatenate([jnp.cos(ar), jnp.cos(ar), jnp.cos(ac), jnp.cos(ac)], axis=1)
    sin = jnp.concatenate([-jnp.sin(ar), jnp.sin(ar), -jnp.sin(ac), jnp.sin(ac)], axis=1)
    reps = LANES // ATT_HEAD_DIM
    cos = jnp.concatenate([jnp.ones((ctx_len, LANES), F32), jnp.tile(cos, (1, reps))], axis=0)
    sin = jnp.concatenate([jnp.zeros((ctx_len, LANES), F32), jnp.tile(sin, (1, reps))], axis=0)
    return cos, sin


def _pad_lanes(a, width):
    return jnp.pad(a, [(0, 0)] * (a.ndim - 1) + [(0, width - a.shape[-1])])


def _split_w_in(w_in):
    sizes = (SSD_XBC, SSD_WIDTH, 2 * SSD_HEADS, ATT_WIDTH, ATT_KV_WIDTH, ATT_KV_WIDTH, ATT_WIDTH,
             2 * CONV_WIDTH, CONV_WIDTH)
    parts, off = [], 0
    for size in sizes:
        parts.append(w_in[:, off:off + size])
        off += size
    dt_w = parts[2]
    parts[2] = jnp.concatenate([_pad_lanes(dt_w[:, :SSD_HEADS], LANES), _pad_lanes(dt_w[:, SSD_HEADS:], LANES)], axis=1)
    return [p.astype(BF16) for p in parts]


def kernel(x, c, ctx, c_ctx, w_mod, b_mod, norm_w, w_in, ssd_conv_w, ssd_conv_b, ssd_dt_bias, ssd_a_log, ssd_d,
           ssd_norm_w, attn_sink, conv_dw_w, conv_dw_b, conv_ln_w, conv_ln_b, conv_pw_w, conv_pw_b, w_out,
           final_norm_w):
    bsz, seq, d = x.shape
    ctx_len = ctx.shape[1]
    depth = w_in.shape[0]
    assert ctx_len % ROW_TILE == 0 and seq % ROW_TILE == 0 and seq % GRID_W == 0
    assert ROW_TILE == ATT_TILE == CONF_TILE
    nct = ctx_len // ROW_TILE
    ncc = ctx_len // SSD_CHUNK

    cond_rows = -(-(bsz + 1) // SUBLANES) * SUBLANES
    cond = jnp.zeros((cond_rows, d), F32).at[:bsz].set(c).at[bsz].set(c_ctx)
    mod = _modulation(cond, w_mod, b_mod)
    cos_t, sin_t = _rope_tables(seq, ctx_len)
    xa = jnp.concatenate([ctx, x], axis=1)

    out = None
    for l in range(depth):
        final = l == depth - 1
        lat = mod[l, :bsz]
        ctxm = jnp.broadcast_to(mod[l, bsz], lat.shape)
        modsel = jnp.stack([ctxm, lat], axis=1).reshape(bsz, 2, 3, d)
        weights = _split_w_in(w_in[l])
        xs, bc, z, dtp, q, k, v, ag, glu, cg = _inproj(xa, modsel, norm_w[l][None], cos_t, sin_t, ssd_conv_w[l],
                                                       ssd_conv_b[l][None], weights, nct)

        o_ssd = _ssd(xs, bc, dtp, z, _pad_lanes(ssd_dt_bias[l], LANES)[:, None],
                     _pad_lanes(ssd_a_log[l], LANES)[:, None], jnp.repeat(ssd_d[l], SSD_HEAD_DIM)[None],
                     ssd_norm_w[l][None], ctx_len)
        tile_off = nct if final else 0
        sink_tab = jnp.broadcast_to(attn_sink[l][:, None], (ATT_HEADS, LANES))
        o_att = _attention(q, k, v, ag, sink_tab, ctx_len, tile_off)
        dw_w = jnp.broadcast_to(conv_dw_w[l][:, None, :], (CONV_KERNEL, SUBLANES, CONV_WIDTH))
        dw_b = jnp.broadcast_to(conv_dw_b[l][None], (SUBLANES, CONV_WIDTH))
        o_conv = _conformer(glu, cg, dw_w, dw_b, conv_ln_w[l][None], conv_ln_b[l][None],
                            conv_pw_w[l].astype(BF16), conv_pw_b[l][None], ctx_len, tile_off)
        w_o = w_out[l].astype(BF16)
        out = _outproj(xa, o_ssd, o_att, o_conv, modsel, w_o[:SSD_WIDTH], w_o[SSD_WIDTH:SSD_WIDTH + ATT_WIDTH],
                       w_o[SSD_WIDTH + ATT_WIDTH:], final_norm_w[None], nct, final)
        xa = out
    return out
```

```python
import functools

import jax
import jax.numpy as jnp
from jax import lax
from jax.experimental import pallas as pl
from jax.experimental.pallas import tpu as pltpu

F32 = jnp.float32
BF16 = jnp.bfloat16

GRID_W = 64
EPS = 1e-6

SSD_HEADS = 16
SSD_HEAD_DIM = 64
SSD_WIDTH = SSD_HEADS * SSD_HEAD_DIM
SSD_GROUPS = 2
SSD_STATE = 128
SSD_CONV = 5
SSD_CHUNK = 128
SSD_GN = SSD_GROUPS * SSD_STATE
SSD_XBC = SSD_WIDTH + 2 * SSD_GN
GROUP_COLS = SSD_WIDTH // SSD_GROUPS

ATT_HEADS = 8
ATT_KV_HEADS = 2
ATT_HEAD_DIM = 64
ATT_WIDTH = ATT_HEADS * ATT_HEAD_DIM
ATT_KV_WIDTH = ATT_KV_HEADS * ATT_HEAD_DIM
WINDOW = 128
ROPE_BASE = 10000.0
ATT_SCALE = ATT_HEAD_DIM ** -0.5

CONV_WIDTH = 512
CONV_KERNEL = 31
CONV_PAD = (CONV_KERNEL - 1) // 2

LANES = 128

PF_Z = 0
PF_GLU = PF_Z + SSD_WIDTH
PF_AG = PF_GLU + 2 * CONV_WIDTH
PF_CG = PF_AG + ATT_WIDTH
PF_DT = PF_CG + CONV_WIDTH
PF_WIDTH = PF_DT + 2 * LANES
PB_Q = 0
PB_KV = PB_Q + ATT_WIDTH
PB_WIDTH = PB_KV + 2 * ATT_KV_WIDTH
SUBLANES = 8
VMEM_LIMIT = 56 * 1024 * 1024

ROW_TILE = 256
ATT_TILE = 256
ATT_QBLOCK = 256
CONF_TILE = 256
CONF_HALO = 16
CONF_SUB = 32
SSD_HALO = SUBLANES
SSD_STEP = 2 * SSD_CHUNK


def _dot(a, b):
    return jnp.dot(a, b, preferred_element_type=F32)


def _dot_nt(a, b):
    return lax.dot_general(a, b, (((1,), (1,)), ((), ())), preferred_element_type=F32)


def _split3(x):
    hi = x.astype(BF16)
    r1 = x - hi.astype(F32)
    mid = r1.astype(BF16)
    lo = (r1 - mid.astype(F32)).astype(BF16)
    return hi, mid, lo


def _pack3(x, width):
    hi, mid, lo = _split3(x)
    packed = hi.astype(F32) + pltpu.roll(mid.astype(F32), width, 1) + pltpu.roll(lo.astype(F32), 2 * width, 1)
    return packed.astype(BF16)


def _shift_rows(x, j):
    rows, w = x.shape
    g = x.reshape(rows // SUBLANES, SUBLANES, w)
    r = pltpu.roll(g, SUBLANES - j, 1)
    sub = lax.broadcasted_iota(jnp.int32, (1, SUBLANES, w), 1)
    return jnp.where(sub < SUBLANES - j, r[:-1], r[1:]).reshape(rows - SUBLANES, w)


def _cparams(*sem):
    return pltpu.CompilerParams(dimension_semantics=sem, vmem_limit_bytes=VMEM_LIMIT)


def _mod_kernel(c_ref, w_ref, b_ref, o_ref):
    a = jax.nn.silu(c_ref[...]).astype(BF16)
    o_ref[...] = _dot(a, w_ref[...].astype(BF16)) + b_ref[...]


def _modulation(cond, w_mod, b_mod):
    depth, d, n = w_mod.shape
    rows = cond.shape[0]
    tn = 512
    return pl.pallas_call(
        _mod_kernel,
        grid=(depth, n // tn),
        in_specs=[
            pl.BlockSpec((rows, d), lambda l, j: (0, 0)),
            pl.BlockSpec((None, d, tn), lambda l, j: (l, 0, j)),
            pl.BlockSpec((None, 1, tn), lambda l, j: (l, 0, j)),
        ],
        out_specs=pl.BlockSpec((None, rows, tn), lambda l, j: (l, 0, j)),
        out_shape=jax.ShapeDtypeStruct((depth, rows, n), F32),
        compiler_params=_cparams("arbitrary", "arbitrary"),
        name="modulation",
    )(cond, w_mod, b_mod.reshape(depth, 1, n))


def _rope(t, cos, sin):
    width = t.shape[-1]
    lane = lax.broadcasted_iota(jnp.int32, t.shape, 1)
    first = (lane % 32) < 16
    rot = jnp.where(first, pltpu.roll(t, width - 16, 1), pltpu.roll(t, 16, 1))
    return t * cos + rot * sin


def _inproj_kernel(x_ref, mod_ref, nw_ref, rope_ref, wxbc, wz, wdt, wq, wk, wv, wag, wglu, wcg,
                   xbc_ref, pf_ref, pb_ref):
    x = x_ref[...]
    y = x * lax.rsqrt(jnp.mean(x * x, axis=-1, keepdims=True) + EPS) * nw_ref[...]
    h = (y * (1.0 + mod_ref[1:2, :]) + mod_ref[0:1, :]).astype(BF16)
    xbc_ref[...] = _dot(h, wxbc[...])
    pf_ref[:, PF_Z:PF_Z + SSD_WIDTH] = _dot(h, wz[...])
    pf_ref[:, PF_DT:PF_DT + 2 * LANES] = _dot(h, wdt[...])
    pf_ref[:, PF_AG:PF_AG + ATT_WIDTH] = _dot(h, wag[...])
    pf_ref[:, PF_GLU:PF_GLU + 2 * CONV_WIDTH] = _dot(h, wglu[...])
    pf_ref[:, PF_CG:PF_CG + CONV_WIDTH] = _dot(h, wcg[...])
    pb_ref[:, PB_KV + ATT_KV_WIDTH:PB_KV + 2 * ATT_KV_WIDTH] = _dot(h, wv[...]).astype(BF16)
    cos = rope_ref[:, :LANES]
    sin = rope_ref[:, LANES:]
    pb_ref[:, PB_KV:PB_KV + ATT_KV_WIDTH] = _rope(_dot(h, wk[...]), cos, sin).astype(BF16)
    reps = ATT_WIDTH // LANES
    cos_q = jnp.concatenate([cos] * reps, axis=1)
    sin_q = jnp.concatenate([sin] * reps, axis=1)
    pb_ref[:, PB_Q:PB_Q + ATT_WIDTH] = (_rope(_dot(h, wq[...]), cos_q, sin_q) * ATT_SCALE).astype(BF16)


def _inproj(xa, modsel, norm_w, rope_t, weights, nct):
    bsz, t, d = xa.shape
    tm = ROW_TILE
    row_map = lambda b, i: (b, i, 0)
    const2 = lambda b, i: (0, 0)
    in_specs = [
        pl.BlockSpec((None, tm, d), row_map),
        pl.BlockSpec((None, None, 3, d), lambda b, i: (b, jnp.where(i >= nct, 1, 0), 0, 0)),
        pl.BlockSpec((1, d), const2),
        pl.BlockSpec((tm, 2 * LANES), lambda b, i: (i, 0)),
    ] + [pl.BlockSpec((d, w.shape[1]), const2) for w in weights]
    widths = (SSD_XBC, PF_WIDTH, PB_WIDTH)
    return pl.pallas_call(
        _inproj_kernel,
        grid=(bsz, t // tm),
        in_specs=in_specs,
        out_specs=[pl.BlockSpec((None, tm, w), row_map) for w in widths],
        out_shape=[jax.ShapeDtypeStruct((bsz, t, w), dt) for w, dt in zip(widths, (F32, F32, BF16))],
        compiler_params=_cparams("parallel", "arbitrary"),
        name="inproj",
    )(xa, modsel, norm_w, rope_t, *weights)


def _ssd_chunk(xs, bc, dt_raw, dtb, alog, fwd, st_ref):
    q = SSD_CHUNK
    bb = bc[:, :SSD_GN]
    cb16 = bc[:, SSD_GN:]

    lane = lax.broadcasted_iota(jnp.int32, (q, LANES), 1)
    head_lane = lane < SSD_HEADS
    dt = jnp.where(head_lane, jax.nn.softplus(dt_raw + dtb), 0.0)
    da = dt * (-jnp.exp(alog))
    li = lax.broadcasted_iota(jnp.int32, (q, q), 0)
    si = lax.broadcasted_iota(jnp.int32, (q, q), 1)
    causal = (si <= li) if fwd else (si >= li)
    cs3 = _dot(jnp.where(causal, 1.0, 0.0).astype(BF16), _pack3(da, SSD_HEADS))
    cs = jnp.where(head_lane, cs3 + pltpu.roll(cs3, LANES - SSD_HEADS, 1) + pltpu.roll(cs3, LANES - 2 * SSD_HEADS, 1), 0.0)
    total = jnp.sum(da, axis=0, keepdims=True)
    to_end = jnp.exp(total - cs)
    from_start = jnp.where(head_lane, jnp.exp(cs), 0.0)
    chunk_decay = jnp.where(head_lane[:SUBLANES], jnp.exp(total), 0.0)
    cs_t = cs.T

    er = lax.broadcasted_iota(jnp.int32, (LANES, SSD_WIDTH), 0)
    ec = lax.broadcasted_iota(jnp.int32, (LANES, SSD_WIDTH), 1)
    spread = jnp.where((ec // SSD_HEAD_DIM == er % SSD_HEADS) & (er < 3 * SSD_HEADS), 1.0, 0.0).astype(BF16)
    dt_e = _dot(_pack3(dt, SSD_HEADS), spread)
    stacked = jnp.concatenate([dt * to_end, from_start, chunk_decay], axis=0)
    stacked_e = _dot(_pack3(stacked, SSD_HEADS), spread)
    w_e = stacked_e[:q]
    fs_e = stacked_e[q:2 * q]
    cd_e = stacked_e[2 * q:2 * q + 1]

    xd = (xs * dt_e).astype(BF16)
    xdw = (xs * w_e).astype(BF16)
    low = lax.broadcasted_iota(jnp.int32, (q, LANES), 1) < SSD_HEAD_DIM
    zero16 = jnp.zeros((q, LANES), BF16)
    ys = []
    for g in range(SSD_GROUPS):
        gs = slice(g * SSD_STATE, (g + 1) * SSD_STATE)
        gc = slice(g * GROUP_COLS, (g + 1) * GROUP_COLS)
        scores = _dot_nt(cb16[:, gs], bb[:, gs])
        entering = st_ref[g]
        y_off = _dot(cb16[:, gs], entering.astype(BF16))
        new_state = _dot(bb[:, gs].astype(F32).T.astype(BF16), xdw[:, gc])
        st_ref[g] = entering * cd_e[:, gc] + new_state
        for pr in range(GROUP_COLS // LANES):
            h0 = (g * GROUP_COLS + pr * LANES) // SSD_HEAD_DIM
            cols = slice(h0 * SSD_HEAD_DIM, h0 * SSD_HEAD_DIM + LANES)
            xpair = xd[:, cols]
            m0 = (scores * jnp.exp(jnp.where(causal, cs[:, h0:h0 + 1] - cs_t[h0:h0 + 1, :], -jnp.inf))).astype(BF16)
            m1 = (scores * jnp.exp(jnp.where(causal, cs[:, h0 + 1:h0 + 2] - cs_t[h0 + 1:h0 + 2, :], -jnp.inf))).astype(BF16)
            block_diag = jnp.concatenate([jnp.where(low, xpair, zero16), jnp.where(low, zero16, xpair)], axis=0)
            y_diag = _dot(jnp.concatenate([m0, m1], axis=1), block_diag)
            ys.append(y_diag + y_off[:, pr * LANES:(pr + 1) * LANES] * fs_e[:, cols])
    return jnp.concatenate(ys, axis=1)


def _ssd_bwd_block(k, ncb, nb):
    return jnp.where(k < ncb, ncb - 1 - k, nb + ncb - 1 - k)


def _ssd_kernel(xm_ref, xp_ref, xn_ref, dt_ref, z_ref, cw_ref, cb_ref, dtb_ref, alog_ref, dvec_ref, nw_ref,
                o_ref, yb_ref, st_ref, xs_ref, bc_ref, *, ncb, nb):
    q = SSD_CHUNK
    rows = SSD_STEP
    n_sub = rows // q
    s = pl.program_id(1)

    def conv_block(cb):
        first = (cb == 0) | (cb == ncb)
        last = (cb == ncb - 1) | (cb == nb - 1)
        xext = jnp.concatenate([jnp.where(first, 0.0, xp_ref[...]), xm_ref[...],
                                jnp.where(last, 0.0, xn_ref[...])], axis=0)
        acc = cb_ref[...]
        for k in range(SSD_CONV):
            d = k - (SSD_CONV - 1) // 2
            if d == 0:
                tap = xext[SSD_HALO:SSD_HALO + rows]
            elif d > 0:
                tap = _shift_rows(xext, d)[SSD_HALO:SSD_HALO + rows]
            else:
                tap = _shift_rows(xext, SSD_HALO + d)[:rows]
            acc = acc + cw_ref[k:k + 1, :] * tap
        u = jax.nn.silu(acc)
        dst = pl.ds(pl.multiple_of(cb * rows, rows), rows)
        xs_ref[dst, :] = u[:, :SSD_WIDTH]
        bc_ref[dst, :] = u[:, SSD_WIDTH:].astype(BF16)

    def chunk_rows(blk, j):
        return pl.ds(pl.multiple_of(blk * rows + j * q, q), q)

    @pl.when((s == 0) | (s == nb))
    def _():
        st_ref[...] = jnp.zeros_like(st_ref)

    @pl.when(s <= nb - 1)
    def _():
        conv_block(_ssd_bwd_block(s, ncb, nb))

    @pl.when(s < nb)
    def _():
        blk = _ssd_bwd_block(s, ncb, nb)
        for j in reversed(range(n_sub)):
            y = _ssd_chunk(xs_ref[chunk_rows(blk, j), :], bc_ref[chunk_rows(blk, j), :], dt_ref[j * q:(j + 1) * q, :],
                           dtb_ref[...], alog_ref[...], False, st_ref)
            yb_ref[chunk_rows(blk, j), :] = y

    @pl.when(s >= nb)
    def _():
        blk = s - nb
        for j in range(n_sub):
            xs = xs_ref[chunk_rows(blk, j), :]
            y = _ssd_chunk(xs, bc_ref[chunk_rows(blk, j), :], dt_ref[j * q:(j + 1) * q, :], dtb_ref[...],
                           alog_ref[...], True, st_ref)
            tot = y + yb_ref[chunk_rows(blk, j), :] + dvec_ref[...] * xs
            gated = tot * jax.nn.silu(z_ref[j * q:(j + 1) * q, :])
            outs = []
            for g in range(SSD_GROUPS):
                gc = slice(g * GROUP_COLS, (g + 1) * GROUP_COLS)
                v = gated[:, gc]
                outs.append(v * lax.rsqrt(jnp.mean(v * v, axis=-1, keepdims=True) + EPS) * nw_ref[:, gc])
            o_ref[j * q:(j + 1) * q, :] = jnp.concatenate(outs, axis=1).astype(BF16)


def _ssd(xbc, pf, conv_w, conv_b, dt_bias, a_log, dvec, norm_w, ctx_len):
    bsz, t, _ = pf.shape
    rows = SSD_STEP
    nb = t // rows
    ncb = ctx_len // rows
    hb = rows // SSD_HALO
    nhalo = t // SSD_HALO
    const2 = lambda b, s: (0, 0)
    xblk = lambda s: _ssd_bwd_block(jnp.minimum(s, nb - 1), ncb, nb)
    cur = lambda s: jnp.where(s < nb, _ssd_bwd_block(s, ncb, nb), s - nb)
    phase = lambda s: jnp.where(s < nb, 1, 0)
    fwd_blk = lambda s: jnp.maximum(s - nb, 0)
    in_specs = [
        pl.BlockSpec((None, rows, SSD_XBC), lambda b, s: (b, xblk(s), 0)),
        pl.BlockSpec((None, SSD_HALO, SSD_XBC), lambda b, s: (b, jnp.maximum(xblk(s) * hb - 1, 0), 0)),
        pl.BlockSpec((None, SSD_HALO, SSD_XBC), lambda b, s: (b, jnp.minimum(xblk(s) * hb + hb, nhalo - 1), 0)),
        pl.BlockSpec((None, rows, LANES), lambda b, s: (b, cur(s), PF_DT // LANES + phase(s))),
        pl.BlockSpec((None, rows, SSD_WIDTH), lambda b, s: (b, fwd_blk(s), PF_Z // SSD_WIDTH)),
        pl.BlockSpec((SSD_CONV, SSD_XBC), const2),
        pl.BlockSpec((1, SSD_XBC), const2),
        pl.BlockSpec((None, 1, LANES), lambda b, s: (phase(s), 0, 0)),
        pl.BlockSpec((None, 1, LANES), lambda b, s: (phase(s), 0, 0)),
        pl.BlockSpec((1, SSD_WIDTH), const2),
        pl.BlockSpec((1, SSD_WIDTH), const2),
    ]
    return pl.pallas_call(
        functools.partial(_ssd_kernel, ncb=ncb, nb=nb),
        grid=(bsz, 2 * nb),
        in_specs=in_specs,
        out_specs=pl.BlockSpec((None, rows, SSD_WIDTH), lambda b, s: (b, fwd_blk(s), 0)),
        out_shape=jax.ShapeDtypeStruct((bsz, t, SSD_WIDTH), BF16),
        scratch_shapes=[
            pltpu.VMEM((t, SSD_WIDTH), F32),
            pltpu.VMEM((SSD_GROUPS, SSD_STATE, GROUP_COLS), F32),
            pltpu.VMEM((t, SSD_WIDTH), F32),
            pltpu.VMEM((t, 2 * SSD_GN), BF16),
        ],
        compiler_params=_cparams("parallel", "arbitrary"),
        name="ssd",
    )(xbc, xbc, xbc, pf, pf, conv_w, conv_b, dt_bias, a_log, dvec, norm_w)


def _swap_halves(blk):
    half = blk.shape[-1] // 2
    return jnp.concatenate([blk[:, half:], blk[:, :half]], axis=1)


def _head_pads(blk, g, low):
    zero = jnp.zeros_like(blk)
    sw = _swap_halves(blk)
    if g == 0:
        return jnp.where(low, blk, zero), jnp.where(low, zero, sw)
    return jnp.where(low, sw, zero), jnp.where(low, zero, blk)


def _attn_kernel(q_ref, kvm_ref, kvp_ref, kvn_ref, kvc_ref, ag_ref, sink_ref, o_ref, pad_ref,
                 *, tile_off, nct, seq, ctx_len):
    qb = ATT_TILE
    blk = ATT_QBLOCK
    wk = blk + 2 * WINDOW
    tile = pl.program_id(1) + tile_off
    lat0 = (tile - nct) * qb
    low_k = lax.broadcasted_iota(jnp.int32, (1, LANES), 1) < ATT_HEAD_DIM

    kv = jnp.concatenate([kvc_ref[...], kvp_ref[...], kvm_ref[...], kvn_ref[...]], axis=0)
    for g in range(ATT_KV_HEADS):
        ka, kb = _head_pads(kv[:, :ATT_KV_WIDTH], g, low_k)
        va, vb = _head_pads(kv[:, ATT_KV_WIDTH:], g, low_k)
        pad_ref[g, 0] = ka
        pad_ref[g, 1] = kb
        pad_ref[g, 2] = va
        pad_ref[g, 3] = vb

    r = lax.broadcasted_iota(jnp.int32, (blk, wk), 0)
    col = lax.broadcasted_iota(jnp.int32, (blk, wk), 1)
    in_band = jnp.abs(col - WINDOW - r) <= WINDOW
    neg = jnp.finfo(F32).min
    rep = ATT_HEADS // ATT_KV_HEADS
    pairs_per_group = rep * ATT_HEAD_DIM // LANES
    for a in range(qb // blk):
        rows = slice(a * blk, (a + 1) * blk)
        kpos = lat0 + a * blk - WINDOW + col
        valid = in_band & (kpos >= 0) & (kpos < seq) & (lat0 >= 0)
        win = slice(ctx_len + a * blk, ctx_len + a * blk + wk)
        outs = []
        for g in range(ATT_KV_HEADS):
            for pr in range(pairs_per_group):
                p = g * pairs_per_group + pr
                qp = q_ref[rows, p * LANES:(p + 1) * LANES]
                acc = jnp.zeros((blk, LANES), F32)
                invs = []
                for j in range(2):
                    head = 2 * p + j
                    sink = sink_ref[head:head + 1, 0:1]
                    sc = _dot_nt(qp, pad_ref[g, j, 0:ctx_len, :])
                    sw = jnp.where(valid, _dot_nt(qp, pad_ref[g, j, win, :]), neg)
                    m = jnp.maximum(jnp.maximum(jnp.max(sc, axis=-1, keepdims=True),
                                                jnp.max(sw, axis=-1, keepdims=True)), sink)
                    pc = jnp.exp(sc - m)
                    pw = jnp.exp(sw - m)
                    den = (jnp.sum(pc, axis=-1, keepdims=True) + jnp.sum(pw, axis=-1, keepdims=True)
                           + jnp.exp(sink - m))
                    acc = (acc + _dot(pc.astype(BF16), pad_ref[g, 2 + j, 0:ctx_len, :])
                           + _dot(pw.astype(BF16), pad_ref[g, 2 + j, win, :]))
                    invs.append(1.0 / den)
                outs.append(acc * jnp.where(low_k, invs[0], invs[1]))
        o = jnp.concatenate(outs, axis=1)
        o_ref[rows, :] = (o * jax.nn.silu(ag_ref[rows, :])).astype(BF16)


def _attention(pf, pb, sink_tab, ctx_len, tile_off):
    bsz, t, _ = pf.shape
    qb = ATT_TILE
    nct = ctx_len // qb
    hb = qb // WINDOW
    ctx_hb = ctx_len // WINDOW
    n_hb = t // WINDOW
    ntiles = t // qb - tile_off
    kvw = 2 * ATT_KV_WIDTH
    kv_col = PB_KV // kvw
    in_specs = [
        pl.BlockSpec((None, qb, ATT_WIDTH), lambda b, i: (b, i + tile_off, PB_Q // ATT_WIDTH)),
        pl.BlockSpec((None, qb, kvw), lambda b, i: (b, i + tile_off, kv_col)),
        pl.BlockSpec((None, WINDOW, kvw), lambda b, i: (b, jnp.maximum((i + tile_off) * hb - 1, ctx_hb), kv_col)),
        pl.BlockSpec((None, WINDOW, kvw), lambda b, i: (b, jnp.minimum((i + tile_off) * hb + hb, n_hb - 1), kv_col)),
        pl.BlockSpec((None, ctx_len, kvw), lambda b, i: (b, 0, kv_col)),
        pl.BlockSpec((None, qb, ATT_WIDTH), lambda b, i: (b, i + tile_off, PF_AG // ATT_WIDTH)),
        pl.BlockSpec((ATT_HEADS, LANES), lambda b, i: (0, 0)),
    ]
    return pl.pallas_call(
        functools.partial(_attn_kernel, tile_off=tile_off, nct=nct, seq=t - ctx_len, ctx_len=ctx_len),
        grid=(bsz, ntiles),
        in_specs=in_specs,
        out_specs=pl.BlockSpec((None, qb, ATT_WIDTH), lambda b, i: (b, i + tile_off, 0)),
        out_shape=jax.ShapeDtypeStruct((bsz, t, ATT_WIDTH), BF16),
        scratch_shapes=[pltpu.VMEM((ATT_KV_HEADS, 4, ctx_len + qb + 2 * WINDOW, LANES), BF16)],
        compiler_params=_cparams("parallel", "arbitrary"),
        name="attention",
    )(pb, pb, pb, pb, pb, pf, sink_tab)


def _conf_kernel(gm_ref, gp_ref, gn_ref, cg_ref, dww_ref, dwb_ref, lnw_ref, lnb_ref, pww_ref, pwb_ref, o_ref,
                 hbuf, abuf, *, tile_off, nct, nt):
    tc = CONF_TILE
    tile = pl.program_id(1) + tile_off
    first = (tile == 0) | (tile == nct)
    last = (tile == nct - 1) | (tile == nt - 1)

    def glu(v):
        return v[:, :CONV_WIDTH] * jax.nn.sigmoid(v[:, CONV_WIDTH:])

    rows = tc + 2 * CONF_HALO
    ext = jnp.concatenate([jnp.where(first, 0.0, glu(gp_ref[...])), glu(gm_ref[...]),
                           jnp.where(last, 0.0, glu(gn_ref[...]))], axis=0)
    hbuf[0] = ext
    for j in range(1, SUBLANES):
        hbuf[j, 0:rows - SUBLANES, :] = _shift_rows(ext, j)
    base = CONF_HALO - CONV_PAD
    for r in range(tc // CONF_SUB):
        acc = jnp.broadcast_to(dwb_ref[...], (CONF_SUB // SUBLANES, SUBLANES, CONV_WIDTH))
        for k in range(CONV_KERNEL):
            blk, res = divmod(base + k, SUBLANES)
            strip = hbuf[res, pl.ds(r * CONF_SUB + blk * SUBLANES, CONF_SUB), :]
            acc = acc + dww_ref[k] * strip.reshape(CONF_SUB // SUBLANES, SUBLANES, CONV_WIDTH)
        acc = acc.reshape(CONF_SUB, CONV_WIDTH)
        xc = acc - jnp.mean(acc, axis=-1, keepdims=True)
        var = jnp.mean(xc * xc, axis=-1, keepdims=True)
        ln = xc * lax.rsqrt(var + EPS) * lnw_ref[...] + lnb_ref[...]
        abuf[pl.ds(r * CONF_SUB, CONF_SUB), :] = jax.nn.silu(ln)
    out = _dot(abuf[...].astype(BF16), pww_ref[...]) + pwb_ref[...]
    o_ref[...] = (out * jax.nn.silu(cg_ref[...])).astype(BF16)


def _conformer(pf, dw_w, dw_b, ln_w, ln_b, pw_w, pw_b, ctx_len, tile_off):
    bsz, t, _ = pf.shape
    tc = CONF_TILE
    nt = t // tc
    nct = ctx_len // tc
    hb = tc // CONF_HALO
    nhalo = t // CONF_HALO
    main = lambda b, i: (b, i + tile_off, 0)
    const2 = lambda b, i: (0, 0)
    glu_col = PF_GLU // (2 * CONV_WIDTH)
    in_specs = [
        pl.BlockSpec((None, tc, 2 * CONV_WIDTH), lambda b, i: (b, i + tile_off, glu_col)),
        pl.BlockSpec((None, CONF_HALO, 2 * CONV_WIDTH),
                     lambda b, i: (b, jnp.maximum((i + tile_off) * hb - 1, 0), glu_col)),
        pl.BlockSpec((None, CONF_HALO, 2 * CONV_WIDTH),
                     lambda b, i: (b, jnp.minimum((i + tile_off) * hb + hb, nhalo - 1), glu_col)),
        pl.BlockSpec((None, tc, CONV_WIDTH), lambda b, i: (b, i + tile_off, PF_CG // CONV_WIDTH)),
        pl.BlockSpec((CONV_KERNEL, SUBLANES, CONV_WIDTH), lambda b, i: (0, 0, 0)),
        pl.BlockSpec((SUBLANES, CONV_WIDTH), const2),
        pl.BlockSpec((1, CONV_WIDTH), const2),
        pl.BlockSpec((1, CONV_WIDTH), const2),
        pl.BlockSpec((CONV_WIDTH, CONV_WIDTH), const2),
        pl.BlockSpec((1, CONV_WIDTH), const2),
    ]
    return pl.pallas_call(
        functools.partial(_conf_kernel, tile_off=tile_off, nct=nct, nt=nt),
        grid=(bsz, nt - tile_off),
        in_specs=in_specs,
        out_specs=pl.BlockSpec((None, tc, CONV_WIDTH), main),
        out_shape=jax.ShapeDtypeStruct((bsz, t, CONV_WIDTH), BF16),
        scratch_shapes=[
            pltpu.VMEM((SUBLANES, tc + 2 * CONF_HALO, CONV_WIDTH), F32),
            pltpu.VMEM((tc, CONV_WIDTH), F32),
        ],
        compiler_params=_cparams("parallel", "arbitrary"),
        name="conformer",
    )(pf, pf, pf, pf, dw_w, dw_b, ln_w, ln_b, pw_w, pw_b)


def _outproj_kernel(x_ref, os_ref, oa_ref, oc_ref, mod_ref, ws_ref, wa_ref, wc_ref, fnw_ref, o_ref, *, final):
    u = _dot(os_ref[...], ws_ref[...]) + _dot(oa_ref[...], wa_ref[...]) + _dot(oc_ref[...], wc_ref[...])
    xn = x_ref[...] + mod_ref[2:3, :] * u
    if final:
        xn = xn * lax.rsqrt(jnp.mean(xn * xn, axis=-1, keepdims=True) + EPS) * fnw_ref[...]
    o_ref[...] = xn


def _outproj(xa, o_ssd, o_att, o_conv, modsel, w_s, w_a, w_c, final_norm_w, nct, final):
    bsz, t, d = xa.shape
    tm = ROW_TILE
    tile_off = nct if final else 0
    main = lambda b, i: (b, i + tile_off, 0)
    const2 = lambda b, i: (0, 0)
    in_specs = [
        pl.BlockSpec((None, tm, d), main),
        pl.BlockSpec((None, tm, SSD_WIDTH), main),
        pl.BlockSpec((None, tm, ATT_WIDTH), main),
        pl.BlockSpec((None, tm, CONV_WIDTH), main),
        pl.BlockSpec((None, None, 3, d), lambda b, i: (b, jnp.where(i + tile_off >= nct, 1, 0), 0, 0)),
        pl.BlockSpec((SSD_WIDTH, d), const2),
        pl.BlockSpec((ATT_WIDTH, d), const2),
        pl.BlockSpec((CONV_WIDTH, d), const2),
        pl.BlockSpec((1, d), const2),
    ]
    rows_out = t - tile_off * tm
    return pl.pallas_call(
        functools.partial(_outproj_kernel, final=final),
        grid=(bsz, rows_out // tm),
        in_specs=in_specs,
        out_specs=pl.BlockSpec((None, tm, d), lambda b, i: (b, i, 0)),
        out_shape=jax.ShapeDtypeStruct((bsz, rows_out, d), F32),
        compiler_params=_cparams("parallel", "arbitrary"),
        name="outproj_final" if final else "outproj",
    )(xa, o_ssd, o_att, o_conv, modsel, w_s, w_a, w_c, final_norm_w)


def _rope_tables(seq, ctx_len):
    rows = seq // GRID_W
    row = jnp.repeat(jnp.arange(rows, dtype=F32), GRID_W)
    col = jnp.tile(jnp.arange(GRID_W, dtype=F32), rows)
    axis_dim = ATT_HEAD_DIM // 2
    inv_freq = ROPE_BASE ** (-jnp.arange(0, axis_dim, 2, dtype=F32) / axis_dim)
    ar = row[:, None] * inv_freq
    ac = col[:, None] * inv_freq
    cos = jnp.concatenate([jnp.cos(ar), jnp.cos(ar), jnp.cos(ac), jnp.cos(ac)], axis=1)
    sin = jnp.concatenate([-jnp.sin(ar), jnp.sin(ar), -jnp.sin(ac), jnp.sin(ac)], axis=1)
    reps = LANES // ATT_HEAD_DIM
    cos = jnp.concatenate([jnp.ones((ctx_len, LANES), F32), jnp.tile(cos, (1, reps))], axis=0)
    sin = jnp.concatenate([jnp.zeros((ctx_len, LANES), F32), jnp.tile(sin, (1, reps))], axis=0)
    return jnp.concatenate([cos, sin], axis=1)


def _pad_lanes(a, width):
    return jnp.pad(a, [(0, 0)] * (a.ndim - 1) + [(0, width - a.shape[-1])])


def _split_w_in(w_in):
    sizes = (SSD_XBC, SSD_WIDTH, 2 * SSD_HEADS, ATT_WIDTH, ATT_KV_WIDTH, ATT_KV_WIDTH, ATT_WIDTH,
             2 * CONV_WIDTH, CONV_WIDTH)
    parts, off = [], 0
    for size in sizes:
        parts.append(w_in[:, off:off + size])
        off += size
    dt_w = parts[2]
    parts[2] = jnp.concatenate([_pad_lanes(dt_w[:, :SSD_HEADS], LANES), _pad_lanes(dt_w[:, SSD_HEADS:], LANES)], axis=1)
    return [p.astype(BF16) for p in parts]


def kernel(x, c, ctx, c_ctx, w_mod, b_mod, norm_w, w_in, ssd_conv_w, ssd_conv_b, ssd_dt_bias, ssd_a_log, ssd_d,
           ssd_norm_w, attn_sink, conv_dw_w, conv_dw_b, conv_ln_w, conv_ln_b, conv_pw_w, conv_pw_b, w_out,
           final_norm_w):
    bsz, seq, d = x.shape
    ctx_len = ctx.shape[1]
    depth = w_in.shape[0]
    assert ctx_len % ROW_TILE == 0 and seq % ROW_TILE == 0 and seq % GRID_W == 0
    assert ROW_TILE == ATT_TILE == CONF_TILE
    nct = ctx_len // ROW_TILE

    cond_rows = -(-(bsz + 1) // SUBLANES) * SUBLANES
    cond = jnp.zeros((cond_rows, d), F32).at[:bsz].set(c).at[bsz].set(c_ctx)
    mod = _modulation(cond, w_mod, b_mod)
    rope_t = _rope_tables(seq, ctx_len)
    xa = jnp.concatenate([ctx, x], axis=1)

    out = None
    for l in range(depth):
        final = l == depth - 1
        lat = mod[l, :bsz]
        ctxm = jnp.broadcast_to(mod[l, bsz], lat.shape)
        modsel = jnp.stack([ctxm, lat], axis=1).reshape(bsz, 2, 3, d)
        weights = _split_w_in(w_in[l])
        xbc, pf, pb = _inproj(xa, modsel, norm_w[l][None], rope_t, weights, nct)

        o_ssd = _ssd(xbc, pf, ssd_conv_w[l], ssd_conv_b[l][None], _pad_lanes(ssd_dt_bias[l], LANES)[:, None],
                     _pad_lanes(ssd_a_log[l], LANES)[:, None], jnp.repeat(ssd_d[l], SSD_HEAD_DIM)[None],
                     ssd_norm_w[l][None], ctx_len)
        tile_off = nct if final else 0
        sink_tab = jnp.broadcast_to(attn_sink[l][:, None], (ATT_HEADS, LANES))
        o_att = _attention(pf, pb, sink_tab, ctx_len, tile_off)
        dw_w = jnp.broadcast_to(conv_dw_w[l][:, None, :], (CONV_KERNEL, SUBLANES, CONV_WIDTH))
        dw_b = jnp.broadcast_to(conv_dw_b[l][None], (SUBLANES, CONV_WIDTH))
        o_conv = _conformer(pf, dw_w, dw_b, conv_ln_w[l][None], conv_ln_b[l][None],
                            conv_pw_w[l].astype(BF16), conv_pw_b[l][None], ctx_len, tile_off)
        w_o = w_out[l].astype(BF16)
        out = _outproj(xa, o_ssd, o_att, o_conv, modsel, w_o[:SSD_WIDTH], w_o[SSD_WIDTH:SSD_WIDTH + ATT_WIDTH],
                       w_o[SSD_WIDTH + ATT_WIDTH:], final_norm_w[None], nct, final)
        xa = out
    return out
```

```python
import functools

import jax
import jax.numpy as jnp
from jax import lax
from jax.experimental import pallas as pl
from jax.experimental.pallas import tpu as pltpu

F32 = jnp.float32
BF16 = jnp.bfloat16

GRID_W = 64
EPS = 1e-6

SSD_HEADS = 16
SSD_HEAD_DIM = 64
SSD_WIDTH = SSD_HEADS * SSD_HEAD_DIM
SSD_GROUPS = 2
SSD_STATE = 128
SSD_CONV = 5
SSD_CHUNK = 128
SSD_GN = SSD_GROUPS * SSD_STATE
SSD_XBC = SSD_WIDTH + 2 * SSD_GN
GROUP_COLS = SSD_WIDTH // SSD_GROUPS

ATT_HEADS = 8
ATT_KV_HEADS = 2
ATT_HEAD_DIM = 64
ATT_WIDTH = ATT_HEADS * ATT_HEAD_DIM
ATT_KV_WIDTH = ATT_KV_HEADS * ATT_HEAD_DIM
WINDOW = 128
ROPE_BASE = 10000.0
LOG2E = 1.4426950408889634
ATT_SCALE = ATT_HEAD_DIM ** -0.5 * LOG2E

CONV_WIDTH = 512
CONV_KERNEL = 31
CONV_PAD = (CONV_KERNEL - 1) // 2

LANES = 128

PF_Z = 0
PF_GLU = PF_Z + SSD_WIDTH
PF_AG = PF_GLU + 2 * CONV_WIDTH
PF_CG = PF_AG + ATT_WIDTH
PF_DT = PF_CG + CONV_WIDTH
PF_WIDTH = PF_DT + 2 * LANES
PB_Q = 0
PB_KV = PB_Q + ATT_WIDTH
PB_WIDTH = PB_KV + 2 * ATT_KV_WIDTH
SUBLANES = 8
VMEM_LIMIT = 56 * 1024 * 1024

ROW_TILE = 256
ATT_TILE = 256
ATT_QBLOCK = 256
CONF_TILE = 256
CONF_HALO = 16
CONF_SUB = 32
SSD_HALO = SUBLANES
SSD_STEP = 2 * SSD_CHUNK


def _dot(a, b):
    return jnp.dot(a, b, preferred_element_type=F32)


def _dot_nt(a, b):
    return lax.dot_general(a, b, (((1,), (1,)), ((), ())), preferred_element_type=F32)


def _split3(x):
    hi = x.astype(BF16)
    r1 = x - hi.astype(F32)
    mid = r1.astype(BF16)
    lo = (r1 - mid.astype(F32)).astype(BF16)
    return hi, mid, lo


def _pack3(x, width):
    hi, mid, lo = _split3(x)
    lane = lax.broadcasted_iota(jnp.int32, x.shape, 1)
    return jnp.where(lane < width, hi, jnp.where(lane < 2 * width, mid, lo))


def _shift_rows(x, j):
    rows, w = x.shape
    g = x.reshape(rows // SUBLANES, SUBLANES, w)
    r = pltpu.roll(g, SUBLANES - j, 1)
    sub = lax.broadcasted_iota(jnp.int32, (1, SUBLANES, w), 1)
    return jnp.where(sub < SUBLANES - j, r[:-1], r[1:]).reshape(rows - SUBLANES, w)


def _cparams(*sem):
    return pltpu.CompilerParams(dimension_semantics=sem, vmem_limit_bytes=VMEM_LIMIT)


def _mod_kernel(c_ref, w_ref, b_ref, o_ref):
    a = jax.nn.silu(c_ref[...]).astype(BF16)
    o_ref[...] = _dot(a, w_ref[...].astype(BF16)) + b_ref[...]


def _modulation(cond, w_mod, b_mod):
    depth, d, n = w_mod.shape
    rows = cond.shape[0]
    tn = 512
    return pl.pallas_call(
        _mod_kernel,
        grid=(depth, n // tn),
        in_specs=[
            pl.BlockSpec((rows, d), lambda l, j: (0, 0)),
            pl.BlockSpec((None, d, tn), lambda l, j: (l, 0, j)),
            pl.BlockSpec((None, 1, tn), lambda l, j: (l, 0, j)),
        ],
        out_specs=pl.BlockSpec((None, rows, tn), lambda l, j: (l, 0, j)),
        out_shape=jax.ShapeDtypeStruct((depth, rows, n), F32),
        compiler_params=_cparams("arbitrary", "arbitrary"),
        name="modulation",
    )(cond, w_mod, b_mod.reshape(depth, 1, n))


def _rope(t, cos, sin):
    width = t.shape[-1]
    lane = lax.broadcasted_iota(jnp.int32, t.shape, 1)
    first = (lane % 32) < 16
    rot = jnp.where(first, pltpu.roll(t, width - 16, 1), pltpu.roll(t, 16, 1))
    return t * cos + rot * sin


def _inproj_kernel(c_ref, x_ref, mod_ref, nw_ref, rope_ref, wxbc, wz, wdt, wq, wk, wv, wag, wglu, wcg,
                   xbc_ref, pf_ref, pb_ref, *, nct):
    x = jnp.where(pl.program_id(1) < nct, c_ref[...], x_ref[...])
    y = x * lax.rsqrt(jnp.mean(x * x, axis=-1, keepdims=True) + EPS) * nw_ref[...]
    h = (y * (1.0 + mod_ref[1:2, :]) + mod_ref[0:1, :]).astype(BF16)
    xbc_ref[...] = _dot(h, wxbc[...])
    pf_ref[:, PF_Z:PF_Z + SSD_WIDTH] = _dot(h, wz[...])
    pf_ref[:, PF_DT:PF_DT + 2 * LANES] = _dot(h, wdt[...])
    pf_ref[:, PF_AG:PF_AG + ATT_WIDTH] = _dot(h, wag[...])
    pf_ref[:, PF_GLU:PF_GLU + 2 * CONV_WIDTH] = _dot(h, wglu[...])
    pf_ref[:, PF_CG:PF_CG + CONV_WIDTH] = _dot(h, wcg[...])
    pb_ref[:, PB_KV + ATT_KV_WIDTH:PB_KV + 2 * ATT_KV_WIDTH] = _dot(h, wv[...]).astype(BF16)
    cos = rope_ref[:, :LANES]
    sin = rope_ref[:, LANES:]
    pb_ref[:, PB_KV:PB_KV + ATT_KV_WIDTH] = _rope(_dot(h, wk[...]), cos, sin).astype(BF16)
    reps = ATT_WIDTH // LANES
    cos_q = jnp.concatenate([cos] * reps, axis=1)
    sin_q = jnp.concatenate([sin] * reps, axis=1)
    pb_ref[:, PB_Q:PB_Q + ATT_WIDTH] = (_rope(_dot(h, wq[...]), cos_q, sin_q) * ATT_SCALE).astype(BF16)


def _token_specs(tm, d, nct, lat_off, tile_off):
    return [
        pl.BlockSpec((None, tm, d), lambda b, i: (b, jnp.minimum(i + tile_off, nct - 1), 0)),
        pl.BlockSpec((None, tm, d), lambda b, i: (b, jnp.maximum(i + tile_off - nct, 0) + lat_off, 0)),
    ]


def _inproj(csrc, xsrc, lat_off, t, modsel, norm_w, rope_t, weights, nct):
    bsz, _, d = xsrc.shape
    tm = ROW_TILE
    row_map = lambda b, i: (b, i, 0)
    const2 = lambda b, i: (0, 0)
    in_specs = _token_specs(tm, d, nct, lat_off, 0) + [
        pl.BlockSpec((None, None, 3, d), lambda b, i: (b, jnp.where(i >= nct, 1, 0), 0, 0)),
        pl.BlockSpec((1, d), const2),
        pl.BlockSpec((tm, 2 * LANES), lambda b, i: (i, 0)),
    ] + [pl.BlockSpec((d, w.shape[1]), const2) for w in weights]
    widths = (SSD_XBC, PF_WIDTH, PB_WIDTH)
    return pl.pallas_call(
        functools.partial(_inproj_kernel, nct=nct),
        grid=(bsz, t // tm),
        in_specs=in_specs,
        out_specs=[pl.BlockSpec((None, tm, w), row_map) for w in widths],
        out_shape=[jax.ShapeDtypeStruct((bsz, t, w), dt) for w, dt in zip(widths, (F32, F32, BF16))],
        compiler_params=_cparams("parallel", "arbitrary"),
        name="inproj",
    )(csrc, xsrc, modsel, norm_w, rope_t, *weights)


def _ssd_chunk(xs, bc, dt_raw, dtb, alog, fwd, st_ref):
    q = SSD_CHUNK
    bb = bc[:, :SSD_GN]
    cb16 = bc[:, SSD_GN:]

    lane = lax.broadcasted_iota(jnp.int32, (q, LANES), 1)
    head_lane = lane < 3 * SSD_HEADS
    dt = jnp.where(head_lane, jax.nn.softplus(dt_raw + dtb), 0.0)
    da = dt * (-jnp.exp(alog))
    li = lax.broadcasted_iota(jnp.int32, (q, q), 0)
    si = lax.broadcasted_iota(jnp.int32, (q, q), 1)
    causal = (si <= li) if fwd else (si >= li)
    ones = jnp.where(causal, 1.0, 0.0).astype(BF16)
    da_hi, da_mid, da_lo = _split3(da)
    cs = _dot(ones, da_hi) + _dot(ones, da_mid) + _dot(ones, da_lo)
    total = jnp.sum(da, axis=0, keepdims=True)
    to_end = jnp.exp(total - cs)
    from_start = jnp.where(head_lane, jnp.exp(cs), 0.0)
    chunk_decay = jnp.where(head_lane[:SUBLANES], jnp.exp(total), 0.0)
    cs_t = cs.T

    er = lax.broadcasted_iota(jnp.int32, (LANES, SSD_WIDTH), 0)
    ec = lax.broadcasted_iota(jnp.int32, (LANES, SSD_WIDTH), 1)
    spread = jnp.where((ec // SSD_HEAD_DIM == er % SSD_HEADS) & (er < 3 * SSD_HEADS), 1.0, 0.0).astype(BF16)
    dt_e = _dot(_pack3(dt, SSD_HEADS), spread)
    stacked = jnp.concatenate([dt * to_end, from_start, chunk_decay], axis=0)
    stacked_e = _dot(_pack3(stacked, SSD_HEADS), spread)
    w_e = stacked_e[:q]
    fs_e = stacked_e[q:2 * q]
    cd_e = stacked_e[2 * q:2 * q + 1]

    xd = (xs * dt_e).astype(BF16)
    xdw = (xs * w_e).astype(BF16)
    low = lax.broadcasted_iota(jnp.int32, (q, LANES), 1) < SSD_HEAD_DIM
    zero16 = jnp.zeros((q, LANES), BF16)
    ys = []
    for g in range(SSD_GROUPS):
        gs = slice(g * SSD_STATE, (g + 1) * SSD_STATE)
        gc = slice(g * GROUP_COLS, (g + 1) * GROUP_COLS)
        scores = _dot_nt(cb16[:, gs], bb[:, gs])
        entering = st_ref[g]
        y_off = _dot(cb16[:, gs], entering.astype(BF16))
        new_state = _dot(bb[:, gs].astype(F32).T.astype(BF16), xdw[:, gc])
        st_ref[g] = entering * cd_e[:, gc] + new_state
        for pr in range(GROUP_COLS // LANES):
            h0 = (g * GROUP_COLS + pr * LANES) // SSD_HEAD_DIM
            cols = slice(h0 * SSD_HEAD_DIM, h0 * SSD_HEAD_DIM + LANES)
            xpair = xd[:, cols]
            m0 = (scores * jnp.exp(jnp.where(causal, cs[:, h0:h0 + 1] - cs_t[h0:h0 + 1, :], -jnp.inf))).astype(BF16)
            m1 = (scores * jnp.exp(jnp.where(causal, cs[:, h0 + 1:h0 + 2] - cs_t[h0 + 1:h0 + 2, :], -jnp.inf))).astype(BF16)
            block_diag = jnp.concatenate([jnp.where(low, xpair, zero16), jnp.where(low, zero16, xpair)], axis=0)
            y_diag = _dot(jnp.concatenate([m0, m1], axis=1), block_diag)
            ys.append(y_diag + y_off[:, pr * LANES:(pr + 1) * LANES] * fs_e[:, cols])
    return jnp.concatenate(ys, axis=1)


def _ssd_bwd_block(k, ncb, nb):
    return jnp.where(k < ncb, ncb - 1 - k, nb + ncb - 1 - k)


def _ssd_kernel(xm_ref, xp_ref, xn_ref, dt_ref, z_ref, cw_ref, cb_ref, dtb_ref, alog_ref, dvec_ref, nw_ref,
                o_ref, yb_ref, st_ref, xs_ref, bc_ref, *, ncb, nb):
    q = SSD_CHUNK
    rows = SSD_STEP
    n_sub = rows // q
    s = pl.program_id(1)

    def conv_block(cb):
        first = (cb == 0) | (cb == ncb)
        last = (cb == ncb - 1) | (cb == nb - 1)
        xext = jnp.concatenate([jnp.where(first, 0.0, xp_ref[...]), xm_ref[...],
                                jnp.where(last, 0.0, xn_ref[...])], axis=0)
        acc = cb_ref[...]
        for k in range(SSD_CONV):
            d = k - (SSD_CONV - 1) // 2
            if d == 0:
                tap = xext[SSD_HALO:SSD_HALO + rows]
            elif d > 0:
                tap = _shift_rows(xext, d)[SSD_HALO:SSD_HALO + rows]
            else:
                tap = _shift_rows(xext, SSD_HALO + d)[:rows]
            acc = acc + cw_ref[k:k + 1, :] * tap
        u = jax.nn.silu(acc)
        dst = pl.ds(pl.multiple_of(cb * rows, rows), rows)
        xs_ref[dst, :] = u[:, :SSD_WIDTH]
        bc_ref[dst, :] = u[:, SSD_WIDTH:].astype(BF16)

    def chunk_rows(blk, j):
        return pl.ds(pl.multiple_of(blk * rows + j * q, q), q)

    @pl.when((s == 0) | (s == nb))
    def _():
        st_ref[...] = jnp.zeros_like(st_ref)

    @pl.when(s <= nb - 1)
    def _():
        conv_block(_ssd_bwd_block(s, ncb, nb))

    @pl.when(s < nb)
    def _():
        blk = _ssd_bwd_block(s, ncb, nb)
        for j in reversed(range(n_sub)):
            y = _ssd_chunk(xs_ref[chunk_rows(blk, j), :], bc_ref[chunk_rows(blk, j), :], dt_ref[j * q:(j + 1) * q, :],
                           dtb_ref[...], alog_ref[...], False, st_ref)
            yb_ref[chunk_rows(blk, j), :] = y

    @pl.when(s >= nb)
    def _():
        blk = s - nb
        for j in range(n_sub):
            xs = xs_ref[chunk_rows(blk, j), :]
            y = _ssd_chunk(xs, bc_ref[chunk_rows(blk, j), :], dt_ref[j * q:(j + 1) * q, :], dtb_ref[...],
                           alog_ref[...], True, st_ref)
            tot = y + yb_ref[chunk_rows(blk, j), :] + dvec_ref[...] * xs
            gated = tot * jax.nn.silu(z_ref[j * q:(j + 1) * q, :])
            outs = []
            for g in range(SSD_GROUPS):
                gc = slice(g * GROUP_COLS, (g + 1) * GROUP_COLS)
                v = gated[:, gc]
                outs.append(v * lax.rsqrt(jnp.mean(v * v, axis=-1, keepdims=True) + EPS) * nw_ref[:, gc])
            o_ref[j * q:(j + 1) * q, :] = jnp.concatenate(outs, axis=1).astype(BF16)


def _ssd(xbc, pf, conv_w, conv_b, dt_bias, a_log, dvec, norm_w, ctx_len):
    bsz, t, _ = pf.shape
    rows = SSD_STEP
    nb = t // rows
    ncb = ctx_len // rows
    hb = rows // SSD_HALO
    nhalo = t // SSD_HALO
    const2 = lambda b, s: (0, 0)
    xblk = lambda s: _ssd_bwd_block(jnp.minimum(s, nb - 1), ncb, nb)
    cur = lambda s: jnp.where(s < nb, _ssd_bwd_block(s, ncb, nb), s - nb)
    phase = lambda s: jnp.where(s < nb, 1, 0)
    fwd_blk = lambda s: jnp.maximum(s - nb, 0)
    in_specs = [
        pl.BlockSpec((None, rows, SSD_XBC), lambda b, s: (b, xblk(s), 0)),
        pl.BlockSpec((None, SSD_HALO, SSD_XBC), lambda b, s: (b, jnp.maximum(xblk(s) * hb - 1, 0), 0)),
        pl.BlockSpec((None, SSD_HALO, SSD_XBC), lambda b, s: (b, jnp.minimum(xblk(s) * hb + hb, nhalo - 1), 0)),
        pl.BlockSpec((None, rows, LANES), lambda b, s: (b, cur(s), PF_DT // LANES + phase(s))),
        pl.BlockSpec((None, rows, SSD_WIDTH), lambda b, s: (b, fwd_blk(s), PF_Z // SSD_WIDTH)),
        pl.BlockSpec((SSD_CONV, SSD_XBC), const2),
        pl.BlockSpec((1, SSD_XBC), const2),
        pl.BlockSpec((None, 1, LANES), lambda b, s: (phase(s), 0, 0)),
        pl.BlockSpec((None, 1, LANES), lambda b, s: (phase(s), 0, 0)),
        pl.BlockSpec((1, SSD_WIDTH), const2),
        pl.BlockSpec((1, SSD_WIDTH), const2),
    ]
    return pl.pallas_call(
        functools.partial(_ssd_kernel, ncb=ncb, nb=nb),
        grid=(bsz, 2 * nb),
        in_specs=in_specs,
        out_specs=pl.BlockSpec((None, rows, SSD_WIDTH), lambda b, s: (b, fwd_blk(s), 0)),
        out_shape=jax.ShapeDtypeStruct((bsz, t, SSD_WIDTH), BF16),
        scratch_shapes=[
            pltpu.VMEM((t, SSD_WIDTH), F32),
            pltpu.VMEM((SSD_GROUPS, SSD_STATE, GROUP_COLS), F32),
            pltpu.VMEM((t, SSD_WIDTH), F32),
            pltpu.VMEM((t, 2 * SSD_GN), BF16),
        ],
        compiler_params=_cparams("parallel", "arbitrary"),
        name="ssd",
    )(xbc, xbc, xbc, pf, pf, conv_w, conv_b, dt_bias, a_log, dvec, norm_w)


def _swap_halves(blk):
    half = blk.shape[-1] // 2
    return jnp.concatenate([blk[:, half:], blk[:, :half]], axis=1)


def _head_pads(blk, g, low):
    zero = jnp.zeros_like(blk)
    sw = _swap_halves(blk)
    if g == 0:
        return jnp.where(low, blk, zero), jnp.where(low, zero, sw)
    return jnp.where(low, sw, zero), jnp.where(low, zero, blk)


def _attn_kernel(q_ref, kvm_ref, kvp_ref, kvn_ref, kvc_ref, ag_ref, sink_ref, o_ref, pad_ref,
                 *, tile_off, nct, seq, ctx_len):
    qb = ATT_TILE
    blk = ATT_QBLOCK
    wk = blk + 2 * WINDOW
    tile = pl.program_id(1) + tile_off
    lat0 = (tile - nct) * qb
    low_k = lax.broadcasted_iota(jnp.int32, (1, LANES), 1) < ATT_HEAD_DIM

    kv = jnp.concatenate([kvc_ref[...], kvp_ref[...], kvm_ref[...], kvn_ref[...]], axis=0)
    for g in range(ATT_KV_HEADS):
        ka, kb = _head_pads(kv[:, :ATT_KV_WIDTH], g, low_k)
        va, vb = _head_pads(kv[:, ATT_KV_WIDTH:], g, low_k)
        pad_ref[g, 0] = ka
        pad_ref[g, 1] = kb
        pad_ref[g, 2] = va
        pad_ref[g, 3] = vb

    r = lax.broadcasted_iota(jnp.int32, (blk, wk), 0)
    col = lax.broadcasted_iota(jnp.int32, (blk, wk), 1)
    in_band = jnp.abs(col - WINDOW - r) <= WINDOW
    neg = jnp.finfo(F32).min
    rep = ATT_HEADS // ATT_KV_HEADS
    pairs_per_group = rep * ATT_HEAD_DIM // LANES
    always = jnp.ones((blk, ctx_len), jnp.bool_)
    for a in range(qb // blk):
        rows = slice(a * blk, (a + 1) * blk)
        kpos = lat0 + a * blk - WINDOW + col
        valid = jnp.concatenate([always, in_band & (kpos >= 0) & (kpos < seq) & (lat0 >= 0)], axis=1)
        win0 = ctx_len + a * blk
        outs = []
        for g in range(ATT_KV_HEADS):
            for pr in range(pairs_per_group):
                p = g * pairs_per_group + pr
                qp = q_ref[rows, p * LANES:(p + 1) * LANES]
                acc = jnp.zeros((blk, LANES), F32)
                invs = []
                for j in range(2):
                    head = 2 * p + j
                    sink = sink_ref[head:head + 1, 0:1]
                    if a == 0:
                        keys = pad_ref[g, j, 0:win0 + wk, :]
                        vals = pad_ref[g, 2 + j, 0:win0 + wk, :]
                    else:
                        keys = jnp.concatenate([pad_ref[g, j, 0:ctx_len, :], pad_ref[g, j, win0:win0 + wk, :]], axis=0)
                        vals = jnp.concatenate([pad_ref[g, 2 + j, 0:ctx_len, :], pad_ref[g, 2 + j, win0:win0 + wk, :]],
                                               axis=0)
                    s = jnp.where(valid, _dot_nt(qp, keys), neg)
                    m = jnp.maximum(jnp.max(s, axis=-1, keepdims=True), sink)
                    pexp = jnp.exp2(s - m)
                    den = jnp.sum(pexp, axis=-1, keepdims=True) + jnp.exp2(sink - m)
                    acc = acc + _dot(pexp.astype(BF16), vals)
                    invs.append(1.0 / den)
                outs.append(acc * jnp.where(low_k, invs[0], invs[1]))
        o = jnp.concatenate(outs, axis=1)
        o_ref[rows, :] = (o * jax.nn.silu(ag_ref[rows, :])).astype(BF16)


def _attention(pf, pb, sink_tab, ctx_len, tile_off):
    bsz, t, _ = pf.shape
    qb = ATT_TILE
    nct = ctx_len // qb
    hb = qb // WINDOW
    ctx_hb = ctx_len // WINDOW
    n_hb = t // WINDOW
    ntiles = t // qb - tile_off
    kvw = 2 * ATT_KV_WIDTH
    kv_col = PB_KV // kvw
    in_specs = [
        pl.BlockSpec((None, qb, ATT_WIDTH), lambda b, i: (b, i + tile_off, PB_Q // ATT_WIDTH)),
        pl.BlockSpec((None, qb, kvw), lambda b, i: (b, i + tile_off, kv_col)),
        pl.BlockSpec((None, WINDOW, kvw), lambda b, i: (b, jnp.maximum((i + tile_off) * hb - 1, ctx_hb), kv_col)),
        pl.BlockSpec((None, WINDOW, kvw), lambda b, i: (b, jnp.minimum((i + tile_off) * hb + hb, n_hb - 1), kv_col)),
        pl.BlockSpec((None, ctx_len, kvw), lambda b, i: (b, 0, kv_col)),
        pl.BlockSpec((None, qb, ATT_WIDTH), lambda b, i: (b, i + tile_off, PF_AG // ATT_WIDTH)),
        pl.BlockSpec((ATT_HEADS, LANES), lambda b, i: (0, 0)),
    ]
    return pl.pallas_call(
        functools.partial(_attn_kernel, tile_off=tile_off, nct=nct, seq=t - ctx_len, ctx_len=ctx_len),
        grid=(bsz, ntiles),
        in_specs=in_specs,
        out_specs=pl.BlockSpec((None, qb, ATT_WIDTH), lambda b, i: (b, i + tile_off, 0)),
        out_shape=jax.ShapeDtypeStruct((bsz, t, ATT_WIDTH), BF16),
        scratch_shapes=[pltpu.VMEM((ATT_KV_HEADS, 4, ctx_len + qb + 2 * WINDOW, LANES), BF16)],
        compiler_params=_cparams("parallel", "arbitrary"),
        name="attention",
    )(pb, pb, pb, pb, pb, pf, sink_tab)


def _conf_kernel(gm_ref, gp_ref, gn_ref, cg_ref, dww_ref, dwb_ref, lnw_ref, lnb_ref, pww_ref, pwb_ref, o_ref,
                 hbuf, abuf, *, tile_off, nct, nt):
    tc = CONF_TILE
    tile = pl.program_id(1) + tile_off
    first = (tile == 0) | (tile == nct)
    last = (tile == nct - 1) | (tile == nt - 1)

    def glu(v):
        return v[:, :CONV_WIDTH] * jax.nn.sigmoid(v[:, CONV_WIDTH:])

    rows = tc + 2 * CONF_HALO
    ext = jnp.concatenate([jnp.where(first, 0.0, glu(gp_ref[...])), glu(gm_ref[...]),
                           jnp.where(last, 0.0, glu(gn_ref[...]))], axis=0)
    hbuf[0] = ext
    for j in range(1, SUBLANES):
        hbuf[j, 0:rows - SUBLANES, :] = _shift_rows(ext, j)
    base = CONF_HALO - CONV_PAD
    for r in range(tc // CONF_SUB):
        acc = jnp.broadcast_to(dwb_ref[...], (CONF_SUB // SUBLANES, SUBLANES, CONV_WIDTH))
        for k in range(CONV_KERNEL):
            blk, res = divmod(base + k, SUBLANES)
            strip = hbuf[res, pl.ds(r * CONF_SUB + blk * SUBLANES, CONF_SUB), :]
            acc = acc + dww_ref[k] * strip.reshape(CONF_SUB // SUBLANES, SUBLANES, CONV_WIDTH)
        acc = acc.reshape(CONF_SUB, CONV_WIDTH)
        xc = acc - jnp.mean(acc, axis=-1, keepdims=True)
        var = jnp.mean(xc * xc, axis=-1, keepdims=True)
        ln = xc * lax.rsqrt(var + EPS) * lnw_ref[...] + lnb_ref[...]
        abuf[pl.ds(r * CONF_SUB, CONF_SUB), :] = jax.nn.silu(ln)
    out = _dot(abuf[...].astype(BF16), pww_ref[...]) + pwb_ref[...]
    o_ref[...] = (out * jax.nn.silu(cg_ref[...])).astype(BF16)


def _conformer(pf, dw_w, dw_b, ln_w, ln_b, pw_w, pw_b, ctx_len, tile_off):
    bsz, t, _ = pf.shape
    tc = CONF_TILE
    nt = t // tc
    nct = ctx_len // tc
    hb = tc // CONF_HALO
    nhalo = t // CONF_HALO
    main = lambda b, i: (b, i + tile_off, 0)
    const2 = lambda b, i: (0, 0)
    glu_col = PF_GLU // (2 * CONV_WIDTH)
    in_specs = [
        pl.BlockSpec((None, tc, 2 * CONV_WIDTH), lambda b, i: (b, i + tile_off, glu_col)),
        pl.BlockSpec((None, CONF_HALO, 2 * CONV_WIDTH),
                     lambda b, i: (b, jnp.maximum((i + tile_off) * hb - 1, 0), glu_col)),
        pl.BlockSpec((None, CONF_HALO, 2 * CONV_WIDTH),
                     lambda b, i: (b, jnp.minimum((i + tile_off) * hb + hb, nhalo - 1), glu_col)),
        pl.BlockSpec((None, tc, CONV_WIDTH), lambda b, i: (b, i + tile_off, PF_CG // CONV_WIDTH)),
        pl.BlockSpec((CONV_KERNEL, SUBLANES, CONV_WIDTH), lambda b, i: (0, 0, 0)),
        pl.BlockSpec((SUBLANES, CONV_WIDTH), const2),
        pl.BlockSpec((1, CONV_WIDTH), const2),
        pl.BlockSpec((1, CONV_WIDTH), const2),
        pl.BlockSpec((CONV_WIDTH, CONV_WIDTH), const2),
        pl.BlockSpec((1, CONV_WIDTH), const2),
    ]
    return pl.pallas_call(
        functools.partial(_conf_kernel, tile_off=tile_off, nct=nct, nt=nt),
        grid=(bsz, nt - tile_off),
        in_specs=in_specs,
        out_specs=pl.BlockSpec((None, tc, CONV_WIDTH), main),
        out_shape=jax.ShapeDtypeStruct((bsz, t, CONV_WIDTH), BF16),
        scratch_shapes=[
            pltpu.VMEM((SUBLANES, tc + 2 * CONF_HALO, CONV_WIDTH), F32),
            pltpu.VMEM((tc, CONV_WIDTH), F32),
        ],
        compiler_params=_cparams("parallel", "arbitrary"),
        name="conformer",
    )(pf, pf, pf, pf, dw_w, dw_b, ln_w, ln_b, pw_w, pw_b)


def _outproj_kernel(c_ref, x_ref, os_ref, oa_ref, oc_ref, mod_ref, ws_ref, wa_ref, wc_ref, fnw_ref, o_ref,
                    *, final, nct, tile_off):
    u = _dot(os_ref[...], ws_ref[...]) + _dot(oa_ref[...], wa_ref[...]) + _dot(oc_ref[...], wc_ref[...])
    x = x_ref[...]
    if tile_off < nct:
        x = jnp.where(pl.program_id(1) + tile_off < nct, c_ref[...], x)
    xn = x + mod_ref[2:3, :] * u
    if final:
        xn = xn * lax.rsqrt(jnp.mean(xn * xn, axis=-1, keepdims=True) + EPS) * fnw_ref[...]
    o_ref[...] = xn


def _outproj(csrc, xsrc, lat_off, o_ssd, o_att, o_conv, modsel, w_s, w_a, w_c, final_norm_w, nct, final):
    bsz, t, _ = o_ssd.shape
    d = xsrc.shape[-1]
    tm = ROW_TILE
    tile_off = nct if final else 0
    main = lambda b, i: (b, i + tile_off, 0)
    const2 = lambda b, i: (0, 0)
    in_specs = _token_specs(tm, d, nct, lat_off, tile_off) + [
        pl.BlockSpec((None, tm, SSD_WIDTH), main),
        pl.BlockSpec((None, tm, ATT_WIDTH), main),
        pl.BlockSpec((None, tm, CONV_WIDTH), main),
        pl.BlockSpec((None, None, 3, d), lambda b, i: (b, jnp.where(i + tile_off >= nct, 1, 0), 0, 0)),
        pl.BlockSpec((SSD_WIDTH, d), const2),
        pl.BlockSpec((ATT_WIDTH, d), const2),
        pl.BlockSpec((CONV_WIDTH, d), const2),
        pl.BlockSpec((1, d), const2),
    ]
    rows_out = t - tile_off * tm
    return pl.pallas_call(
        functools.partial(_outproj_kernel, final=final, nct=nct, tile_off=tile_off),
        grid=(bsz, rows_out // tm),
        in_specs=in_specs,
        out_specs=pl.BlockSpec((None, tm, d), lambda b, i: (b, i, 0)),
        out_shape=jax.ShapeDtypeStruct((bsz, rows_out, d), F32),
        compiler_params=_cparams("parallel", "arbitrary"),
        name="outproj_final" if final else "outproj",
    )(csrc, xsrc, o_ssd, o_att, o_conv, modsel, w_s, w_a, w_c, final_norm_w)


def _rope_tables(seq, ctx_len):
    rows = seq // GRID_W
    row = jnp.repeat(jnp.arange(rows, dtype=F32), GRID_W)
    col = jnp.tile(jnp.arange(GRID_W, dtype=F32), rows)
    axis_dim = ATT_HEAD_DIM // 2
    inv_freq = ROPE_BASE ** (-jnp.arange(0, axis_dim, 2, dtype=F32) / axis_dim)
    ar = row[:, None] * inv_freq
    ac = col[:, None] * inv_freq
    cos = jnp.concatenate([jnp.cos(ar), jnp.cos(ar), jnp.cos(ac), jnp.cos(ac)], axis=1)
    sin = jnp.concatenate([-jnp.sin(ar), jnp.sin(ar), -jnp.sin(ac), jnp.sin(ac)], axis=1)
    reps = LANES // ATT_HEAD_DIM
    cos = jnp.concatenate([jnp.ones((ctx_len, LANES), F32), jnp.tile(cos, (1, reps))], axis=0)
    sin = jnp.concatenate([jnp.zeros((ctx_len, LANES), F32), jnp.tile(sin, (1, reps))], axis=0)
    return jnp.concatenate([cos, sin], axis=1)


def _head_lanes(a):
    rep = jnp.concatenate([a, a, a], axis=-1)
    return jnp.pad(rep, [(0, 0)] * (a.ndim - 1) + [(0, LANES - 3 * SSD_HEADS)])


def _split_w_in(w_in):
    sizes = (SSD_XBC, SSD_WIDTH, 2 * SSD_HEADS, ATT_WIDTH, ATT_KV_WIDTH, ATT_KV_WIDTH, ATT_WIDTH,
             2 * CONV_WIDTH, CONV_WIDTH)
    parts, off = [], 0
    for size in sizes:
        parts.append(w_in[:, off:off + size])
        off += size
    dt_w = parts[2]
    parts[2] = jnp.concatenate([_head_lanes(dt_w[:, :SSD_HEADS]), _head_lanes(dt_w[:, SSD_HEADS:])], axis=1)
    return [p.astype(BF16) for p in parts]


def kernel(x, c, ctx, c_ctx, w_mod, b_mod, norm_w, w_in, ssd_conv_w, ssd_conv_b, ssd_dt_bias, ssd_a_log, ssd_d,
           ssd_norm_w, attn_sink, conv_dw_w, conv_dw_b, conv_ln_w, conv_ln_b, conv_pw_w, conv_pw_b, w_out,
           final_norm_w):
    bsz, seq, d = x.shape
    ctx_len = ctx.shape[1]
    depth = w_in.shape[0]
    assert ctx_len % ROW_TILE == 0 and seq % ROW_TILE == 0 and seq % GRID_W == 0
    assert ROW_TILE == ATT_TILE == CONF_TILE
    nct = ctx_len // ROW_TILE

    cond_rows = -(-(bsz + 1) // SUBLANES) * SUBLANES
    cond = jnp.zeros((cond_rows, d), F32).at[:bsz].set(c).at[bsz].set(c_ctx)
    mod = _modulation(cond, w_mod, b_mod)
    rope_t = _rope_tables(seq, ctx_len)
    csrc, xsrc, lat_off = ctx, x, 0

    out = None
    for l in range(depth):
        final = l == depth - 1
        lat = mod[l, :bsz]
        ctxm = jnp.broadcast_to(mod[l, bsz], lat.shape)
        modsel = jnp.stack([ctxm, lat], axis=1).reshape(bsz, 2, 3, d)
        weights = _split_w_in(w_in[l])
        xbc, pf, pb = _inproj(csrc, xsrc, lat_off, ctx_len + seq, modsel, norm_w[l][None], rope_t, weights, nct)

        o_ssd = _ssd(xbc, pf, ssd_conv_w[l], ssd_conv_b[l][None], _head_lanes(ssd_dt_bias[l])[:, None],
                     _head_lanes(ssd_a_log[l])[:, None], jnp.repeat(ssd_d[l], SSD_HEAD_DIM)[None],
                     ssd_norm_w[l][None], ctx_len)
        tile_off = nct if final else 0
        sink_tab = jnp.broadcast_to(attn_sink[l][:, None] * LOG2E, (ATT_HEADS, LANES))
        o_att = _attention(pf, pb, sink_tab, ctx_len, tile_off)
        dw_w = jnp.broadcast_to(conv_dw_w[l][:, None, :], (CONV_KERNEL, SUBLANES, CONV_WIDTH))
        dw_b = jnp.broadcast_to(conv_dw_b[l][None], (SUBLANES, CONV_WIDTH))
        o_conv = _conformer(pf, dw_w, dw_b, conv_ln_w[l][None], conv_ln_b[l][None],
                            conv_pw_w[l].astype(BF16), conv_pw_b[l][None], ctx_len, tile_off)
        w_o = w_out[l].astype(BF16)
        out = _outproj(csrc, xsrc, lat_off, o_ssd, o_att, o_conv, modsel, w_o[:SSD_WIDTH],
                       w_o[SSD_WIDTH:SSD_WIDTH + ATT_WIDTH], w_o[SSD_WIDTH + ATT_WIDTH:], final_norm_w[None], nct, final)
        csrc, xsrc, lat_off = out, out, nct
    return out
```

```python
import functools

import jax
import jax.numpy as jnp
from jax import lax
from jax.experimental import pallas as pl
from jax.experimental.pallas import tpu as pltpu

F32 = jnp.float32
BF16 = jnp.bfloat16

GRID_W = 64
EPS = 1e-6

SSD_HEADS = 16
SSD_HEAD_DIM = 64
SSD_WIDTH = SSD_HEADS * SSD_HEAD_DIM
SSD_GROUPS = 2
SSD_STATE = 128
SSD_CONV = 5
SSD_CHUNK = 128
SSD_GN = SSD_GROUPS * SSD_STATE
SSD_XBC = SSD_WIDTH + 2 * SSD_GN
GROUP_COLS = SSD_WIDTH // SSD_GROUPS

ATT_HEADS = 8
ATT_KV_HEADS = 2
ATT_HEAD_DIM = 64
ATT_WIDTH = ATT_HEADS * ATT_HEAD_DIM
ATT_KV_WIDTH = ATT_KV_HEADS * ATT_HEAD_DIM
WINDOW = 128
ROPE_BASE = 10000.0
LOG2E = 1.4426950408889634
ATT_SCALE = ATT_HEAD_DIM ** -0.5 * LOG2E

CONV_WIDTH = 512
CONV_KERNEL = 31
CONV_PAD = (CONV_KERNEL - 1) // 2

LANES = 128

PF_Z = 0
PF_GLU = PF_Z + SSD_WIDTH
PF_AG = PF_GLU + 2 * CONV_WIDTH
PF_CG = PF_AG + ATT_WIDTH
PF_DT = PF_CG + CONV_WIDTH
PF_WIDTH = PF_DT + 2 * LANES
PB_Q = 0
PB_KV = PB_Q + ATT_WIDTH
PB_WIDTH = PB_KV + 2 * ATT_KV_WIDTH

W_XBC = 0
W_Z = W_XBC + SSD_XBC
W_Q = W_Z + SSD_WIDTH
W_K = W_Q + ATT_WIDTH
W_V = W_K + ATT_KV_WIDTH
W_AG = W_V + ATT_KV_WIDTH
W_GLU = W_AG + ATT_WIDTH
W_CG = W_GLU + 2 * CONV_WIDTH
W_DT = W_CG + CONV_WIDTH
W_WIDTH = W_DT + 2 * LANES
SUBLANES = 8
VMEM_LIMIT = 56 * 1024 * 1024

ROW_TILE = 256
ATT_TILE = 256
ATT_QBLOCK = 256
CONF_TILE = 256
CONF_HALO = 16
CONF_SUB = 32
SSD_HALO = SUBLANES
SSD_STEP = 2 * SSD_CHUNK


def _dot(a, b):
    return jnp.dot(a, b, preferred_element_type=F32)


def _dot_nt(a, b):
    return lax.dot_general(a, b, (((1,), (1,)), ((), ())), preferred_element_type=F32)


def _split3(x):
    hi = x.astype(BF16)
    r1 = x - hi.astype(F32)
    mid = r1.astype(BF16)
    lo = (r1 - mid.astype(F32)).astype(BF16)
    return hi, mid, lo


def _pack3(x, width):
    hi, mid, lo = _split3(x)
    lane = lax.broadcasted_iota(jnp.int32, x.shape, 1)
    return jnp.where(lane < width, hi, jnp.where(lane < 2 * width, mid, lo))


def _shift_rows(x, j):
    rows, w = x.shape
    g = x.reshape(rows // SUBLANES, SUBLANES, w)
    r = pltpu.roll(g, SUBLANES - j, 1)
    sub = lax.broadcasted_iota(jnp.int32, (1, SUBLANES, w), 1)
    return jnp.where(sub < SUBLANES - j, r[:-1], r[1:]).reshape(rows - SUBLANES, w)


def _cparams(*sem, flags=None):
    return pltpu.CompilerParams(dimension_semantics=sem, vmem_limit_bytes=VMEM_LIMIT, flags=flags)


def _mod_kernel(c_ref, w_ref, b_ref, o_ref):
    a = jax.nn.silu(c_ref[...]).astype(BF16)
    o_ref[...] = _dot(a, w_ref[...].astype(BF16)) + b_ref[...]


def _modulation(cond, w_mod, b_mod):
    depth, d, n = w_mod.shape
    rows = cond.shape[0]
    tn = n // 2 if (n // 2) % LANES == 0 else n
    return pl.pallas_call(
        _mod_kernel,
        grid=(depth, n // tn),
        in_specs=[
            pl.BlockSpec((rows, d), lambda l, j: (0, 0)),
            pl.BlockSpec((None, d, tn), lambda l, j: (l, 0, j)),
            pl.BlockSpec((None, 1, tn), lambda l, j: (l, 0, j)),
        ],
        out_specs=pl.BlockSpec((None, rows, tn), lambda l, j: (l, 0, j)),
        out_shape=jax.ShapeDtypeStruct((depth, rows, n), F32),
        compiler_params=_cparams("arbitrary", "arbitrary"),
        name="modulation",
    )(cond, w_mod, b_mod.reshape(depth, 1, n))


def _rope(t, cos, sin):
    width = t.shape[-1]
    lane = lax.broadcasted_iota(jnp.int32, t.shape, 1)
    first = (lane % 32) < 16
    rot = jnp.where(first, pltpu.roll(t, width - 16, 1), pltpu.roll(t, 16, 1))
    return t * cos + rot * sin


def _inproj_kernel(c_ref, x_ref, mod_ref, nw_ref, rope_ref, w_ref, xbc_ref, pf_ref, pb_ref, *, nct):
    wxbc = w_ref.at[:, W_XBC:W_XBC + SSD_XBC]
    wz = w_ref.at[:, W_Z:W_Z + SSD_WIDTH]
    wq = w_ref.at[:, W_Q:W_Q + ATT_WIDTH]
    wk = w_ref.at[:, W_K:W_K + ATT_KV_WIDTH]
    wv = w_ref.at[:, W_V:W_V + ATT_KV_WIDTH]
    wag = w_ref.at[:, W_AG:W_AG + ATT_WIDTH]
    wglu = w_ref.at[:, W_GLU:W_GLU + 2 * CONV_WIDTH]
    wcg = w_ref.at[:, W_CG:W_CG + CONV_WIDTH]
    wdt = w_ref.at[:, W_DT:W_DT + 2 * LANES]
    x = jnp.where(pl.program_id(1) < nct, c_ref[...], x_ref[...])
    y = x * lax.rsqrt(jnp.mean(x * x, axis=-1, keepdims=True) + EPS) * nw_ref[...]
    h = (y * (1.0 + mod_ref[1:2, :]) + mod_ref[0:1, :]).astype(BF16)
    xbc_ref[...] = _dot(h, wxbc[...])
    pf_ref[:, PF_Z:PF_Z + SSD_WIDTH] = _dot(h, wz[...])
    pf_ref[:, PF_DT:PF_DT + 2 * LANES] = _dot(h, wdt[...])
    pf_ref[:, PF_AG:PF_AG + ATT_WIDTH] = _dot(h, wag[...])
    pf_ref[:, PF_GLU:PF_GLU + 2 * CONV_WIDTH] = _dot(h, wglu[...])
    pf_ref[:, PF_CG:PF_CG + CONV_WIDTH] = _dot(h, wcg[...])
    pb_ref[:, PB_KV + ATT_KV_WIDTH:PB_KV + 2 * ATT_KV_WIDTH] = _dot(h, wv[...]).astype(BF16)
    cos = rope_ref[:, :LANES]
    sin = rope_ref[:, LANES:]
    pb_ref[:, PB_KV:PB_KV + ATT_KV_WIDTH] = _rope(_dot(h, wk[...]), cos, sin).astype(BF16)
    reps = ATT_WIDTH // LANES
    cos_q = jnp.concatenate([cos] * reps, axis=1)
    sin_q = jnp.concatenate([sin] * reps, axis=1)
    pb_ref[:, PB_Q:PB_Q + ATT_WIDTH] = (_rope(_dot(h, wq[...]), cos_q, sin_q) * ATT_SCALE).astype(BF16)


def _token_specs(tm, d, nct, lat_off, tile_off):
    return [
        pl.BlockSpec((None, tm, d), lambda b, i: (b, jnp.minimum(i + tile_off, nct - 1), 0)),
        pl.BlockSpec((None, tm, d), lambda b, i: (b, jnp.maximum(i + tile_off - nct, 0) + lat_off, 0)),
    ]


def _inproj(csrc, xsrc, lat_off, t, modsel, norm_w, rope_t, w_packed, layer, nct):
    bsz, _, d = xsrc.shape
    tm = ROW_TILE
    row_map = lambda b, i: (b, i, 0)
    in_specs = _token_specs(tm, d, nct, lat_off, 0) + [
        pl.BlockSpec((None, None, 3, d), lambda b, i: (b, jnp.where(i >= nct, 1, 0), 0, 0)),
        pl.BlockSpec((None, 1, d), lambda b, i: (layer, 0, 0)),
        pl.BlockSpec((tm, 2 * LANES), lambda b, i: (i, 0)),
        pl.BlockSpec((None, d, W_WIDTH), lambda b, i: (layer, 0, 0)),
    ]
    widths = (SSD_XBC, PF_WIDTH, PB_WIDTH)
    return pl.pallas_call(
        functools.partial(_inproj_kernel, nct=nct),
        grid=(bsz, t // tm),
        in_specs=in_specs,
        out_specs=[pl.BlockSpec((None, tm, w), row_map) for w in widths],
        out_shape=[jax.ShapeDtypeStruct((bsz, t, w), dt) for w, dt in zip(widths, (F32, F32, BF16))],
        compiler_params=_cparams("parallel", "arbitrary"),
        name="inproj",
    )(csrc, xsrc, modsel, norm_w, rope_t, w_packed)


def _ssd_chunk(xs, bc, dt_raw, dtb, alog, fwd, st_ref):
    q = SSD_CHUNK
    bb = bc[:, :SSD_GN]
    cb16 = bc[:, SSD_GN:]

    lane = lax.broadcasted_iota(jnp.int32, (q, LANES), 1)
    head_lane = lane < 3 * SSD_HEADS
    dt = jnp.where(head_lane, jax.nn.softplus(dt_raw + dtb), 0.0)
    da = dt * (-jnp.exp(alog))
    li = lax.broadcasted_iota(jnp.int32, (q, q), 0)
    si = lax.broadcasted_iota(jnp.int32, (q, q), 1)
    causal = (si <= li) if fwd else (si >= li)
    ones = jnp.where(causal, 1.0, 0.0).astype(BF16)
    da_hi, da_mid, da_lo = _split3(da)
    cs = _dot(ones, da_hi) + _dot(ones, da_mid) + _dot(ones, da_lo)
    total = jnp.sum(da, axis=0, keepdims=True)
    to_end = jnp.exp(total - cs)
    from_start = jnp.where(head_lane, jnp.exp(cs), 0.0)
    chunk_decay = jnp.where(head_lane[:SUBLANES], jnp.exp(total), 0.0)
    cs_t = cs.T

    er = lax.broadcasted_iota(jnp.int32, (LANES, SSD_WIDTH), 0)
    ec = lax.broadcasted_iota(jnp.int32, (LANES, SSD_WIDTH), 1)
    spread = jnp.where((ec // SSD_HEAD_DIM == er % SSD_HEADS) & (er < 3 * SSD_HEADS), 1.0, 0.0).astype(BF16)
    dt_e = _dot(_pack3(dt, SSD_HEADS), spread)
    stacked = jnp.concatenate([dt * to_end, from_start, chunk_decay], axis=0)
    stacked_e = _dot(_pack3(stacked, SSD_HEADS), spread)
    w_e = stacked_e[:q]
    fs_e = stacked_e[q:2 * q]
    cd_e = stacked_e[2 * q:2 * q + 1]

    xd = (xs * dt_e).astype(BF16)
    xdw = (xs * w_e).astype(BF16)
    low = lax.broadcasted_iota(jnp.int32, (q, LANES), 1) < SSD_HEAD_DIM
    zero16 = jnp.zeros((q, LANES), BF16)
    ys = []
    for g in range(SSD_GROUPS):
        gs = slice(g * SSD_STATE, (g + 1) * SSD_STATE)
        gc = slice(g * GROUP_COLS, (g + 1) * GROUP_COLS)
        scores = _dot_nt(cb16[:, gs], bb[:, gs])
        entering = st_ref[g]
        y_off = _dot(cb16[:, gs], entering.astype(BF16))
        new_state = _dot(bb[:, gs].astype(F32).T.astype(BF16), xdw[:, gc])
        st_ref[g] = entering * cd_e[:, gc] + new_state
        for pr in range(GROUP_COLS // LANES):
            h0 = (g * GROUP_COLS + pr * LANES) // SSD_HEAD_DIM
            cols = slice(h0 * SSD_HEAD_DIM, h0 * SSD_HEAD_DIM + LANES)
            xpair = xd[:, cols]
            m0 = (scores * jnp.exp(jnp.where(causal, cs[:, h0:h0 + 1] - cs_t[h0:h0 + 1, :], -jnp.inf))).astype(BF16)
            m1 = (scores * jnp.exp(jnp.where(causal, cs[:, h0 + 1:h0 + 2] - cs_t[h0 + 1:h0 + 2, :], -jnp.inf))).astype(BF16)
            block_diag = jnp.concatenate([jnp.where(low, xpair, zero16), jnp.where(low, zero16, xpair)], axis=0)
            y_diag = _dot(jnp.concatenate([m0, m1], axis=1), block_diag)
            ys.append(y_diag + y_off[:, pr * LANES:(pr + 1) * LANES] * fs_e[:, cols])
    return jnp.concatenate(ys, axis=1)


def _ssd_bwd_block(k, ncb, nb):
    return jnp.where(k < ncb, ncb - 1 - k, nb + ncb - 1 - k)


def _ssd_kernel(xm_ref, xp_ref, xn_ref, dt_ref, z_ref, cw_ref, cb_ref, dtb_ref, alog_ref, dvec_ref, nw_ref,
                o_ref, yb_ref, st_ref, xs_ref, bc_ref, *, ncb, nb):
    q = SSD_CHUNK
    rows = SSD_STEP
    n_sub = rows // q
    s = pl.program_id(1)

    def conv_block(cb):
        first = (cb == 0) | (cb == ncb)
        last = (cb == ncb - 1) | (cb == nb - 1)
        xext = jnp.concatenate([jnp.where(first, 0.0, xp_ref[...]), xm_ref[...],
                                jnp.where(last, 0.0, xn_ref[...])], axis=0)
        acc = cb_ref[...]
        for k in range(SSD_CONV):
            d = k - (SSD_CONV - 1) // 2
            if d == 0:
                tap = xext[SSD_HALO:SSD_HALO + rows]
            elif d > 0:
                tap = _shift_rows(xext, d)[SSD_HALO:SSD_HALO + rows]
            else:
                tap = _shift_rows(xext, SSD_HALO + d)[:rows]
            acc = acc + cw_ref[k:k + 1, :] * tap
        u = jax.nn.silu(acc)
        dst = pl.ds(pl.multiple_of(cb * rows, rows), rows)
        xs_ref[dst, :] = u[:, :SSD_WIDTH]
        bc_ref[dst, :] = u[:, SSD_WIDTH:].astype(BF16)

    def chunk_rows(blk, j):
        return pl.ds(pl.multiple_of(blk * rows + j * q, q), q)

    @pl.when((s == 0) | (s == nb))
    def _():
        st_ref[...] = jnp.zeros_like(st_ref)

    @pl.when(s <= nb - 1)
    def _():
        conv_block(_ssd_bwd_block(s, ncb, nb))

    @pl.when(s < nb)
    def _():
        blk = _ssd_bwd_block(s, ncb, nb)
        for j in reversed(range(n_sub)):
            y = _ssd_chunk(xs_ref[chunk_rows(blk, j), :], bc_ref[chunk_rows(blk, j), :], dt_ref[j * q:(j + 1) * q, :],
                           dtb_ref[...], alog_ref[...], False, st_ref)
            yb_ref[chunk_rows(blk, j), :] = y

    @pl.when(s >= nb)
    def _():
        blk = s - nb
        for j in range(n_sub):
            xs = xs_ref[chunk_rows(blk, j), :]
            y = _ssd_chunk(xs, bc_ref[chunk_rows(blk, j), :], dt_ref[j * q:(j + 1) * q, :], dtb_ref[...],
                           alog_ref[...], True, st_ref)
            tot = y + yb_ref[chunk_rows(blk, j), :] + dvec_ref[...] * xs
            gated = tot * jax.nn.silu(z_ref[j * q:(j + 1) * q, :])
            outs = []
            for g in range(SSD_GROUPS):
                gc = slice(g * GROUP_COLS, (g + 1) * GROUP_COLS)
                v = gated[:, gc]
                outs.append(v * lax.rsqrt(jnp.mean(v * v, axis=-1, keepdims=True) + EPS) * nw_ref[:, gc])
            o_ref[j * q:(j + 1) * q, :] = jnp.concatenate(outs, axis=1).astype(BF16)


def _ssd(xbc, pf, conv_w, conv_b, dt_bias, a_log, dvec, norm_w, ctx_len):
    bsz, t, _ = pf.shape
    rows = SSD_STEP
    nb = t // rows
    ncb = ctx_len // rows
    hb = rows // SSD_HALO
    nhalo = t // SSD_HALO
    const2 = lambda b, s: (0, 0)
    xblk = lambda s: _ssd_bwd_block(jnp.minimum(s, nb - 1), ncb, nb)
    cur = lambda s: jnp.where(s < nb, _ssd_bwd_block(s, ncb, nb), s - nb)
    phase = lambda s: jnp.where(s < nb, 1, 0)
    fwd_blk = lambda s: jnp.maximum(s - nb, 0)
    in_specs = [
        pl.BlockSpec((None, rows, SSD_XBC), lambda b, s: (b, xblk(s), 0)),
        pl.BlockSpec((None, SSD_HALO, SSD_XBC), lambda b, s: (b, jnp.maximum(xblk(s) * hb - 1, 0), 0)),
        pl.BlockSpec((None, SSD_HALO, SSD_XBC), lambda b, s: (b, jnp.minimum(xblk(s) * hb + hb, nhalo - 1), 0)),
        pl.BlockSpec((None, rows, LANES), lambda b, s: (b, cur(s), PF_DT // LANES + phase(s))),
        pl.BlockSpec((None, rows, SSD_WIDTH), lambda b, s: (b, fwd_blk(s), PF_Z // SSD_WIDTH)),
        pl.BlockSpec((SSD_CONV, SSD_XBC), const2),
        pl.BlockSpec((1, SSD_XBC), const2),
        pl.BlockSpec((None, 1, LANES), lambda b, s: (phase(s), 0, 0)),
        pl.BlockSpec((None, 1, LANES), lambda b, s: (phase(s), 0, 0)),
        pl.BlockSpec((1, SSD_WIDTH), const2),
        pl.BlockSpec((1, SSD_WIDTH), const2),
    ]
    return pl.pallas_call(
        functools.partial(_ssd_kernel, ncb=ncb, nb=nb),
        grid=(bsz, 2 * nb),
        in_specs=in_specs,
        out_specs=pl.BlockSpec((None, rows, SSD_WIDTH), lambda b, s: (b, fwd_blk(s), 0)),
        out_shape=jax.ShapeDtypeStruct((bsz, t, SSD_WIDTH), BF16),
        scratch_shapes=[
            pltpu.VMEM((t, SSD_WIDTH), F32),
            pltpu.VMEM((SSD_GROUPS, SSD_STATE, GROUP_COLS), F32),
            pltpu.VMEM((t, SSD_WIDTH), F32),
            pltpu.VMEM((t, 2 * SSD_GN), BF16),
        ],
        compiler_params=_cparams("parallel", "arbitrary"),
        name="ssd",
    )(xbc, xbc, xbc, pf, pf, conv_w, conv_b, dt_bias, a_log, dvec, norm_w)


def _swap_halves(blk):
    half = blk.shape[-1] // 2
    return jnp.concatenate([blk[:, half:], blk[:, :half]], axis=1)


def _head_pads(blk, g, low):
    zero = jnp.zeros_like(blk)
    sw = _swap_halves(blk)
    if g == 0:
        return jnp.where(low, blk, zero), jnp.where(low, zero, sw)
    return jnp.where(low, sw, zero), jnp.where(low, zero, blk)


def _attn_kernel(q_ref, kvm_ref, kvp_ref, kvn_ref, kvc_ref, ag_ref, sink_ref, o_ref, pad_ref,
                 *, tile_off, nct, seq, ctx_len):
    qb = ATT_TILE
    blk = ATT_QBLOCK
    wk = blk + 2 * WINDOW
    tile = pl.program_id(1) + tile_off
    lat0 = (tile - nct) * qb
    low_k = lax.broadcasted_iota(jnp.int32, (1, LANES), 1) < ATT_HEAD_DIM

    kv = jnp.concatenate([kvc_ref[...], kvp_ref[...], kvm_ref[...], kvn_ref[...]], axis=0)
    for g in range(ATT_KV_HEADS):
        ka, kb = _head_pads(kv[:, :ATT_KV_WIDTH], g, low_k)
        va, vb = _head_pads(kv[:, ATT_KV_WIDTH:], g, low_k)
        pad_ref[g, 0] = ka
        pad_ref[g, 1] = kb
        pad_ref[g, 2] = va
        pad_ref[g, 3] = vb

    r = lax.broadcasted_iota(jnp.int32, (blk, wk), 0)
    col = lax.broadcasted_iota(jnp.int32, (blk, wk), 1)
    in_band = jnp.abs(col - WINDOW - r) <= WINDOW
    neg = jnp.finfo(F32).min
    rep = ATT_HEADS // ATT_KV_HEADS
    pairs_per_group = rep * ATT_HEAD_DIM // LANES
    always = jnp.ones((blk, ctx_len), jnp.bool_)
    for a in range(qb // blk):
        rows = slice(a * blk, (a + 1) * blk)
        kpos = lat0 + a * blk - WINDOW + col
        valid = jnp.concatenate([always, in_band & (kpos >= 0) & (kpos < seq) & (lat0 >= 0)], axis=1)
        win0 = ctx_len + a * blk
        def operand(g, slot):
            if a == 0:
                return pad_ref[g, slot, 0:win0 + wk, :]
            return jnp.concatenate([pad_ref[g, slot, 0:ctx_len, :], pad_ref[g, slot, win0:win0 + wk, :]], axis=0)

        outs = []
        for g in range(ATT_KV_HEADS):
            for pr in range(pairs_per_group):
                p = g * pairs_per_group + pr
                qp = q_ref[rows, p * LANES:(p + 1) * LANES]
                acc = jnp.zeros((blk, LANES), F32)
                invs = []
                for j in range(2):
                    head = 2 * p + j
                    sink = sink_ref[head:head + 1, 0:1]
                    s = jnp.where(valid, _dot_nt(qp, operand(g, j)), neg)
                    m = jnp.maximum(jnp.max(s, axis=-1, keepdims=True), sink)
                    pexp = jnp.exp2(s - m)
                    den = jnp.sum(pexp, axis=-1, keepdims=True) + jnp.exp2(sink - m)
                    acc = acc + _dot(pexp.astype(BF16), operand(g, 2 + j))
                    invs.append(1.0 / den)
                outs.append(acc * jnp.where(low_k, invs[0], invs[1]))
        o = jnp.concatenate(outs, axis=1)
        o_ref[rows, :] = (o * jax.nn.silu(ag_ref[rows, :])).astype(BF16)


def _attention(pf, pb, sink_tab, ctx_len, tile_off):
    bsz, t, _ = pf.shape
    qb = ATT_TILE
    nct = ctx_len // qb
    hb = qb // WINDOW
    ctx_hb = ctx_len // WINDOW
    n_hb = t // WINDOW
    ntiles = t // qb - tile_off
    kvw = 2 * ATT_KV_WIDTH
    kv_col = PB_KV // kvw
    in_specs = [
        pl.BlockSpec((None, qb, ATT_WIDTH), lambda b, i: (b, i + tile_off, PB_Q // ATT_WIDTH)),
        pl.BlockSpec((None, qb, kvw), lambda b, i: (b, i + tile_off, kv_col)),
        pl.BlockSpec((None, WINDOW, kvw), lambda b, i: (b, jnp.maximum((i + tile_off) * hb - 1, ctx_hb), kv_col)),
        pl.BlockSpec((None, WINDOW, kvw), lambda b, i: (b, jnp.minimum((i + tile_off) * hb + hb, n_hb - 1), kv_col)),
        pl.BlockSpec((None, ctx_len, kvw), lambda b, i: (b, 0, kv_col)),
        pl.BlockSpec((None, qb, ATT_WIDTH), lambda b, i: (b, i + tile_off, PF_AG // ATT_WIDTH)),
        pl.BlockSpec((ATT_HEADS, LANES), lambda b, i: (0, 0)),
    ]
    return pl.pallas_call(
        functools.partial(_attn_kernel, tile_off=tile_off, nct=nct, seq=t - ctx_len, ctx_len=ctx_len),
        grid=(bsz, ntiles),
        in_specs=in_specs,
        out_specs=pl.BlockSpec((None, qb, ATT_WIDTH), lambda b, i: (b, i + tile_off, 0)),
        out_shape=jax.ShapeDtypeStruct((bsz, t, ATT_WIDTH), BF16),
        scratch_shapes=[pltpu.VMEM((ATT_KV_HEADS, 4, ctx_len + qb + 2 * WINDOW, LANES), BF16)],
        compiler_params=_cparams("parallel", "arbitrary"),
        name="attention",
    )(pb, pb, pb, pb, pb, pf, sink_tab)


def _conf_kernel(gm_ref, gp_ref, gn_ref, cg_ref, dww_ref, dwb_ref, lnw_ref, lnb_ref, pww_ref, pwb_ref, o_ref,
                 hbuf, abuf, *, tile_off, nct, nt):
    tc = CONF_TILE
    tile = pl.program_id(1) + tile_off
    first = (tile == 0) | (tile == nct)
    last = (tile == nct - 1) | (tile == nt - 1)

    def glu(v):
        return v[:, :CONV_WIDTH] * jax.nn.sigmoid(v[:, CONV_WIDTH:])

    rows = tc + 2 * CONF_HALO
    ext = jnp.concatenate([jnp.where(first, 0.0, glu(gp_ref[...])), glu(gm_ref[...]),
                           jnp.where(last, 0.0, glu(gn_ref[...]))], axis=0)
    hbuf[0] = ext
    for j in range(1, SUBLANES):
        hbuf[j, 0:rows - SUBLANES, :] = _shift_rows(ext, j)
    base = CONF_HALO - CONV_PAD
    for r in range(tc // CONF_SUB):
        acc = jnp.broadcast_to(dwb_ref[...], (CONF_SUB // SUBLANES, SUBLANES, CONV_WIDTH))
        for k in range(CONV_KERNEL):
            blk, res = divmod(base + k, SUBLANES)
            strip = hbuf[res, pl.ds(r * CONF_SUB + blk * SUBLANES, CONF_SUB), :]
            acc = acc + dww_ref[k] * strip.reshape(CONF_SUB // SUBLANES, SUBLANES, CONV_WIDTH)
        acc = acc.reshape(CONF_SUB, CONV_WIDTH)
        xc = acc - jnp.mean(acc, axis=-1, keepdims=True)
        var = jnp.mean(xc * xc, axis=-1, keepdims=True)
        ln = xc * lax.rsqrt(var + EPS) * lnw_ref[...] + lnb_ref[...]
        abuf[pl.ds(r * CONF_SUB, CONF_SUB), :] = jax.nn.silu(ln)
    out = _dot(abuf[...].astype(BF16), pww_ref[...]) + pwb_ref[...]
    o_ref[...] = (out * jax.nn.silu(cg_ref[...])).astype(BF16)


def _conformer(pf, dw_w, dw_b, ln_w, ln_b, pw_w16, layer, pw_b, ctx_len, tile_off):
    bsz, t, _ = pf.shape
    tc = CONF_TILE
    nt = t // tc
    nct = ctx_len // tc
    hb = tc // CONF_HALO
    nhalo = t // CONF_HALO
    main = lambda b, i: (b, i + tile_off, 0)
    const2 = lambda b, i: (0, 0)
    glu_col = PF_GLU // (2 * CONV_WIDTH)
    in_specs = [
        pl.BlockSpec((None, tc, 2 * CONV_WIDTH), lambda b, i: (b, i + tile_off, glu_col)),
        pl.BlockSpec((None, CONF_HALO, 2 * CONV_WIDTH),
                     lambda b, i: (b, jnp.maximum((i + tile_off) * hb - 1, 0), glu_col)),
        pl.BlockSpec((None, CONF_HALO, 2 * CONV_WIDTH),
                     lambda b, i: (b, jnp.minimum((i + tile_off) * hb + hb, nhalo - 1), glu_col)),
        pl.BlockSpec((None, tc, CONV_WIDTH), lambda b, i: (b, i + tile_off, PF_CG // CONV_WIDTH)),
        pl.BlockSpec((CONV_KERNEL, SUBLANES, CONV_WIDTH), lambda b, i: (0, 0, 0)),
        pl.BlockSpec((SUBLANES, CONV_WIDTH), const2),
        pl.BlockSpec((1, CONV_WIDTH), const2),
        pl.BlockSpec((1, CONV_WIDTH), const2),
        pl.BlockSpec((None, CONV_WIDTH, CONV_WIDTH), lambda b, i: (layer, 0, 0)),
        pl.BlockSpec((1, CONV_WIDTH), const2),
    ]
    return pl.pallas_call(
        functools.partial(_conf_kernel, tile_off=tile_off, nct=nct, nt=nt),
        grid=(bsz, nt - tile_off),
        in_specs=in_specs,
        out_specs=pl.BlockSpec((None, tc, CONV_WIDTH), main),
        out_shape=jax.ShapeDtypeStruct((bsz, t, CONV_WIDTH), BF16),
        scratch_shapes=[
            pltpu.VMEM((SUBLANES, tc + 2 * CONF_HALO, CONV_WIDTH), F32),
            pltpu.VMEM((tc, CONV_WIDTH), F32),
        ],
        compiler_params=_cparams("parallel", "arbitrary"),
        name="conformer",
    )(pf, pf, pf, pf, dw_w, dw_b, ln_w, ln_b, pw_w16, pw_b)


def _outproj_kernel(c_ref, x_ref, os_ref, oa_ref, oc_ref, mod_ref, w_ref, fnw_ref, o_ref,
                    *, final, nct, tile_off):
    a0 = SSD_WIDTH
    c0 = SSD_WIDTH + ATT_WIDTH
    u = (_dot(os_ref[...], w_ref[0:a0, :]) + _dot(oa_ref[...], w_ref[a0:c0, :])
         + _dot(oc_ref[...], w_ref[c0:c0 + CONV_WIDTH, :]))
    x = x_ref[...]
    if tile_off < nct:
        x = jnp.where(pl.program_id(1) + tile_off < nct, c_ref[...], x)
    xn = x + mod_ref[2:3, :] * u
    if final:
        xn = xn * lax.rsqrt(jnp.mean(xn * xn, axis=-1, keepdims=True) + EPS) * fnw_ref[...]
    o_ref[...] = xn


def _outproj(csrc, xsrc, lat_off, o_ssd, o_att, o_conv, modsel, w_out16, layer, final_norm_w, nct, final):
    bsz, t, _ = o_ssd.shape
    d = xsrc.shape[-1]
    tm = ROW_TILE
    tile_off = nct if final else 0
    main = lambda b, i: (b, i + tile_off, 0)
    in_specs = _token_specs(tm, d, nct, lat_off, tile_off) + [
        pl.BlockSpec((None, tm, SSD_WIDTH), main),
        pl.BlockSpec((None, tm, ATT_WIDTH), main),
        pl.BlockSpec((None, tm, CONV_WIDTH), main),
        pl.BlockSpec((None, None, 3, d), lambda b, i: (b, jnp.where(i + tile_off >= nct, 1, 0), 0, 0)),
        pl.BlockSpec((None,) + w_out16.shape[1:], lambda b, i: (layer, 0, 0)),
        pl.BlockSpec((1, d), lambda b, i: (0, 0)),
    ]
    rows_out = t - tile_off * tm
    return pl.pallas_call(
        functools.partial(_outproj_kernel, final=final, nct=nct, tile_off=tile_off),
        grid=(bsz, rows_out // tm),
        in_specs=in_specs,
        out_specs=pl.BlockSpec((None, tm, d), lambda b, i: (b, i, 0)),
        out_shape=jax.ShapeDtypeStruct((bsz, rows_out, d), F32),
        compiler_params=_cparams("parallel", "arbitrary"),
        name="outproj_final" if final else "outproj",
    )(csrc, xsrc, o_ssd, o_att, o_conv, modsel, w_out16, final_norm_w)


def _rope_tables(seq, ctx_len):
    rows = seq // GRID_W
    row = jnp.repeat(jnp.arange(rows, dtype=F32), GRID_W)
    col = jnp.tile(jnp.arange(GRID_W, dtype=F32), rows)
    axis_dim = ATT_HEAD_DIM // 2
    inv_freq = ROPE_BASE ** (-jnp.arange(0, axis_dim, 2, dtype=F32) / axis_dim)
    ar = row[:, None] * inv_freq
    ac = col[:, None] * inv_freq
    cos = jnp.concatenate([jnp.cos(ar), jnp.cos(ar), jnp.cos(ac), jnp.cos(ac)], axis=1)
    sin = jnp.concatenate([-jnp.sin(ar), jnp.sin(ar), -jnp.sin(ac), jnp.sin(ac)], axis=1)
    reps = LANES // ATT_HEAD_DIM
    cos = jnp.concatenate([jnp.ones((ctx_len, LANES), F32), jnp.tile(cos, (1, reps))], axis=0)
    sin = jnp.concatenate([jnp.zeros((ctx_len, LANES), F32), jnp.tile(sin, (1, reps))], axis=0)
    return jnp.concatenate([cos, sin], axis=1)


def _head_lanes(a):
    rep = jnp.concatenate([a, a, a], axis=-1)
    return jnp.pad(rep, [(0, 0)] * (a.ndim - 1) + [(0, LANES - 3 * SSD_HEADS)])


def _pack_w_in(w_in):
    dt0 = SSD_XBC + SSD_WIDTH
    dt_w = w_in[..., dt0:dt0 + 2 * SSD_HEADS]
    packed = jnp.concatenate([w_in[..., :dt0], w_in[..., dt0 + 2 * SSD_HEADS:],
                              _head_lanes(dt_w[..., :SSD_HEADS]), _head_lanes(dt_w[..., SSD_HEADS:])], axis=-1)
    assert packed.shape[-1] == W_WIDTH
    return packed.astype(BF16)


def kernel(x, c, ctx, c_ctx, w_mod, b_mod, norm_w, w_in, ssd_conv_w, ssd_conv_b, ssd_dt_bias, ssd_a_log, ssd_d,
           ssd_norm_w, attn_sink, conv_dw_w, conv_dw_b, conv_ln_w, conv_ln_b, conv_pw_w, conv_pw_b, w_out,
           final_norm_w):
    bsz, seq, d = x.shape
    ctx_len = ctx.shape[1]
    depth = w_in.shape[0]
    assert ctx_len % ROW_TILE == 0 and seq % ROW_TILE == 0 and seq % GRID_W == 0
    assert ROW_TILE == ATT_TILE == CONF_TILE
    nct = ctx_len // ROW_TILE

    cond_rows = -(-(bsz + 1) // SUBLANES) * SUBLANES
    cond = jnp.zeros((cond_rows, d), F32).at[:bsz].set(c).at[bsz].set(c_ctx)
    mod = _modulation(cond, w_mod, b_mod)
    rope_t = _rope_tables(seq, ctx_len)
    csrc, xsrc, lat_off = ctx, x, 0
    w_in16 = _pack_w_in(w_in)
    w_out16 = w_out.astype(BF16)
    pw_w16 = conv_pw_w.astype(BF16)
    norm_w3 = norm_w[:, None, :]

    out = None
    for l in range(depth):
        final = l == depth - 1
        lat = mod[l, :bsz]
        ctxm = jnp.broadcast_to(mod[l, bsz], lat.shape)
        modsel = jnp.stack([ctxm, lat], axis=1).reshape(bsz, 2, 3, d)
        xbc, pf, pb = _inproj(csrc, xsrc, lat_off, ctx_len + seq, modsel, norm_w3, rope_t, w_in16, l, nct)

        o_ssd = _ssd(xbc, pf, ssd_conv_w[l], ssd_conv_b[l][None], _head_lanes(ssd_dt_bias[l])[:, None],
                     _head_lanes(ssd_a_log[l])[:, None], jnp.repeat(ssd_d[l], SSD_HEAD_DIM)[None],
                     ssd_norm_w[l][None], ctx_len)
        tile_off = nct if final else 0
        sink_tab = jnp.broadcast_to(attn_sink[l][:, None] * LOG2E, (ATT_HEADS, LANES))
        o_att = _attention(pf, pb, sink_tab, ctx_len, tile_off)
        dw_w = jnp.broadcast_to(conv_dw_w[l][:, None, :], (CONV_KERNEL, SUBLANES, CONV_WIDTH))
        dw_b = jnp.broadcast_to(conv_dw_b[l][None], (SUBLANES, CONV_WIDTH))
        o_conv = _conformer(pf, dw_w, dw_b, conv_ln_w[l][None], conv_ln_b[l][None], pw_w16, l, conv_pw_b[l][None],
                            ctx_len, tile_off)
        out = _outproj(csrc, xsrc, lat_off, o_ssd, o_att, o_conv, modsel, w_out16, l, final_norm_w[None], nct, final)
        csrc, xsrc, lat_off = out, out, nct
    return out
```

```python
import functools

import jax
import jax.numpy as jnp
from jax import lax
from jax.experimental import pallas as pl
from jax.experimental.pallas import tpu as pltpu

F32 = jnp.float32
BF16 = jnp.bfloat16

GRID_W = 64
EPS = 1e-6

SSD_HEADS = 16
SSD_HEAD_DIM = 64
SSD_WIDTH = SSD_HEADS * SSD_HEAD_DIM
SSD_GROUPS = 2
SSD_STATE = 128
SSD_CONV = 5
SSD_CHUNK = 128
SSD_GN = SSD_GROUPS * SSD_STATE
SSD_XBC = SSD_WIDTH + 2 * SSD_GN
GROUP_COLS = SSD_WIDTH // SSD_GROUPS

ATT_HEADS = 8
ATT_KV_HEADS = 2
ATT_HEAD_DIM = 64
ATT_WIDTH = ATT_HEADS * ATT_HEAD_DIM
ATT_KV_WIDTH = ATT_KV_HEADS * ATT_HEAD_DIM
WINDOW = 128
ROPE_BASE = 10000.0
LOG2E = 1.4426950408889634
ATT_SCALE = ATT_HEAD_DIM ** -0.5 * LOG2E

CONV_WIDTH = 512
CONV_KERNEL = 31
CONV_PAD = (CONV_KERNEL - 1) // 2

LANES = 128

PF_Z = 0
PF_GLU = PF_Z + SSD_WIDTH
PF_AG = PF_GLU + 2 * CONV_WIDTH
PF_CG = PF_AG + ATT_WIDTH
PF_DT = PF_CG + CONV_WIDTH
PF_WIDTH = PF_DT + 2 * LANES
PB_Q = 0
PB_KV = PB_Q + ATT_WIDTH
PB_WIDTH = PB_KV + 2 * ATT_KV_WIDTH

W_XBC = 0
W_Z = W_XBC + SSD_XBC
W_Q = W_Z + SSD_WIDTH
W_K = W_Q + ATT_WIDTH
W_V = W_K + ATT_KV_WIDTH
W_AG = W_V + ATT_KV_WIDTH
W_GLU = W_AG + ATT_WIDTH
W_CG = W_GLU + 2 * CONV_WIDTH
W_DT = W_CG + CONV_WIDTH
W_WIDTH = W_DT + 2 * LANES
SUBLANES = 8
VMEM_LIMIT = 56 * 1024 * 1024

ROW_TILE = 256
ATT_TILE = 256
ATT_QBLOCK = 256
CONF_TILE = 256
CONF_HALO = 16
CONF_SUB = 32
SSD_HALO = SUBLANES
SSD_STEP = 2 * SSD_CHUNK


def _dot(a, b):
    return jnp.dot(a, b, preferred_element_type=F32)


def _dot_nt(a, b):
    return lax.dot_general(a, b, (((1,), (1,)), ((), ())), preferred_element_type=F32)


def _split3(x):
    hi = x.astype(BF16)
    r1 = x - hi.astype(F32)
    mid = r1.astype(BF16)
    lo = (r1 - mid.astype(F32)).astype(BF16)
    return hi, mid, lo


def _pack3(x, width):
    hi, mid, lo = _split3(x)
    lane = lax.broadcasted_iota(jnp.int32, x.shape, 1)
    return jnp.where(lane < width, hi, jnp.where(lane < 2 * width, mid, lo))


def _shift_rows(x, j):
    rows, w = x.shape
    g = x.reshape(rows // SUBLANES, SUBLANES, w)
    r = pltpu.roll(g, SUBLANES - j, 1)
    sub = lax.broadcasted_iota(jnp.int32, (1, SUBLANES, w), 1)
    return jnp.where(sub < SUBLANES - j, r[:-1], r[1:]).reshape(rows - SUBLANES, w)


def _cparams(*sem, flags=None):
    return pltpu.CompilerParams(dimension_semantics=sem, vmem_limit_bytes=VMEM_LIMIT, flags=flags)


def _mod_kernel(c_ref, w_ref, b_ref, o_ref):
    a = jax.nn.silu(c_ref[...]).astype(BF16)
    o_ref[...] = _dot(a, w_ref[...].astype(BF16)) + b_ref[...]


def _modulation(cond, w_mod, b_mod):
    depth, d, n = w_mod.shape
    rows = cond.shape[0]
    tn = n // 2 if (n // 2) % LANES == 0 else n
    return pl.pallas_call(
        _mod_kernel,
        grid=(depth, n // tn),
        in_specs=[
            pl.BlockSpec((rows, d), lambda l, j: (0, 0)),
            pl.BlockSpec((None, d, tn), lambda l, j: (l, 0, j)),
            pl.BlockSpec((None, 1, tn), lambda l, j: (l, 0, j)),
        ],
        out_specs=pl.BlockSpec((None, rows, tn), lambda l, j: (l, 0, j)),
        out_shape=jax.ShapeDtypeStruct((depth, rows, n), F32),
        compiler_params=_cparams("arbitrary", "arbitrary"),
        name="modulation",
    )(cond, w_mod, b_mod.reshape(depth, 1, n))


def _rope(t, cos, sin):
    width = t.shape[-1]
    lane = lax.broadcasted_iota(jnp.int32, t.shape, 1)
    first = (lane % 32) < 16
    rot = jnp.where(first, pltpu.roll(t, width - 16, 1), pltpu.roll(t, 16, 1))
    return t * cos + rot * sin


def _inproj_kernel(c_ref, x_ref, mod_ref, nw_ref, rope_ref, w_ref, xbc_ref, pf_ref, pb_ref, *, nct):
    wxbc = w_ref.at[:, W_XBC:W_XBC + SSD_XBC]
    wz = w_ref.at[:, W_Z:W_Z + SSD_WIDTH]
    wq = w_ref.at[:, W_Q:W_Q + ATT_WIDTH]
    wk = w_ref.at[:, W_K:W_K + ATT_KV_WIDTH]
    wv = w_ref.at[:, W_V:W_V + ATT_KV_WIDTH]
    wag = w_ref.at[:, W_AG:W_AG + ATT_WIDTH]
    wglu = w_ref.at[:, W_GLU:W_GLU + 2 * CONV_WIDTH]
    wcg = w_ref.at[:, W_CG:W_CG + CONV_WIDTH]
    wdt = w_ref.at[:, W_DT:W_DT + 2 * LANES]
    x = jnp.where(pl.program_id(1) < nct, c_ref[...], x_ref[...])
    y = x * lax.rsqrt(jnp.mean(x * x, axis=-1, keepdims=True) + EPS) * nw_ref[...]
    h = (y * (1.0 + mod_ref[1:2, :]) + mod_ref[0:1, :]).astype(BF16)
    xbc_ref[...] = _dot(h, wxbc[...])
    pf_ref[:, PF_Z:PF_Z + SSD_WIDTH] = _dot(h, wz[...])
    pf_ref[:, PF_DT:PF_DT + 2 * LANES] = _dot(h, wdt[...])
    pf_ref[:, PF_AG:PF_AG + ATT_WIDTH] = _dot(h, wag[...])
    pf_ref[:, PF_GLU:PF_GLU + 2 * CONV_WIDTH] = _dot(h, wglu[...])
    pf_ref[:, PF_CG:PF_CG + CONV_WIDTH] = _dot(h, wcg[...])
    pb_ref[:, PB_KV + ATT_KV_WIDTH:PB_KV + 2 * ATT_KV_WIDTH] = _dot(h, wv[...]).astype(BF16)
    cos = rope_ref[:, :LANES]
    sin = rope_ref[:, LANES:]
    pb_ref[:, PB_KV:PB_KV + ATT_KV_WIDTH] = _rope(_dot(h, wk[...]), cos, sin).astype(BF16)
    reps = ATT_WIDTH // LANES
    cos_q = jnp.concatenate([cos] * reps, axis=1)
    sin_q = jnp.concatenate([sin] * reps, axis=1)
    pb_ref[:, PB_Q:PB_Q + ATT_WIDTH] = (_rope(_dot(h, wq[...]), cos_q, sin_q) * ATT_SCALE).astype(BF16)


def _token_specs(tm, d, nct, lat_off, tile_off):
    return [
        pl.BlockSpec((None, tm, d), lambda b, i: (b, jnp.minimum(i + tile_off, nct - 1), 0)),
        pl.BlockSpec((None, tm, d), lambda b, i: (b, jnp.maximum(i + tile_off - nct, 0) + lat_off, 0)),
    ]


def _inproj(csrc, xsrc, lat_off, t, modsel, norm_w, rope_t, w_packed, layer, nct):
    bsz, _, d = xsrc.shape
    tm = ROW_TILE
    row_map = lambda b, i: (b, i, 0)
    in_specs = _token_specs(tm, d, nct, lat_off, 0) + [
        pl.BlockSpec((None, None, 3, d), lambda b, i: (b, jnp.where(i >= nct, 1, 0), 0, 0)),
        pl.BlockSpec((None, 1, d), lambda b, i: (layer, 0, 0)),
        pl.BlockSpec((tm, 2 * LANES), lambda b, i: (i, 0)),
        pl.BlockSpec((None, d, W_WIDTH), lambda b, i: (layer, 0, 0)),
    ]
    widths = (SSD_XBC, PF_WIDTH, PB_WIDTH)
    return pl.pallas_call(
        functools.partial(_inproj_kernel, nct=nct),
        grid=(bsz, t // tm),
        in_specs=in_specs,
        out_specs=[pl.BlockSpec((None, tm, w), row_map) for w in widths],
        out_shape=[jax.ShapeDtypeStruct((bsz, t, w), dt) for w, dt in zip(widths, (F32, F32, BF16))],
        compiler_params=_cparams("parallel", "arbitrary"),
        name="inproj",
    )(csrc, xsrc, modsel, norm_w, rope_t, w_packed)


def _ssd_chunk(xs, bc, dt_raw, dtb, alog, fwd, st_ref):
    q = SSD_CHUNK
    bb = bc[:, :SSD_GN]
    cb16 = bc[:, SSD_GN:]

    lane = lax.broadcasted_iota(jnp.int32, (q, LANES), 1)
    head_lane = lane < 3 * SSD_HEADS
    dt = jnp.where(head_lane, jax.nn.softplus(dt_raw + dtb), 0.0)
    da = dt * (-jnp.exp(alog))
    li = lax.broadcasted_iota(jnp.int32, (q, q), 0)
    si = lax.broadcasted_iota(jnp.int32, (q, q), 1)
    causal = (si <= li) if fwd else (si >= li)
    ones = jnp.where(causal, 1.0, 0.0).astype(BF16)
    da_hi, da_mid, da_lo = _split3(da)
    cs = _dot(ones, da_hi) + _dot(ones, da_mid) + _dot(ones, da_lo)
    total = jnp.sum(da, axis=0, keepdims=True)
    to_end = jnp.exp(total - cs)
    from_start = jnp.where(head_lane, jnp.exp(cs), 0.0)
    chunk_decay = jnp.where(head_lane[:SUBLANES], jnp.exp(total), 0.0)
    cs_t = cs.T

    er = lax.broadcasted_iota(jnp.int32, (LANES, SSD_WIDTH), 0)
    ec = lax.broadcasted_iota(jnp.int32, (LANES, SSD_WIDTH), 1)
    spread = jnp.where((ec // SSD_HEAD_DIM == er % SSD_HEADS) & (er < 3 * SSD_HEADS), 1.0, 0.0).astype(BF16)
    dt_e = _dot(_pack3(dt, SSD_HEADS), spread)
    stacked = jnp.concatenate([dt * to_end, from_start, chunk_decay], axis=0)
    stacked_e = _dot(_pack3(stacked, SSD_HEADS), spread)
    w_e = stacked_e[:q]
    fs_e = stacked_e[q:2 * q]
    cd_e = stacked_e[2 * q:2 * q + 1]

    xd = (xs * dt_e).astype(BF16)
    xdw = (xs * w_e).astype(BF16)
    low = lax.broadcasted_iota(jnp.int32, (q, LANES), 1) < SSD_HEAD_DIM
    zero16 = jnp.zeros((q, LANES), BF16)
    ys = []
    for g in range(SSD_GROUPS):
        gs = slice(g * SSD_STATE, (g + 1) * SSD_STATE)
        gc = slice(g * GROUP_COLS, (g + 1) * GROUP_COLS)
        scores = _dot_nt(cb16[:, gs], bb[:, gs])
        entering = st_ref[g]
        y_off = _dot(cb16[:, gs], entering.astype(BF16))
        new_state = _dot(bb[:, gs].astype(F32).T.astype(BF16), xdw[:, gc])
        st_ref[g] = entering * cd_e[:, gc] + new_state
        for pr in range(GROUP_COLS // LANES):
            h0 = (g * GROUP_COLS + pr * LANES) // SSD_HEAD_DIM
            cols = slice(h0 * SSD_HEAD_DIM, h0 * SSD_HEAD_DIM + LANES)
            xpair = xd[:, cols]
            m0 = (scores * jnp.exp(jnp.where(causal, cs[:, h0:h0 + 1] - cs_t[h0:h0 + 1, :], -jnp.inf))).astype(BF16)
            m1 = (scores * jnp.exp(jnp.where(causal, cs[:, h0 + 1:h0 + 2] - cs_t[h0 + 1:h0 + 2, :], -jnp.inf))).astype(BF16)
            block_diag = jnp.concatenate([jnp.where(low, xpair, zero16), jnp.where(low, zero16, xpair)], axis=0)
            y_diag = _dot(jnp.concatenate([m0, m1], axis=1), block_diag)
            ys.append(y_diag + y_off[:, pr * LANES:(pr + 1) * LANES] * fs_e[:, cols])
    return jnp.concatenate(ys, axis=1)


def _ssd_bwd_block(k, ncb, nb):
    return jnp.where(k < ncb, ncb - 1 - k, nb + ncb - 1 - k)


def _ssd_kernel(xm_ref, xp_ref, xn_ref, dt_ref, z_ref, cw_ref, cb_ref, dtb_ref, alog_ref, dvec_ref, nw_ref,
                o_ref, yb_ref, st_ref, xs_ref, bc_ref, *, ncb, nb):
    q = SSD_CHUNK
    rows = SSD_STEP
    n_sub = rows // q
    s = pl.program_id(1)

    def conv_block(cb):
        first = (cb == 0) | (cb == ncb)
        last = (cb == ncb - 1) | (cb == nb - 1)
        xext = jnp.concatenate([jnp.where(first, 0.0, xp_ref[...]), xm_ref[...],
                                jnp.where(last, 0.0, xn_ref[...])], axis=0)
        acc = cb_ref[...]
        for k in range(SSD_CONV):
            d = k - (SSD_CONV - 1) // 2
            if d == 0:
                tap = xext[SSD_HALO:SSD_HALO + rows]
            elif d > 0:
                tap = _shift_rows(xext, d)[SSD_HALO:SSD_HALO + rows]
            else:
                tap = _shift_rows(xext, SSD_HALO + d)[:rows]
            acc = acc + cw_ref[k:k + 1, :] * tap
        u = jax.nn.silu(acc)
        dst = pl.ds(pl.multiple_of(cb * rows, rows), rows)
        xs_ref[dst, :] = u[:, :SSD_WIDTH]
        bc_ref[dst, :] = u[:, SSD_WIDTH:].astype(BF16)

    def chunk_rows(blk, j):
        return pl.ds(pl.multiple_of(blk * rows + j * q, q), q)

    @pl.when((s == 0) | (s == nb))
    def _():
        st_ref[...] = jnp.zeros_like(st_ref)

    @pl.when(s <= nb - 1)
    def _():
        conv_block(_ssd_bwd_block(s, ncb, nb))

    @pl.when(s < nb)
    def _():
        blk = _ssd_bwd_block(s, ncb, nb)
        for j in reversed(range(n_sub)):
            y = _ssd_chunk(xs_ref[chunk_rows(blk, j), :], bc_ref[chunk_rows(blk, j), :], dt_ref[j * q:(j + 1) * q, :],
                           dtb_ref[...], alog_ref[...], False, st_ref)
            yb_ref[chunk_rows(blk, j), :] = y

    @pl.when(s >= nb)
    def _():
        blk = s - nb
        for j in range(n_sub):
            xs = xs_ref[chunk_rows(blk, j), :]
            y = _ssd_chunk(xs, bc_ref[chunk_rows(blk, j), :], dt_ref[j * q:(j + 1) * q, :], dtb_ref[...],
                           alog_ref[...], True, st_ref)
            tot = y + yb_ref[chunk_rows(blk, j), :] + dvec_ref[...] * xs
            gated = tot * jax.nn.silu(z_ref[j * q:(j + 1) * q, :])
            outs = []
            for g in range(SSD_GROUPS):
                gc = slice(g * GROUP_COLS, (g + 1) * GROUP_COLS)
                v = gated[:, gc]
                outs.append(v * lax.rsqrt(jnp.mean(v * v, axis=-1, keepdims=True) + EPS) * nw_ref[:, gc])
            o_ref[j * q:(j + 1) * q, :] = jnp.concatenate(outs, axis=1).astype(BF16)


def _ssd(xbc, pf, conv_w, conv_b, dt_bias, a_log, dvec, norm_w, ctx_len):
    bsz, t, _ = pf.shape
    rows = SSD_STEP
    nb = t // rows
    ncb = ctx_len // rows
    hb = rows // SSD_HALO
    nhalo = t // SSD_HALO
    const2 = lambda b, s: (0, 0)
    xblk = lambda s: _ssd_bwd_block(jnp.minimum(s, nb - 1), ncb, nb)
    cur = lambda s: jnp.where(s < nb, _ssd_bwd_block(s, ncb, nb), s - nb)
    phase = lambda s: jnp.where(s < nb, 1, 0)
    fwd_blk = lambda s: jnp.maximum(s - nb, 0)
    in_specs = [
        pl.BlockSpec((None, rows, SSD_XBC), lambda b, s: (b, xblk(s), 0)),
        pl.BlockSpec((None, SSD_HALO, SSD_XBC), lambda b, s: (b, jnp.maximum(xblk(s) * hb - 1, 0), 0)),
        pl.BlockSpec((None, SSD_HALO, SSD_XBC), lambda b, s: (b, jnp.minimum(xblk(s) * hb + hb, nhalo - 1), 0)),
        pl.BlockSpec((None, rows, LANES), lambda b, s: (b, cur(s), PF_DT // LANES + phase(s))),
        pl.BlockSpec((None, rows, SSD_WIDTH), lambda b, s: (b, fwd_blk(s), PF_Z // SSD_WIDTH)),
        pl.BlockSpec((SSD_CONV, SSD_XBC), const2),
        pl.BlockSpec((1, SSD_XBC), const2),
        pl.BlockSpec((None, 1, LANES), lambda b, s: (phase(s), 0, 0)),
        pl.BlockSpec((None, 1, LANES), lambda b, s: (phase(s), 0, 0)),
        pl.BlockSpec((1, SSD_WIDTH), const2),
        pl.BlockSpec((1, SSD_WIDTH), const2),
    ]
    return pl.pallas_call(
        functools.partial(_ssd_kernel, ncb=ncb, nb=nb),
        grid=(bsz, 2 * nb),
        in_specs=in_specs,
        out_specs=pl.BlockSpec((None, rows, SSD_WIDTH), lambda b, s: (b, fwd_blk(s), 0)),
        out_shape=jax.ShapeDtypeStruct((bsz, t, SSD_WIDTH), BF16),
        scratch_shapes=[
            pltpu.VMEM((t, SSD_WIDTH), F32),
            pltpu.VMEM((SSD_GROUPS, SSD_STATE, GROUP_COLS), F32),
            pltpu.VMEM((t, SSD_WIDTH), F32),
            pltpu.VMEM((t, 2 * SSD_GN), BF16),
        ],
        compiler_params=_cparams("parallel", "arbitrary"),
        name="ssd",
    )(xbc, xbc, xbc, pf, pf, conv_w, conv_b, dt_bias, a_log, dvec, norm_w)


def _swap_halves(blk):
    half = blk.shape[-1] // 2
    return jnp.concatenate([blk[:, half:], blk[:, :half]], axis=1)


def _head_pads(blk, g, low):
    zero = jnp.zeros_like(blk)
    sw = _swap_halves(blk)
    if g == 0:
        return jnp.where(low, blk, zero), jnp.where(low, zero, sw)
    return jnp.where(low, sw, zero), jnp.where(low, zero, blk)


def _attn_kernel(q_ref, kvm_ref, kvp_ref, kvn_ref, kvc_ref, ag_ref, sink_ref, o_ref, pad_ref,
                 *, tile_off, nct, seq, ctx_len):
    qb = ATT_TILE
    blk = ATT_QBLOCK
    wk = blk + 2 * WINDOW
    tile = pl.program_id(1) + tile_off
    lat0 = (tile - nct) * qb
    low_k = lax.broadcasted_iota(jnp.int32, (1, LANES), 1) < ATT_HEAD_DIM

    kv = jnp.concatenate([kvc_ref[...], kvp_ref[...], kvm_ref[...], kvn_ref[...]], axis=0)
    for g in range(ATT_KV_HEADS):
        ka, kb = _head_pads(kv[:, :ATT_KV_WIDTH], g, low_k)
        va, vb = _head_pads(kv[:, ATT_KV_WIDTH:], g, low_k)
        pad_ref[g, 0] = ka
        pad_ref[g, 1] = kb
        pad_ref[g, 2] = va
        pad_ref[g, 3] = vb

    r = lax.broadcasted_iota(jnp.int32, (blk, wk), 0)
    col = lax.broadcasted_iota(jnp.int32, (blk, wk), 1)
    in_band = jnp.abs(col - WINDOW - r) <= WINDOW
    neg = jnp.finfo(F32).min
    rep = ATT_HEADS // ATT_KV_HEADS
    pairs_per_group = rep * ATT_HEAD_DIM // LANES
    always = jnp.ones((blk, ctx_len), jnp.bool_)
    for a in range(qb // blk):
        rows = slice(a * blk, (a + 1) * blk)
        kpos = lat0 + a * blk - WINDOW + col
        valid = jnp.concatenate([always, in_band & (kpos >= 0) & (kpos < seq) & (lat0 >= 0)], axis=1)
        win0 = ctx_len + a * blk
        def operand(g, slot):
            if a == 0:
                return pad_ref[g, slot, 0:win0 + wk, :]
            return jnp.concatenate([pad_ref[g, slot, 0:ctx_len, :], pad_ref[g, slot, win0:win0 + wk, :]], axis=0)

        outs = []
        for g in range(ATT_KV_HEADS):
            for pr in range(pairs_per_group):
                p = g * pairs_per_group + pr
                qp = q_ref[rows, p * LANES:(p + 1) * LANES]
                acc = jnp.zeros((blk, LANES), F32)
                invs = []
                for j in range(2):
                    head = 2 * p + j
                    sink = sink_ref[head:head + 1, 0:1]
                    s = jnp.where(valid, _dot_nt(qp, operand(g, j)), neg)
                    m = jnp.maximum(jnp.max(s, axis=-1, keepdims=True), sink)
                    pexp = jnp.exp2(s - m)
                    den = jnp.sum(pexp, axis=-1, keepdims=True) + jnp.exp2(sink - m)
                    acc = acc + _dot(pexp.astype(BF16), operand(g, 2 + j))
                    invs.append(1.0 / den)
                outs.append(acc * jnp.where(low_k, invs[0], invs[1]))
        o = jnp.concatenate(outs, axis=1)
        o_ref[rows, :] = (o * jax.nn.silu(ag_ref[rows, :])).astype(BF16)


def _attention(pf, pb, sink_tab, ctx_len, tile_off):
    bsz, t, _ = pf.shape
    qb = ATT_TILE
    nct = ctx_len // qb
    hb = qb // WINDOW
    ctx_hb = ctx_len // WINDOW
    n_hb = t // WINDOW
    ntiles = t // qb - tile_off
    kvw = 2 * ATT_KV_WIDTH
    kv_col = PB_KV // kvw
    in_specs = [
        pl.BlockSpec((None, qb, ATT_WIDTH), lambda b, i: (b, i + tile_off, PB_Q // ATT_WIDTH)),
        pl.BlockSpec((None, qb, kvw), lambda b, i: (b, i + tile_off, kv_col)),
        pl.BlockSpec((None, WINDOW, kvw), lambda b, i: (b, jnp.maximum((i + tile_off) * hb - 1, ctx_hb), kv_col)),
        pl.BlockSpec((None, WINDOW, kvw), lambda b, i: (b, jnp.minimum((i + tile_off) * hb + hb, n_hb - 1), kv_col)),
        pl.BlockSpec((None, ctx_len, kvw), lambda b, i: (b, 0, kv_col)),
        pl.BlockSpec((None, qb, ATT_WIDTH), lambda b, i: (b, i + tile_off, PF_AG // ATT_WIDTH)),
        pl.BlockSpec((ATT_HEADS, LANES), lambda b, i: (0, 0)),
    ]
    return pl.pallas_call(
        functools.partial(_attn_kernel, tile_off=tile_off, nct=nct, seq=t - ctx_len, ctx_len=ctx_len),
        grid=(bsz, ntiles),
        in_specs=in_specs,
        out_specs=pl.BlockSpec((None, qb, ATT_WIDTH), lambda b, i: (b, i + tile_off, 0)),
        out_shape=jax.ShapeDtypeStruct((bsz, t, ATT_WIDTH), BF16),
        scratch_shapes=[pltpu.VMEM((ATT_KV_HEADS, 4, ctx_len + qb + 2 * WINDOW, LANES), BF16)],
        compiler_params=_cparams("parallel", "arbitrary"),
        name="attention",
    )(pb, pb, pb, pb, pb, pf, sink_tab)


def _conf_tile(gm_ref, gp_ref, gn_ref, cg_ref, dww_ref, dwb_ref, lnw_ref, lnb_ref, pww_ref, pwb_ref, hbuf, abuf,
               first, last):
    tc = CONF_TILE

    def glu(v):
        return v[:, :CONV_WIDTH] * jax.nn.sigmoid(v[:, CONV_WIDTH:])

    rows = tc + 2 * CONF_HALO
    ext = jnp.concatenate([jnp.where(first, 0.0, glu(gp_ref[...])), glu(gm_ref[...]),
                           jnp.where(last, 0.0, glu(gn_ref[...]))], axis=0)
    hbuf[0] = ext
    for j in range(1, SUBLANES):
        hbuf[j, 0:rows - SUBLANES, :] = _shift_rows(ext, j)
    base = CONF_HALO - CONV_PAD
    for r in range(tc // CONF_SUB):
        acc = jnp.broadcast_to(dwb_ref[...], (CONF_SUB // SUBLANES, SUBLANES, CONV_WIDTH))
        for k in range(CONV_KERNEL):
            blk, res = divmod(base + k, SUBLANES)
            strip = hbuf[res, pl.ds(r * CONF_SUB + blk * SUBLANES, CONF_SUB), :]
            acc = acc + dww_ref[k] * strip.reshape(CONF_SUB // SUBLANES, SUBLANES, CONV_WIDTH)
        acc = acc.reshape(CONF_SUB, CONV_WIDTH)
        xc = acc - jnp.mean(acc, axis=-1, keepdims=True)
        var = jnp.mean(xc * xc, axis=-1, keepdims=True)
        ln = xc * lax.rsqrt(var + EPS) * lnw_ref[...] + lnb_ref[...]
        abuf[pl.ds(r * CONF_SUB, CONF_SUB), :] = jax.nn.silu(ln)
    out = _dot(abuf[...].astype(BF16), pww_ref[...]) + pwb_ref[...]
    return (out * jax.nn.silu(cg_ref[...])).astype(BF16)


def _conf_kernel(gm_ref, gp_ref, gn_ref, cg_ref, dww_ref, dwb_ref, lnw_ref, lnb_ref, pww_ref, pwb_ref, o_ref,
                 hbuf, abuf, *, tile_off, nct, nt):
    tile = pl.program_id(1) + tile_off
    first = (tile == 0) | (tile == nct)
    last = (tile == nct - 1) | (tile == nt - 1)
    o_ref[...] = _conf_tile(gm_ref, gp_ref, gn_ref, cg_ref, dww_ref, dwb_ref, lnw_ref, lnb_ref, pww_ref, pwb_ref,
                            hbuf, abuf, first, last)


def _conformer(pf, dw_w, dw_b, ln_w, ln_b, pw_w16, layer, pw_b, ctx_len, tile_off):
    bsz, t, _ = pf.shape
    tc = CONF_TILE
    nt = t // tc
    nct = ctx_len // tc
    hb = tc // CONF_HALO
    nhalo = t // CONF_HALO
    main = lambda b, i: (b, i + tile_off, 0)
    const2 = lambda b, i: (0, 0)
    glu_col = PF_GLU // (2 * CONV_WIDTH)
    in_specs = [
        pl.BlockSpec((None, tc, 2 * CONV_WIDTH), lambda b, i: (b, i + tile_off, glu_col)),
        pl.BlockSpec((None, CONF_HALO, 2 * CONV_WIDTH),
                     lambda b, i: (b, jnp.maximum((i + tile_off) * hb - 1, 0), glu_col)),
        pl.BlockSpec((None, CONF_HALO, 2 * CONV_WIDTH),
                     lambda b, i: (b, jnp.minimum((i + tile_off) * hb + hb, nhalo - 1), glu_col)),
        pl.BlockSpec((None, tc, CONV_WIDTH), lambda b, i: (b, i + tile_off, PF_CG // CONV_WIDTH)),
        pl.BlockSpec((CONV_KERNEL, SUBLANES, CONV_WIDTH), lambda b, i: (0, 0, 0)),
        pl.BlockSpec((SUBLANES, CONV_WIDTH), const2),
        pl.BlockSpec((1, CONV_WIDTH), const2),
        pl.BlockSpec((1, CONV_WIDTH), const2),
        pl.BlockSpec((None, CONV_WIDTH, CONV_WIDTH), lambda b, i: (layer, 0, 0)),
        pl.BlockSpec((1, CONV_WIDTH), const2),
    ]
    return pl.pallas_call(
        functools.partial(_conf_kernel, tile_off=tile_off, nct=nct, nt=nt),
        grid=(bsz, nt - tile_off),
        in_specs=in_specs,
        out_specs=pl.BlockSpec((None, tc, CONV_WIDTH), main),
        out_shape=jax.ShapeDtypeStruct((bsz, t, CONV_WIDTH), BF16),
        scratch_shapes=[
            pltpu.VMEM((SUBLANES, tc + 2 * CONF_HALO, CONV_WIDTH), F32),
            pltpu.VMEM((tc, CONV_WIDTH), F32),
        ],
        compiler_params=_cparams("parallel", "arbitrary"),
        name="conformer",
    )(pf, pf, pf, pf, dw_w, dw_b, ln_w, ln_b, pw_w16, pw_b)


def _outproj_kernel(c_ref, x_ref, os_ref, oa_ref, oc_ref, mod_ref, w_ref, fnw_ref, o_ref,
                    *, final, nct, tile_off):
    a0 = SSD_WIDTH
    c0 = SSD_WIDTH + ATT_WIDTH
    u = (_dot(os_ref[...], w_ref[0:a0, :]) + _dot(oa_ref[...], w_ref[a0:c0, :])
         + _dot(oc_ref[...], w_ref[c0:c0 + CONV_WIDTH, :]))
    x = x_ref[...]
    if tile_off < nct:
        x = jnp.where(pl.program_id(1) + tile_off < nct, c_ref[...], x)
    xn = x + mod_ref[2:3, :] * u
    if final:
        xn = xn * lax.rsqrt(jnp.mean(xn * xn, axis=-1, keepdims=True) + EPS) * fnw_ref[...]
    o_ref[...] = xn


def _outproj(csrc, xsrc, lat_off, o_ssd, o_att, o_conv, modsel, w_out16, layer, final_norm_w, nct, final):
    bsz, t, _ = o_ssd.shape
    d = xsrc.shape[-1]
    tm = ROW_TILE
    tile_off = nct if final else 0
    main = lambda b, i: (b, i + tile_off, 0)
    in_specs = _token_specs(tm, d, nct, lat_off, tile_off) + [
        pl.BlockSpec((None, tm, SSD_WIDTH), main),
        pl.BlockSpec((None, tm, ATT_WIDTH), main),
        pl.BlockSpec((None, tm, CONV_WIDTH), main),
        pl.BlockSpec((None, None, 3, d), lambda b, i: (b, jnp.where(i + tile_off >= nct, 1, 0), 0, 0)),
        pl.BlockSpec((None,) + w_out16.shape[1:], lambda b, i: (layer, 0, 0)),
        pl.BlockSpec((1, d), lambda b, i: (0, 0)),
    ]
    rows_out = t - tile_off * tm
    return pl.pallas_call(
        functools.partial(_outproj_kernel, final=final, nct=nct, tile_off=tile_off),
        grid=(bsz, rows_out // tm),
        in_specs=in_specs,
        out_specs=pl.BlockSpec((None, tm, d), lambda b, i: (b, i, 0)),
        out_shape=jax.ShapeDtypeStruct((bsz, rows_out, d), F32),
        compiler_params=_cparams("parallel", "arbitrary"),
        name="outproj_final" if final else "outproj",
    )(csrc, xsrc, o_ssd, o_att, o_conv, modsel, w_out16, final_norm_w)


def _mixout_kernel(c_ref, x_ref, os_ref, oa_ref, gm_ref, gp_ref, gn_ref, cg_ref, dww_ref, dwb_ref, lnw_ref, lnb_ref,
                   pww_ref, pwb_ref, mod_ref, w_ref, fnw_ref, o_ref, hbuf, abuf, *, final, nct, nt, tile_off):
    tile = pl.program_id(1) + tile_off
    first = (tile == 0) | (tile == nct)
    last = (tile == nct - 1) | (tile == nt - 1)
    a0 = SSD_WIDTH
    c0 = SSD_WIDTH + ATT_WIDTH
    u = _dot(os_ref[...], w_ref[0:a0, :]) + _dot(oa_ref[...], w_ref[a0:c0, :])
    oc = _conf_tile(gm_ref, gp_ref, gn_ref, cg_ref, dww_ref, dwb_ref, lnw_ref, lnb_ref, pww_ref, pwb_ref,
                    hbuf, abuf, first, last)
    u = u + _dot(oc, w_ref[c0:c0 + CONV_WIDTH, :])
    x = x_ref[...]
    if tile_off < nct:
        x = jnp.where(tile < nct, c_ref[...], x)
    xn = x + mod_ref[2:3, :] * u
    if final:
        xn = xn * lax.rsqrt(jnp.mean(xn * xn, axis=-1, keepdims=True) + EPS) * fnw_ref[...]
    o_ref[...] = xn


def _mixout(csrc, xsrc, lat_off, o_ssd, o_att, pf, dw_w, dw_b, ln_w, ln_b, pw_w16, pw_b, modsel, w_out16, layer,
            final_norm_w, ctx_len, final):
    bsz, t, _ = o_ssd.shape
    d = xsrc.shape[-1]
    tm = ROW_TILE
    nt = t // tm
    nct = ctx_len // tm
    tile_off = nct if final else 0
    hb = tm // CONF_HALO
    nhalo = t // CONF_HALO
    main = lambda b, i: (b, i + tile_off, 0)
    const2 = lambda b, i: (0, 0)
    glu_col = PF_GLU // (2 * CONV_WIDTH)
    in_specs = _token_specs(tm, d, nct, lat_off, tile_off) + [
        pl.BlockSpec((None, tm, SSD_WIDTH), main),
        pl.BlockSpec((None, tm, ATT_WIDTH), main),
        pl.BlockSpec((None, tm, 2 * CONV_WIDTH), lambda b, i: (b, i + tile_off, glu_col)),
        pl.BlockSpec((None, CONF_HALO, 2 * CONV_WIDTH),
                     lambda b, i: (b, jnp.maximum((i + tile_off) * hb - 1, 0), glu_col)),
        pl.BlockSpec((None, CONF_HALO, 2 * CONV_WIDTH),
                     lambda b, i: (b, jnp.minimum((i + tile_off) * hb + hb, nhalo - 1), glu_col)),
        pl.BlockSpec((None, tm, CONV_WIDTH), lambda b, i: (b, i + tile_off, PF_CG // CONV_WIDTH)),
        pl.BlockSpec((CONV_KERNEL, SUBLANES, CONV_WIDTH), lambda b, i: (0, 0, 0)),
        pl.BlockSpec((SUBLANES, CONV_WIDTH), const2),
        pl.BlockSpec((1, CONV_WIDTH), const2),
        pl.BlockSpec((1, CONV_WIDTH), const2),
        pl.BlockSpec((None, CONV_WIDTH, CONV_WIDTH), lambda b, i: (layer, 0, 0)),
        pl.BlockSpec((1, CONV_WIDTH), const2),
        pl.BlockSpec((None, None, 3, d), lambda b, i: (b, jnp.where(i + tile_off >= nct, 1, 0), 0, 0)),
        pl.BlockSpec((None,) + w_out16.shape[1:], lambda b, i: (layer, 0, 0)),
        pl.BlockSpec((1, d), const2),
    ]
    rows_out = t - tile_off * tm
    return pl.pallas_call(
        functools.partial(_mixout_kernel, final=final, nct=nct, nt=nt, tile_off=tile_off),
        grid=(bsz, rows_out // tm),
        in_specs=in_specs,
        out_specs=pl.BlockSpec((None, tm, d), lambda b, i: (b, i, 0)),
        out_shape=jax.ShapeDtypeStruct((bsz, rows_out, d), F32),
        scratch_shapes=[
            pltpu.VMEM((SUBLANES, tm + 2 * CONF_HALO, CONV_WIDTH), F32),
            pltpu.VMEM((tm, CONV_WIDTH), F32),
        ],
        compiler_params=_cparams("parallel", "arbitrary"),
        name="mixout_final" if final else "mixout",
    )(csrc, xsrc, o_ssd, o_att, pf, pf, pf, pf, dw_w, dw_b, ln_w, ln_b, pw_w16, pw_b, modsel, w_out16, final_norm_w)


def _rope_tables(seq, ctx_len):
    rows = seq // GRID_W
    row = jnp.repeat(jnp.arange(rows, dtype=F32), GRID_W)
    col = jnp.tile(jnp.arange(GRID_W, dtype=F32), rows)
    axis_dim = ATT_HEAD_DIM // 2
    inv_freq = ROPE_BASE ** (-jnp.arange(0, axis_dim, 2, dtype=F32) / axis_dim)
    ar = row[:, None] * inv_freq
    ac = col[:, None] * inv_freq
    cos = jnp.concatenate([jnp.cos(ar), jnp.cos(ar), jnp.cos(ac), jnp.cos(ac)], axis=1)
    sin = jnp.concatenate([-jnp.sin(ar), jnp.sin(ar), -jnp.sin(ac), jnp.sin(ac)], axis=1)
    reps = LANES // ATT_HEAD_DIM
    cos = jnp.concatenate([jnp.ones((ctx_len, LANES), F32), jnp.tile(cos, (1, reps))], axis=0)
    sin = jnp.concatenate([jnp.zeros((ctx_len, LANES), F32), jnp.tile(sin, (1, reps))], axis=0)
    return jnp.concatenate([cos, sin], axis=1)


def _head_lanes(a):
    rep = jnp.concatenate([a, a, a], axis=-1)
    return jnp.pad(rep, [(0, 0)] * (a.ndim - 1) + [(0, LANES - 3 * SSD_HEADS)])


def _pack_w_in(w_in):
    dt0 = SSD_XBC + SSD_WIDTH
    dt_w = w_in[..., dt0:dt0 + 2 * SSD_HEADS]
    packed = jnp.concatenate([w_in[..., :dt0], w_in[..., dt0 + 2 * SSD_HEADS:],
                              _head_lanes(dt_w[..., :SSD_HEADS]), _head_lanes(dt_w[..., SSD_HEADS:])], axis=-1)
    assert packed.shape[-1] == W_WIDTH
    return packed.astype(BF16)


def kernel(x, c, ctx, c_ctx, w_mod, b_mod, norm_w, w_in, ssd_conv_w, ssd_conv_b, ssd_dt_bias, ssd_a_log, ssd_d,
           ssd_norm_w, attn_sink, conv_dw_w, conv_dw_b, conv_ln_w, conv_ln_b, conv_pw_w, conv_pw_b, w_out,
           final_norm_w):
    bsz, seq, d = x.shape
    ctx_len = ctx.shape[1]
    depth = w_in.shape[0]
    assert ctx_len % ROW_TILE == 0 and seq % ROW_TILE == 0 and seq % GRID_W == 0
    assert ROW_TILE == ATT_TILE == CONF_TILE
    nct = ctx_len // ROW_TILE

    cond_rows = -(-(bsz + 1) // SUBLANES) * SUBLANES
    cond = jnp.zeros((cond_rows, d), F32).at[:bsz].set(c).at[bsz].set(c_ctx)
    mod = _modulation(cond, w_mod, b_mod)
    rope_t = _rope_tables(seq, ctx_len)
    csrc, xsrc, lat_off = ctx, x, 0
    w_in16 = _pack_w_in(w_in)
    w_out16 = w_out.astype(BF16)
    pw_w16 = conv_pw_w.astype(BF16)
    norm_w3 = norm_w[:, None, :]

    out = None
    for l in range(depth):
        final = l == depth - 1
        lat = mod[l, :bsz]
        ctxm = jnp.broadcast_to(mod[l, bsz], lat.shape)
        modsel = jnp.stack([ctxm, lat], axis=1).reshape(bsz, 2, 3, d)
        xbc, pf, pb = _inproj(csrc, xsrc, lat_off, ctx_len + seq, modsel, norm_w3, rope_t, w_in16, l, nct)

        o_ssd = _ssd(xbc, pf, ssd_conv_w[l], ssd_conv_b[l][None], _head_lanes(ssd_dt_bias[l])[:, None],
                     _head_lanes(ssd_a_log[l])[:, None], jnp.repeat(ssd_d[l], SSD_HEAD_DIM)[None],
                     ssd_norm_w[l][None], ctx_len)
        tile_off = nct if final else 0
        sink_tab = jnp.broadcast_to(attn_sink[l][:, None] * LOG2E, (ATT_HEADS, LANES))
        o_att = _attention(pf, pb, sink_tab, ctx_len, tile_off)
        dw_w = jnp.broadcast_to(conv_dw_w[l][:, None, :], (CONV_KERNEL, SUBLANES, CONV_WIDTH))
        dw_b = jnp.broadcast_to(conv_dw_b[l][None], (SUBLANES, CONV_WIDTH))
        out = _mixout(csrc, xsrc, lat_off, o_ssd, o_att, pf, dw_w, dw_b, conv_ln_w[l][None], conv_ln_b[l][None], pw_w16,
                      conv_pw_b[l][None], modsel, w_out16, l, final_norm_w[None], ctx_len, final)
        csrc, xsrc, lat_off = out, out, nct
    return out
```

```python
import functools

import jax
import jax.numpy as jnp
from jax import lax
from jax.experimental import pallas as pl
from jax.experimental.pallas import tpu as pltpu

F32 = jnp.float32
BF16 = jnp.bfloat16

GRID_W = 64
EPS = 1e-6

SSD_HEADS = 16
SSD_HEAD_DIM = 64
SSD_WIDTH = SSD_HEADS * SSD_HEAD_DIM
SSD_GROUPS = 2
SSD_STATE = 128
SSD_CONV = 5
SSD_CHUNK = 128
SSD_GN = SSD_GROUPS * SSD_STATE
SSD_XBC = SSD_WIDTH + 2 * SSD_GN
GROUP_COLS = SSD_WIDTH // SSD_GROUPS

ATT_HEADS = 8
ATT_KV_HEADS = 2
ATT_HEAD_DIM = 64
ATT_WIDTH = ATT_HEADS * ATT_HEAD_DIM
ATT_KV_WIDTH = ATT_KV_HEADS * ATT_HEAD_DIM
WINDOW = 128
ROPE_BASE = 10000.0
LOG2E = 1.4426950408889634
ATT_SCALE = ATT_HEAD_DIM ** -0.5 * LOG2E

CONV_WIDTH = 512
CONV_KERNEL = 31
CONV_PAD = (CONV_KERNEL - 1) // 2

LANES = 128

PF_Z = 0
PF_GLU = PF_Z + SSD_WIDTH
PF_AG = PF_GLU + 2 * CONV_WIDTH
PF_CG = PF_AG + ATT_WIDTH
PF_DT = PF_CG + CONV_WIDTH
PF_WIDTH = PF_DT + 2 * LANES
PB_Q = 0
PB_KV = PB_Q + ATT_WIDTH
PB_WIDTH = PB_KV + 2 * ATT_KV_WIDTH

W_XBC = 0
W_Z = W_XBC + SSD_XBC
W_Q = W_Z + SSD_WIDTH
W_K = W_Q + ATT_WIDTH
W_V = W_K + ATT_KV_WIDTH
W_AG = W_V + ATT_KV_WIDTH
W_GLU = W_AG + ATT_WIDTH
W_CG = W_GLU + 2 * CONV_WIDTH
W_DT = W_CG + CONV_WIDTH
W_WIDTH = W_DT + 2 * LANES
SUBLANES = 8
VMEM_LIMIT = 56 * 1024 * 1024

ROW_TILE = 256
ATT_TILE = 256
ATT_QBLOCK = 256
CONF_TILE = 256
CONF_HALO = 16
CONF_SUB = 32
SSD_HALO = SUBLANES
SSD_STEP = 2 * SSD_CHUNK


def _dot(a, b):
    return jnp.dot(a, b, preferred_element_type=F32)


def _dot_nt(a, b):
    return lax.dot_general(a, b, (((1,), (1,)), ((), ())), preferred_element_type=F32)


def _split3(x):
    hi = x.astype(BF16)
    r1 = x - hi.astype(F32)
    mid = r1.astype(BF16)
    lo = (r1 - mid.astype(F32)).astype(BF16)
    return hi, mid, lo


def _pack3(x, width):
    hi, mid, lo = _split3(x)
    lane = lax.broadcasted_iota(jnp.int32, x.shape, 1)
    return jnp.where(lane < width, hi, jnp.where(lane < 2 * width, mid, lo))


def _shift_rows(x, j):
    rows, w = x.shape
    g = x.reshape(rows // SUBLANES, SUBLANES, w)
    r = pltpu.roll(g, SUBLANES - j, 1)
    sub = lax.broadcasted_iota(jnp.int32, (1, SUBLANES, w), 1)
    return jnp.where(sub < SUBLANES - j, r[:-1], r[1:]).reshape(rows - SUBLANES, w)


def _cparams(*sem, flags=None):
    return pltpu.CompilerParams(dimension_semantics=sem, vmem_limit_bytes=VMEM_LIMIT, flags=flags)


def _mod_kernel(c_ref, w_ref, b_ref, o_ref):
    a = jax.nn.silu(c_ref[...]).astype(BF16)
    o_ref[...] = _dot(a, w_ref[...].astype(BF16)) + b_ref[...]


def _modulation(cond, w_mod, b_mod):
    depth, d, n = w_mod.shape
    rows = cond.shape[0]
    tn = n // 2 if (n // 2) % LANES == 0 else n
    return pl.pallas_call(
        _mod_kernel,
        grid=(depth, n // tn),
        in_specs=[
            pl.BlockSpec((rows, d), lambda l, j: (0, 0)),
            pl.BlockSpec((None, d, tn), lambda l, j: (l, 0, j)),
            pl.BlockSpec((None, 1, tn), lambda l, j: (l, 0, j)),
        ],
        out_specs=pl.BlockSpec((None, rows, tn), lambda l, j: (l, 0, j)),
        out_shape=jax.ShapeDtypeStruct((depth, rows, n), F32),
        compiler_params=_cparams("arbitrary", "arbitrary"),
        name="modulation",
    )(cond, w_mod, b_mod.reshape(depth, 1, n))


def _rope(t, cos, sin):
    width = t.shape[-1]
    lane = lax.broadcasted_iota(jnp.int32, t.shape, 1)
    first = (lane % 32) < 16
    rot = jnp.where(first, pltpu.roll(t, width - 16, 1), pltpu.roll(t, 16, 1))
    return t * cos + rot * sin


def _inproj_kernel(c_ref, x_ref, mod_ref, nw_ref, rope_ref, w_ref, xbc_ref, pf_ref, pb_ref, *, nct):
    wxbc = w_ref.at[:, W_XBC:W_XBC + SSD_XBC]
    wz = w_ref.at[:, W_Z:W_Z + SSD_WIDTH]
    wq = w_ref.at[:, W_Q:W_Q + ATT_WIDTH]
    wk = w_ref.at[:, W_K:W_K + ATT_KV_WIDTH]
    wv = w_ref.at[:, W_V:W_V + ATT_KV_WIDTH]
    wag = w_ref.at[:, W_AG:W_AG + ATT_WIDTH]
    wglu = w_ref.at[:, W_GLU:W_GLU + 2 * CONV_WIDTH]
    wcg = w_ref.at[:, W_CG:W_CG + CONV_WIDTH]
    wdt = w_ref.at[:, W_DT:W_DT + 2 * LANES]
    x = jnp.where(pl.program_id(1) < nct, c_ref[...], x_ref[...])
    y = x * lax.rsqrt(jnp.mean(x * x, axis=-1, keepdims=True) + EPS) * nw_ref[...]
    h = (y * (1.0 + mod_ref[1:2, :]) + mod_ref[0:1, :]).astype(BF16)
    cos = rope_ref[:, :LANES]
    sin = rope_ref[:, LANES:]
    reps = ATT_WIDTH // LANES
    cos_q = jnp.concatenate([cos] * reps, axis=1)
    sin_q = jnp.concatenate([sin] * reps, axis=1)
    pb_ref[:, PB_Q:PB_Q + ATT_WIDTH] = (_rope(_dot(h, wq[...]), cos_q, sin_q) * ATT_SCALE).astype(BF16)
    pb_ref[:, PB_KV:PB_KV + ATT_KV_WIDTH] = _rope(_dot(h, wk[...]), cos, sin).astype(BF16)
    pb_ref[:, PB_KV + ATT_KV_WIDTH:PB_KV + 2 * ATT_KV_WIDTH] = _dot(h, wv[...]).astype(BF16)
    xbc_ref[...] = _dot(h, wxbc[...])
    pf_ref[:, PF_Z:PF_Z + SSD_WIDTH] = _dot(h, wz[...])
    pf_ref[:, PF_DT:PF_DT + 2 * LANES] = _dot(h, wdt[...])
    pf_ref[:, PF_AG:PF_AG + ATT_WIDTH] = _dot(h, wag[...])
    pf_ref[:, PF_GLU:PF_GLU + 2 * CONV_WIDTH] = _dot(h, wglu[...])
    pf_ref[:, PF_CG:PF_CG + CONV_WIDTH] = _dot(h, wcg[...])


def _token_specs(tm, d, nct, lat_off, tile_off):
    return [
        pl.BlockSpec((None, tm, d), lambda b, i: (b, jnp.minimum(i + tile_off, nct - 1), 0)),
        pl.BlockSpec((None, tm, d), lambda b, i: (b, jnp.maximum(i + tile_off - nct, 0) + lat_off, 0)),
    ]


def _inproj(csrc, xsrc, lat_off, t, modsel, norm_w, rope_t, w_packed, layer, nct):
    bsz, _, d = xsrc.shape
    tm = ROW_TILE
    row_map = lambda b, i: (b, i, 0)
    in_specs = _token_specs(tm, d, nct, lat_off, 0) + [
        pl.BlockSpec((None, None, 3, d), lambda b, i: (b, jnp.where(i >= nct, 1, 0), 0, 0)),
        pl.BlockSpec((None, 1, d), lambda b, i: (layer, 0, 0)),
        pl.BlockSpec((tm, 2 * LANES), lambda b, i: (i, 0)),
        pl.BlockSpec((None, d, W_WIDTH), lambda b, i: (layer, 0, 0)),
    ]
    widths = (SSD_XBC, PF_WIDTH, PB_WIDTH)
    return pl.pallas_call(
        functools.partial(_inproj_kernel, nct=nct),
        grid=(bsz, t // tm),
        in_specs=in_specs,
        out_specs=[pl.BlockSpec((None, tm, w), row_map) for w in widths],
        out_shape=[jax.ShapeDtypeStruct((bsz, t, w), dt) for w, dt in zip(widths, (F32, F32, BF16))],
        compiler_params=_cparams("parallel", "arbitrary"),
        name="inproj",
    )(csrc, xsrc, modsel, norm_w, rope_t, w_packed)


def _ssd_prep(xs, bc, dt_raw, dtb, alog, fwd):
    q = SSD_CHUNK
    bb = bc[:, :SSD_GN]
    cb16 = bc[:, SSD_GN:]

    groups = [slice(g * SSD_STATE, (g + 1) * SSD_STATE) for g in range(SSD_GROUPS)]
    scores_g = [_dot_nt(cb16[:, gs], bb[:, gs]) for gs in groups]

    lane = lax.broadcasted_iota(jnp.int32, (q, LANES), 1)
    head_lane = lane < 3 * SSD_HEADS
    dt = jnp.where(head_lane, jax.nn.softplus(dt_raw + dtb), 0.0)
    da = dt * (-jnp.exp(alog))
    li = lax.broadcasted_iota(jnp.int32, (q, q), 0)
    si = lax.broadcasted_iota(jnp.int32, (q, q), 1)
    causal = (si <= li) if fwd else (si >= li)
    ones = jnp.where(causal, 1.0, 0.0).astype(BF16)
    da_hi, da_mid, da_lo = _split3(da)
    cs = _dot(ones, da_hi) + _dot(ones, da_mid) + _dot(ones, da_lo)
    total = jnp.sum(da, axis=0, keepdims=True)
    to_end = jnp.exp(total - cs)
    from_start = jnp.where(head_lane, jnp.exp(cs), 0.0)
    chunk_decay = jnp.where(head_lane[:SUBLANES], jnp.exp(total), 0.0)
    cs_t = cs.T

    er = lax.broadcasted_iota(jnp.int32, (LANES, SSD_WIDTH), 0)
    ec = lax.broadcasted_iota(jnp.int32, (LANES, SSD_WIDTH), 1)
    spread = jnp.where((ec // SSD_HEAD_DIM == er % SSD_HEADS) & (er < 3 * SSD_HEADS), 1.0, 0.0).astype(BF16)
    dt_e = _dot(_pack3(dt, SSD_HEADS), spread)
    stacked = jnp.concatenate([dt * to_end, from_start, chunk_decay], axis=0)
    stacked_e = _dot(_pack3(stacked, SSD_HEADS), spread)
    w_e = stacked_e[:q]
    fs_e = stacked_e[q:2 * q]
    cd_e = stacked_e[2 * q:2 * q + 1]

    xd = (xs * dt_e).astype(BF16)
    xdw = (xs * w_e).astype(BF16)
    low = lax.broadcasted_iota(jnp.int32, (q, LANES), 1) < SSD_HEAD_DIM
    zero16 = jnp.zeros((q, LANES), BF16)
    y_diag, new_states = [], []
    for g in range(SSD_GROUPS):
        gs = groups[g]
        gc = slice(g * GROUP_COLS, (g + 1) * GROUP_COLS)
        scores = scores_g[g]
        new_states.append(_dot(bb[:, gs].astype(F32).T.astype(BF16), xdw[:, gc]))
        for pr in range(GROUP_COLS // LANES):
            h0 = (g * GROUP_COLS + pr * LANES) // SSD_HEAD_DIM
            cols = slice(h0 * SSD_HEAD_DIM, h0 * SSD_HEAD_DIM + LANES)
            xpair = xd[:, cols]
            m0 = (scores * jnp.exp(jnp.where(causal, cs[:, h0:h0 + 1] - cs_t[h0:h0 + 1, :], -jnp.inf))).astype(BF16)
            m1 = (scores * jnp.exp(jnp.where(causal, cs[:, h0 + 1:h0 + 2] - cs_t[h0 + 1:h0 + 2, :], -jnp.inf))).astype(BF16)
            block_diag = jnp.concatenate([jnp.where(low, xpair, zero16), jnp.where(low, zero16, xpair)], axis=0)
            y_diag.append(_dot(jnp.concatenate([m0, m1], axis=1), block_diag))
    return y_diag, new_states, cb16, fs_e, cd_e


def _ssd_finish(prep, st_ref):
    y_diag, new_states, cb16, fs_e, cd_e = prep
    ys = []
    for g in range(SSD_GROUPS):
        gs = slice(g * SSD_STATE, (g + 1) * SSD_STATE)
        gc = slice(g * GROUP_COLS, (g + 1) * GROUP_COLS)
        entering = st_ref[g]
        y_off = _dot(cb16[:, gs], entering.astype(BF16)) * fs_e[:, gc]
        st_ref[g] = entering * cd_e[:, gc] + new_states[g]
        per_group = GROUP_COLS // LANES
        for pr in range(per_group):
            ys.append(y_diag[g * per_group + pr] + y_off[:, pr * LANES:(pr + 1) * LANES])
    return jnp.concatenate(ys, axis=1)


def _ssd_bwd_block(k, ncb, nb):
    return jnp.where(k < ncb, ncb - 1 - k, nb + ncb - 1 - k)


def _ssd_kernel(xm_ref, xp_ref, xn_ref, dt_ref, z_ref, cw_ref, cb_ref, dtb_ref, alog_ref, dvec_ref, nw_ref,
                o_ref, yb_ref, st_ref, xs_ref, bc_ref, *, ncb, nb):
    q = SSD_CHUNK
    rows = SSD_STEP
    n_sub = rows // q
    s = pl.program_id(1)

    def conv_block(cb):
        first = (cb == 0) | (cb == ncb)
        last = (cb == ncb - 1) | (cb == nb - 1)
        xext = jnp.concatenate([jnp.where(first, 0.0, xp_ref[...]), xm_ref[...],
                                jnp.where(last, 0.0, xn_ref[...])], axis=0)
        acc = cb_ref[...]
        for k in range(SSD_CONV):
            d = k - (SSD_CONV - 1) // 2
            if d == 0:
                tap = xext[SSD_HALO:SSD_HALO + rows]
            elif d > 0:
                tap = _shift_rows(xext, d)[SSD_HALO:SSD_HALO + rows]
            else:
                tap = _shift_rows(xext, SSD_HALO + d)[:rows]
            acc = acc + cw_ref[k:k + 1, :] * tap
        u = jax.nn.silu(acc)
        dst = pl.ds(pl.multiple_of(cb * rows, rows), rows)
        xs_ref[dst, :] = u[:, :SSD_WIDTH]
        bc_ref[dst, :] = u[:, SSD_WIDTH:].astype(BF16)

    def chunk_rows(blk, j):
        return pl.ds(pl.multiple_of(blk * rows + j * q, q), q)

    @pl.when((s == 0) | (s == nb))
    def _():
        st_ref[...] = jnp.zeros_like(st_ref)

    @pl.when(s <= nb - 1)
    def _():
        conv_block(_ssd_bwd_block(s, ncb, nb))

    @pl.when(s < nb)
    def _():
        blk = _ssd_bwd_block(s, ncb, nb)
        order = tuple(reversed(range(n_sub)))
        preps = [_ssd_prep(xs_ref[chunk_rows(blk, j), :], bc_ref[chunk_rows(blk, j), :], dt_ref[j * q:(j + 1) * q, :],
                           dtb_ref[...], alog_ref[...], False) for j in order]
        for j, prep in zip(order, preps):
            yb_ref[chunk_rows(blk, j), :] = _ssd_finish(prep, st_ref)

    @pl.when(s >= nb)
    def _():
        blk = s - nb
        preps = [_ssd_prep(xs_ref[chunk_rows(blk, j), :], bc_ref[chunk_rows(blk, j), :], dt_ref[j * q:(j + 1) * q, :],
                           dtb_ref[...], alog_ref[...], True) for j in range(n_sub)]
        for j, prep in enumerate(preps):
            y = _ssd_finish(prep, st_ref)
            tot = y + yb_ref[chunk_rows(blk, j), :] + dvec_ref[...] * xs_ref[chunk_rows(blk, j), :]
            gated = tot * jax.nn.silu(z_ref[j * q:(j + 1) * q, :])
            outs = []
            for g in range(SSD_GROUPS):
                gc = slice(g * GROUP_COLS, (g + 1) * GROUP_COLS)
                v = gated[:, gc]
                outs.append(v * lax.rsqrt(jnp.mean(v * v, axis=-1, keepdims=True) + EPS) * nw_ref[:, gc])
            o_ref[j * q:(j + 1) * q, :] = jnp.concatenate(outs, axis=1).astype(BF16)


def _ssd(xbc, pf, conv_w, conv_b, dt_bias, a_log, dvec, norm_w, ctx_len):
    bsz, t, _ = pf.shape
    rows = SSD_STEP
    nb = t // rows
    ncb = ctx_len // rows
    hb = rows // SSD_HALO
    nhalo = t // SSD_HALO
    const2 = lambda b, s: (0, 0)
    xblk = lambda s: _ssd_bwd_block(jnp.minimum(s, nb - 1), ncb, nb)
    cur = lambda s: jnp.where(s < nb, _ssd_bwd_block(s, ncb, nb), s - nb)
    phase = lambda s: jnp.where(s < nb, 1, 0)
    fwd_blk = lambda s: jnp.maximum(s - nb, 0)
    in_specs = [
        pl.BlockSpec((None, rows, SSD_XBC), lambda b, s: (b, xblk(s), 0)),
        pl.BlockSpec((None, SSD_HALO, SSD_XBC), lambda b, s: (b, jnp.maximum(xblk(s) * hb - 1, 0), 0)),
        pl.BlockSpec((None, SSD_HALO, SSD_XBC), lambda b, s: (b, jnp.minimum(xblk(s) * hb + hb, nhalo - 1), 0)),
        pl.BlockSpec((None, rows, LANES), lambda b, s: (b, cur(s), PF_DT // LANES + phase(s))),
        pl.BlockSpec((None, rows, SSD_WIDTH), lambda b, s: (b, fwd_blk(s), PF_Z // SSD_WIDTH)),
        pl.BlockSpec((SSD_CONV, SSD_XBC), const2),
        pl.BlockSpec((1, SSD_XBC), const2),
        pl.BlockSpec((None, 1, LANES), lambda b, s: (phase(s), 0, 0)),
        pl.BlockSpec((None, 1, LANES), lambda b, s: (phase(s), 0, 0)),
        pl.BlockSpec((1, SSD_WIDTH), const2),
        pl.BlockSpec((1, SSD_WIDTH), const2),
    ]
    return pl.pallas_call(
        functools.partial(_ssd_kernel, ncb=ncb, nb=nb),
        grid=(bsz, 2 * nb),
        in_specs=in_specs,
        out_specs=pl.BlockSpec((None, rows, SSD_WIDTH), lambda b, s: (b, fwd_blk(s), 0)),
        out_shape=jax.ShapeDtypeStruct((bsz, t, SSD_WIDTH), BF16),
        scratch_shapes=[
            pltpu.VMEM((t, SSD_WIDTH), F32),
            pltpu.VMEM((SSD_GROUPS, SSD_STATE, GROUP_COLS), F32),
            pltpu.VMEM((t, SSD_WIDTH), F32),
            pltpu.VMEM((t, 2 * SSD_GN), BF16),
        ],
        compiler_params=_cparams("parallel", "arbitrary"),
        name="ssd",
    )(xbc, xbc, xbc, pf, pf, conv_w, conv_b, dt_bias, a_log, dvec, norm_w)


def _swap_halves(blk):
    half = blk.shape[-1] // 2
    return jnp.concatenate([blk[:, half:], blk[:, :half]], axis=1)


def _head_pads(blk, g, low):
    zero = jnp.zeros_like(blk)
    sw = _swap_halves(blk)
    if g == 0:
        return jnp.where(low, blk, zero), jnp.where(low, zero, sw)
    return jnp.where(low, sw, zero), jnp.where(low, zero, blk)


def _attn_kernel(q_ref, kvm_ref, kvp_ref, kvn_ref, kvc_ref, ag_ref, sink_ref, o_ref, pad_ref,
                 *, tile_off, nct, seq, ctx_len):
    qb = ATT_TILE
    blk = ATT_QBLOCK
    wk = blk + 2 * WINDOW
    tile = pl.program_id(1) + tile_off
    lat0 = (tile - nct) * qb
    low_k = lax.broadcasted_iota(jnp.int32, (1, LANES), 1) < ATT_HEAD_DIM

    kv = jnp.concatenate([kvc_ref[...], kvp_ref[...], kvm_ref[...], kvn_ref[...]], axis=0)
    for g in range(ATT_KV_HEADS):
        ka, kb = _head_pads(kv[:, :ATT_KV_WIDTH], g, low_k)
        va, vb = _head_pads(kv[:, ATT_KV_WIDTH:], g, low_k)
        pad_ref[g, 0] = ka
        pad_ref[g, 1] = kb
        pad_ref[g, 2] = va
        pad_ref[g, 3] = vb

    r = lax.broadcasted_iota(jnp.int32, (blk, wk), 0)
    col = lax.broadcasted_iota(jnp.int32, (blk, wk), 1)
    in_band = jnp.abs(col - WINDOW - r) <= WINDOW
    neg = jnp.finfo(F32).min
    rep = ATT_HEADS // ATT_KV_HEADS
    pairs_per_group = rep * ATT_HEAD_DIM // LANES
    for a in range(qb // blk):
        rows = slice(a * blk, (a + 1) * blk)
        kpos = lat0 + a * blk - WINDOW + col
        valid = in_band & (kpos >= 0) & (kpos < seq) & (lat0 >= 0)
        win0 = ctx_len + a * blk
        def operand(g, slot):
            if a == 0:
                return pad_ref[g, slot, 0:win0 + wk, :]
            return jnp.concatenate([pad_ref[g, slot, 0:ctx_len, :], pad_ref[g, slot, win0:win0 + wk, :]], axis=0)

        for g in range(ATT_KV_HEADS):
            for pr in range(pairs_per_group):
                p = g * pairs_per_group + pr
                qp = q_ref[rows, p * LANES:(p + 1) * LANES]
                acc = jnp.zeros((blk, LANES), F32)
                invs = []
                for j in range(2):
                    head = 2 * p + j
                    sink = sink_ref[head:head + 1, 0:1]
                    s = _dot_nt(qp, operand(g, j))
                    s = jnp.concatenate([s[:, :ctx_len], jnp.where(valid, s[:, ctx_len:], neg)], axis=1)
                    m = jnp.maximum(jnp.max(s, axis=-1, keepdims=True), sink)
                    pexp = jnp.exp2(s - m)
                    den = jnp.sum(pexp, axis=-1, keepdims=True) + jnp.exp2(sink - m)
                    acc = acc + _dot(pexp.astype(BF16), operand(g, 2 + j))
                    invs.append(1.0 / den)
                gate = jax.nn.silu(ag_ref[rows, p * LANES:(p + 1) * LANES])
                o_ref[rows, p * LANES:(p + 1) * LANES] = (acc * jnp.where(low_k, invs[0], invs[1]) * gate).astype(BF16)


def _attention(pf, pb, sink_tab, ctx_len, tile_off):
    bsz, t, _ = pf.shape
    qb = ATT_TILE
    nct = ctx_len // qb
    hb = qb // WINDOW
    ctx_hb = ctx_len // WINDOW
    n_hb = t // WINDOW
    ntiles = t // qb - tile_off
    kvw = 2 * ATT_KV_WIDTH
    kv_col = PB_KV // kvw
    in_specs = [
        pl.BlockSpec((None, qb, ATT_WIDTH), lambda b, i: (b, i + tile_off, PB_Q // ATT_WIDTH)),
        pl.BlockSpec((None, qb, kvw), lambda b, i: (b, i + tile_off, kv_col)),
        pl.BlockSpec((None, WINDOW, kvw), lambda b, i: (b, jnp.maximum((i + tile_off) * hb - 1, ctx_hb), kv_col)),
        pl.BlockSpec((None, WINDOW, kvw), lambda b, i: (b, jnp.minimum((i + tile_off) * hb + hb, n_hb - 1), kv_col)),
        pl.BlockSpec((None, ctx_len, kvw), lambda b, i: (b, 0, kv_col)),
        pl.BlockSpec((None, qb, ATT_WIDTH), lambda b, i: (b, i + tile_off, PF_AG // ATT_WIDTH)),
        pl.BlockSpec((ATT_HEADS, LANES), lambda b, i: (0, 0)),
    ]
    return pl.pallas_call(
        functools.partial(_attn_kernel, tile_off=tile_off, nct=nct, seq=t - ctx_len, ctx_len=ctx_len),
        grid=(bsz, ntiles),
        in_specs=in_specs,
        out_specs=pl.BlockSpec((None, qb, ATT_WIDTH), lambda b, i: (b, i + tile_off, 0)),
        out_shape=jax.ShapeDtypeStruct((bsz, t, ATT_WIDTH), BF16),
        scratch_shapes=[pltpu.VMEM((ATT_KV_HEADS, 4, ctx_len + qb + 2 * WINDOW, LANES), BF16)],
        compiler_params=_cparams("parallel", "arbitrary"),
        name="attention",
    )(pb, pb, pb, pb, pb, pf, sink_tab)


def _conf_tile(gm_ref, gp_ref, gn_ref, cg_ref, dww_ref, dwb_ref, lnw_ref, lnb_ref, pww_ref, pwb_ref, hbuf, abuf,
               first, last):
    tc = CONF_TILE

    def glu(v):
        return v[:, :CONV_WIDTH] * jax.nn.sigmoid(v[:, CONV_WIDTH:])

    rows = tc + 2 * CONF_HALO
    ext = jnp.concatenate([jnp.where(first, 0.0, glu(gp_ref[...])), glu(gm_ref[...]),
                           jnp.where(last, 0.0, glu(gn_ref[...]))], axis=0)
    hbuf[0] = ext
    for j in range(1, SUBLANES):
        hbuf[j, 0:rows - SUBLANES, :] = _shift_rows(ext, j)
    base = CONF_HALO - CONV_PAD
    for r in range(tc // CONF_SUB):
        acc = jnp.broadcast_to(dwb_ref[...], (CONF_SUB // SUBLANES, SUBLANES, CONV_WIDTH))
        for k in range(CONV_KERNEL):
            blk, res = divmod(base + k, SUBLANES)
            strip = hbuf[res, pl.ds(r * CONF_SUB + blk * SUBLANES, CONF_SUB), :]
            acc = acc + dww_ref[k] * strip.reshape(CONF_SUB // SUBLANES, SUBLANES, CONV_WIDTH)
        acc = acc.reshape(CONF_SUB, CONV_WIDTH)
        xc = acc - jnp.mean(acc, axis=-1, keepdims=True)
        var = jnp.mean(xc * xc, axis=-1, keepdims=True)
        ln = xc * lax.rsqrt(var + EPS) * lnw_ref[...] + lnb_ref[...]
        abuf[pl.ds(r * CONF_SUB, CONF_SUB), :] = jax.nn.silu(ln)
    out = _dot(abuf[...].astype(BF16), pww_ref[...]) + pwb_ref[...]
    return (out * jax.nn.silu(cg_ref[...])).astype(BF16)


def _conf_kernel(gm_ref, gp_ref, gn_ref, cg_ref, dww_ref, dwb_ref, lnw_ref, lnb_ref, pww_ref, pwb_ref, o_ref,
                 hbuf, abuf, *, tile_off, nct, nt):
    tile = pl.program_id(1) + tile_off
    first = (tile == 0) | (tile == nct)
    last = (tile == nct - 1) | (tile == nt - 1)
    o_ref[...] = _conf_tile(gm_ref, gp_ref, gn_ref, cg_ref, dww_ref, dwb_ref, lnw_ref, lnb_ref, pww_ref, pwb_ref,
                            hbuf, abuf, first, last)


def _conformer(pf, dw_w, dw_b, ln_w, ln_b, pw_w16, layer, pw_b, ctx_len, tile_off):
    bsz, t, _ = pf.shape
    tc = CONF_TILE
    nt = t // tc
    nct = ctx_len // tc
    hb = tc // CONF_HALO
    nhalo = t // CONF_HALO
    main = lambda b, i: (b, i + tile_off, 0)
    const2 = lambda b, i: (0, 0)
    glu_col = PF_GLU // (2 * CONV_WIDTH)
    in_specs = [
        pl.BlockSpec((None, tc, 2 * CONV_WIDTH), lambda b, i: (b, i + tile_off, glu_col)),
        pl.BlockSpec((None, CONF_HALO, 2 * CONV_WIDTH),
                     lambda b, i: (b, jnp.maximum((i + tile_off) * hb - 1, 0), glu_col)),
        pl.BlockSpec((None, CONF_HALO, 2 * CONV_WIDTH),
                     lambda b, i: (b, jnp.minimum((i + tile_off) * hb + hb, nhalo - 1), glu_col)),
        pl.BlockSpec((None, tc, CONV_WIDTH), lambda b, i: (b, i + tile_off, PF_CG // CONV_WIDTH)),
        pl.BlockSpec((CONV_KERNEL, SUBLANES, CONV_WIDTH), lambda b, i: (0, 0, 0)),
        pl.BlockSpec((SUBLANES, CONV_WIDTH), const2),
        pl.BlockSpec((1, CONV_WIDTH), const2),
        pl.BlockSpec((1, CONV_WIDTH), const2),
        pl.BlockSpec((None, CONV_WIDTH, CONV_WIDTH), lambda b, i: (layer, 0, 0)),
        pl.BlockSpec((1, CONV_WIDTH), const2),
    ]
    return pl.pallas_call(
        functools.partial(_conf_kernel, tile_off=tile_off, nct=nct, nt=nt),
        grid=(bsz, nt - tile_off),
        in_specs=in_specs,
        out_specs=pl.BlockSpec((None, tc, CONV_WIDTH), main),
        out_shape=jax.ShapeDtypeStruct((bsz, t, CONV_WIDTH), BF16),
        scratch_shapes=[
            pltpu.VMEM((SUBLANES, tc + 2 * CONF_HALO, CONV_WIDTH), F32),
            pltpu.VMEM((tc, CONV_WIDTH), F32),
        ],
        compiler_params=_cparams("parallel", "arbitrary"),
        name="conformer",
    )(pf, pf, pf, pf, dw_w, dw_b, ln_w, ln_b, pw_w16, pw_b)


def _outproj_kernel(c_ref, x_ref, os_ref, oa_ref, oc_ref, mod_ref, w_ref, fnw_ref, o_ref,
                    *, final, nct, tile_off):
    a0 = SSD_WIDTH
    c0 = SSD_WIDTH + ATT_WIDTH
    u = (_dot(os_ref[...], w_ref[0:a0, :]) + _dot(oa_ref[...], w_ref[a0:c0, :])
         + _dot(oc_ref[...], w_ref[c0:c0 + CONV_WIDTH, :]))
    x = x_ref[...]
    if tile_off < nct:
        x = jnp.where(pl.program_id(1) + tile_off < nct, c_ref[...], x)
    xn = x + mod_ref[2:3, :] * u
    if final:
        xn = xn * lax.rsqrt(jnp.mean(xn * xn, axis=-1, keepdims=True) + EPS) * fnw_ref[...]
    o_ref[...] = xn


def _outproj(csrc, xsrc, lat_off, o_ssd, o_att, o_conv, modsel, w_out16, layer, final_norm_w, nct, final):
    bsz, t, _ = o_ssd.shape
    d = xsrc.shape[-1]
    tm = ROW_TILE
    tile_off = nct if final else 0
    main = lambda b, i: (b, i + tile_off, 0)
    in_specs = _token_specs(tm, d, nct, lat_off, tile_off) + [
        pl.BlockSpec((None, tm, SSD_WIDTH), main),
        pl.BlockSpec((None, tm, ATT_WIDTH), main),
        pl.BlockSpec((None, tm, CONV_WIDTH), main),
        pl.BlockSpec((None, None, 3, d), lambda b, i: (b, jnp.where(i + tile_off >= nct, 1, 0), 0, 0)),
        pl.BlockSpec((None,) + w_out16.shape[1:], lambda b, i: (layer, 0, 0)),
        pl.BlockSpec((1, d), lambda b, i: (0, 0)),
    ]
    rows_out = t - tile_off * tm
    return pl.pallas_call(
        functools.partial(_outproj_kernel, final=final, nct=nct, tile_off=tile_off),
        grid=(bsz, rows_out // tm),
        in_specs=in_specs,
        out_specs=pl.BlockSpec((None, tm, d), lambda b, i: (b, i, 0)),
        out_shape=jax.ShapeDtypeStruct((bsz, rows_out, d), F32),
        compiler_params=_cparams("parallel", "arbitrary"),
        name="outproj_final" if final else "outproj",
    )(csrc, xsrc, o_ssd, o_att, o_conv, modsel, w_out16, final_norm_w)


def _mixout_kernel(c_ref, x_ref, os_ref, oa_ref, gm_ref, gp_ref, gn_ref, cg_ref, dww_ref, dwb_ref, lnw_ref, lnb_ref,
                   pww_ref, pwb_ref, mod_ref, w_ref, fnw_ref, o_ref, hbuf, abuf, *, final, nct, nt, tile_off):
    tile = pl.program_id(1) + tile_off
    first = (tile == 0) | (tile == nct)
    last = (tile == nct - 1) | (tile == nt - 1)
    a0 = SSD_WIDTH
    c0 = SSD_WIDTH + ATT_WIDTH
    u = _dot(os_ref[...], w_ref[0:a0, :]) + _dot(oa_ref[...], w_ref[a0:c0, :])
    oc = _conf_tile(gm_ref, gp_ref, gn_ref, cg_ref, dww_ref, dwb_ref, lnw_ref, lnb_ref, pww_ref, pwb_ref,
                    hbuf, abuf, first, last)
    u = u + _dot(oc, w_ref[c0:c0 + CONV_WIDTH, :])
    x = x_ref[...]
    if tile_off < nct:
        x = jnp.where(tile < nct, c_ref[...], x)
    xn = x + mod_ref[2:3, :] * u
    if final:
        xn = xn * lax.rsqrt(jnp.mean(xn * xn, axis=-1, keepdims=True) + EPS) * fnw_ref[...]
    o_ref[...] = xn


def _mixout(csrc, xsrc, lat_off, o_ssd, o_att, pf, dw_w, dw_b, ln_w, ln_b, pw_w16, pw_b, modsel, w_out16, layer,
            final_norm_w, ctx_len, final):
    bsz, t, _ = o_ssd.shape
    d = xsrc.shape[-1]
    tm = ROW_TILE
    nt = t // tm
    nct = ctx_len // tm
    tile_off = nct if final else 0
    hb = tm // CONF_HALO
    nhalo = t // CONF_HALO
    main = lambda b, i: (b, i + tile_off, 0)
    const2 = lambda b, i: (0, 0)
    glu_col = PF_GLU // (2 * CONV_WIDTH)
    in_specs = _token_specs(tm, d, nct, lat_off, tile_off) + [
        pl.BlockSpec((None, tm, SSD_WIDTH), main),
        pl.BlockSpec((None, tm, ATT_WIDTH), main),
        pl.BlockSpec((None, tm, 2 * CONV_WIDTH), lambda b, i: (b, i + tile_off, glu_col)),
        pl.BlockSpec((None, CONF_HALO, 2 * CONV_WIDTH),
                     lambda b, i: (b, jnp.maximum((i + tile_off) * hb - 1, 0), glu_col)),
        pl.BlockSpec((None, CONF_HALO, 2 * CONV_WIDTH),
                     lambda b, i: (b, jnp.minimum((i + tile_off) * hb + hb, nhalo - 1), glu_col)),
        pl.BlockSpec((None, tm, CONV_WIDTH), lambda b, i: (b, i + tile_off, PF_CG // CONV_WIDTH)),
        pl.BlockSpec((CONV_KERNEL, SUBLANES, CONV_WIDTH), lambda b, i: (0, 0, 0)),
        pl.BlockSpec((SUBLANES, CONV_WIDTH), const2),
        pl.BlockSpec((1, CONV_WIDTH), const2),
        pl.BlockSpec((1, CONV_WIDTH), const2),
        pl.BlockSpec((None, CONV_WIDTH, CONV_WIDTH), lambda b, i: (layer, 0, 0)),
        pl.BlockSpec((1, CONV_WIDTH), const2),
        pl.BlockSpec((None, None, 3, d), lambda b, i: (b, jnp.where(i + tile_off >= nct, 1, 0), 0, 0)),
        pl.BlockSpec((None,) + w_out16.shape[1:], lambda b, i: (layer, 0, 0)),
        pl.BlockSpec((1, d), const2),
    ]
    rows_out = t - tile_off * tm
    return pl.pallas_call(
        functools.partial(_mixout_kernel, final=final, nct=nct, nt=nt, tile_off=tile_off),
        grid=(bsz, rows_out // tm),
        in_specs=in_specs,
        out_specs=pl.BlockSpec((None, tm, d), lambda b, i: (b, i, 0)),
        out_shape=jax.ShapeDtypeStruct((bsz, rows_out, d), F32),
        scratch_shapes=[
            pltpu.VMEM((SUBLANES, tm + 2 * CONF_HALO, CONV_WIDTH), F32),
            pltpu.VMEM((tm, CONV_WIDTH), F32),
        ],
        compiler_params=_cparams("parallel", "arbitrary"),
        name="mixout_final" if final else "mixout",
    )(csrc, xsrc, o_ssd, o_att, pf, pf, pf, pf, dw_w, dw_b, ln_w, ln_b, pw_w16, pw_b, modsel, w_out16, final_norm_w)


def _rope_tables(seq, ctx_len):
    rows = seq // GRID_W
    row = jnp.repeat(jnp.arange(rows, dtype=F32), GRID_W)
    col = jnp.tile(jnp.arange(GRID_W, dtype=F32), rows)
    axis_dim = ATT_HEAD_DIM // 2
    inv_freq = ROPE_BASE ** (-jnp.arange(0, axis_dim, 2, dtype=F32) / axis_dim)
    ar = row[:, None] * inv_freq
    ac = col[:, None] * inv_freq
    cos = jnp.concatenate([jnp.cos(ar), jnp.cos(ar), jnp.cos(ac), jnp.cos(ac)], axis=1)
    sin = jnp.concatenate([-jnp.sin(ar), jnp.sin(ar), -jnp.sin(ac), jnp.sin(ac)], axis=1)
    reps = LANES // ATT_HEAD_DIM
    cos = jnp.concatenate([jnp.ones((ctx_len, LANES), F32), jnp.tile(cos, (1, reps))], axis=0)
    sin = jnp.concatenate([jnp.zeros((ctx_len, LANES), F32), jnp.tile(sin, (1, reps))], axis=0)
    return jnp.concatenate([cos, sin], axis=1)


def _head_lanes(a):
    rep = jnp.concatenate([a, a, a], axis=-1)
    return jnp.pad(rep, [(0, 0)] * (a.ndim - 1) + [(0, LANES - 3 * SSD_HEADS)])


def _pack_w_in(w_in):
    dt0 = SSD_XBC + SSD_WIDTH
    dt_w = w_in[..., dt0:dt0 + 2 * SSD_HEADS]
    packed = jnp.concatenate([w_in[..., :dt0], w_in[..., dt0 + 2 * SSD_HEADS:],
                              _head_lanes(dt_w[..., :SSD_HEADS]), _head_lanes(dt_w[..., SSD_HEADS:])], axis=-1)
    assert packed.shape[-1] == W_WIDTH
    return packed.astype(BF16)


def kernel(x, c, ctx, c_ctx, w_mod, b_mod, norm_w, w_in, ssd_conv_w, ssd_conv_b, ssd_dt_bias, ssd_a_log, ssd_d,
           ssd_norm_w, attn_sink, conv_dw_w, conv_dw_b, conv_ln_w, conv_ln_b, conv_pw_w, conv_pw_b, w_out,
           final_norm_w):
    bsz, seq, d = x.shape
    ctx_len = ctx.shape[1]
    depth = w_in.shape[0]
    assert ctx_len % ROW_TILE == 0 and seq % ROW_TILE == 0 and seq % GRID_W == 0
    assert ROW_TILE == ATT_TILE == CONF_TILE
    nct = ctx_len // ROW_TILE

    cond_rows = -(-(bsz + 1) // SUBLANES) * SUBLANES
    cond = jnp.zeros((cond_rows, d), F32).at[:bsz].set(c).at[bsz].set(c_ctx)
    mod = _modulation(cond, w_mod, b_mod)
    rope_t = _rope_tables(seq, ctx_len)
    csrc, xsrc, lat_off = ctx, x, 0
    w_in16 = _pack_w_in(w_in)
    w_out16 = w_out.astype(BF16)
    pw_w16 = conv_pw_w.astype(BF16)
    norm_w3 = norm_w[:, None, :]

    out = None
    for l in range(depth):
        final = l == depth - 1
        lat = mod[l, :bsz]
        ctxm = jnp.broadcast_to(mod[l, bsz], lat.shape)
        modsel = jnp.stack([ctxm, lat], axis=1).reshape(bsz, 2, 3, d)
        xbc, pf, pb = _inproj(csrc, xsrc, lat_off, ctx_len + seq, modsel, norm_w3, rope_t, w_in16, l, nct)

        o_ssd = _ssd(xbc, pf, ssd_conv_w[l], ssd_conv_b[l][None], _head_lanes(ssd_dt_bias[l])[:, None],
                     _head_lanes(ssd_a_log[l])[:, None], jnp.repeat(ssd_d[l], SSD_HEAD_DIM)[None],
                     ssd_norm_w[l][None], ctx_len)
        tile_off = nct if final else 0
        sink_tab = jnp.broadcast_to(attn_sink[l][:, None] * LOG2E, (ATT_HEADS, LANES))
        o_att = _attention(pf, pb, sink_tab, ctx_len, tile_off)
        dw_w = jnp.broadcast_to(conv_dw_w[l][:, None, :], (CONV_KERNEL, SUBLANES, CONV_WIDTH))
        dw_b = jnp.broadcast_to(conv_dw_b[l][None], (SUBLANES, CONV_WIDTH))
        out = _mixout(csrc, xsrc, lat_off, o_ssd, o_att, pf, dw_w, dw_b, conv_ln_w[l][None], conv_ln_b[l][None], pw_w16,
                      conv_pw_b[l][None], modsel, w_out16, l, final_norm_w[None], ctx_len, final)
        csrc, xsrc, lat_off = out, out, nct
    return out
```

```python
import functools

import jax
import jax.numpy as jnp
from jax import lax
from jax.experimental import pallas as pl
from jax.experimental.pallas import tpu as pltpu

F32 = jnp.float32
BF16 = jnp.bfloat16

GRID_W = 64
EPS = 1e-6

SSD_HEADS = 16
SSD_HEAD_DIM = 64
SSD_WIDTH = SSD_HEADS * SSD_HEAD_DIM
SSD_GROUPS = 2
SSD_STATE = 128
SSD_CONV = 5
SSD_CHUNK = 128
SSD_GN = SSD_GROUPS * SSD_STATE
SSD_XBC = SSD_WIDTH + 2 * SSD_GN
GROUP_COLS = SSD_WIDTH // SSD_GROUPS

ATT_HEADS = 8
ATT_KV_HEADS = 2
ATT_HEAD_DIM = 64
ATT_WIDTH = ATT_HEADS * ATT_HEAD_DIM
ATT_KV_WIDTH = ATT_KV_HEADS * ATT_HEAD_DIM
WINDOW = 128
ROPE_BASE = 10000.0
LOG2E = 1.4426950408889634
ATT_SCALE = ATT_HEAD_DIM ** -0.5 * LOG2E

CONV_WIDTH = 512
CONV_KERNEL = 31
CONV_PAD = (CONV_KERNEL - 1) // 2

LANES = 128

PF_Z = 0
PF_GLU = PF_Z + SSD_WIDTH
PF_AG = PF_GLU + 2 * CONV_WIDTH
PF_CG = PF_AG + ATT_WIDTH
PF_DT = PF_CG + CONV_WIDTH
PF_WIDTH = PF_DT + 2 * LANES
PB_Q = 0
PB_KV = PB_Q + ATT_WIDTH
PB_WIDTH = PB_KV + 2 * ATT_KV_WIDTH

W_XBC = 0
W_Z = W_XBC + SSD_XBC
W_Q = W_Z + SSD_WIDTH
W_K = W_Q + ATT_WIDTH
W_V = W_K + ATT_KV_WIDTH
W_AG = W_V + ATT_KV_WIDTH
W_GLU = W_AG + ATT_WIDTH
W_CG = W_GLU + 2 * CONV_WIDTH
W_DT = W_CG + CONV_WIDTH
W_WIDTH = W_DT + 2 * LANES
SUBLANES = 8
VMEM_LIMIT = 56 * 1024 * 1024

ROW_TILE = 256
ATT_TILE = 256
ATT_QBLOCK = 256
CONF_TILE = 256
CONF_HALO = 16
CONF_SUB = 32
SSD_HALO = SUBLANES
SSD_STEP = 2 * SSD_CHUNK


def _dot(a, b):
    return jnp.dot(a, b, preferred_element_type=F32)


def _dot_nt(a, b):
    return lax.dot_general(a, b, (((1,), (1,)), ((), ())), preferred_element_type=F32)


def _split3(x):
    hi = x.astype(BF16)
    r1 = x - hi.astype(F32)
    mid = r1.astype(BF16)
    lo = (r1 - mid.astype(F32)).astype(BF16)
    return hi, mid, lo


def _pack3(x, width):
    hi, mid, lo = _split3(x)
    lane = lax.broadcasted_iota(jnp.int32, x.shape, 1)
    return jnp.where(lane < width, hi, jnp.where(lane < 2 * width, mid, lo))


def _shift_rows(x, j):
    rows, w = x.shape
    g = x.reshape(rows // SUBLANES, SUBLANES, w)
    r = pltpu.roll(g, SUBLANES - j, 1)
    sub = lax.broadcasted_iota(jnp.int32, (1, SUBLANES, w), 1)
    return jnp.where(sub < SUBLANES - j, r[:-1], r[1:]).reshape(rows - SUBLANES, w)


def _cparams(*sem, flags=None):
    return pltpu.CompilerParams(dimension_semantics=sem, vmem_limit_bytes=VMEM_LIMIT, flags=flags)


def _mod_kernel(c_ref, w_ref, b_ref, o_ref):
    a = jax.nn.silu(c_ref[...]).astype(BF16)
    o_ref[...] = _dot(a, w_ref[...].astype(BF16)) + b_ref[...]


def _modulation(cond, w_mod, b_mod):
    depth, d, n = w_mod.shape
    rows = cond.shape[0]
    tn = n // 2 if (n // 2) % LANES == 0 else n
    return pl.pallas_call(
        _mod_kernel,
        grid=(depth, n // tn),
        in_specs=[
            pl.BlockSpec((rows, d), lambda l, j: (0, 0)),
            pl.BlockSpec((None, d, tn), lambda l, j: (l, 0, j)),
            pl.BlockSpec((None, 1, tn), lambda l, j: (l, 0, j)),
        ],
        out_specs=pl.BlockSpec((None, rows, tn), lambda l, j: (l, 0, j)),
        out_shape=jax.ShapeDtypeStruct((depth, rows, n), F32),
        compiler_params=_cparams("arbitrary", "arbitrary"),
        name="modulation",
    )(cond, w_mod, b_mod.reshape(depth, 1, n))


def _rope(t, cos, sin):
    width = t.shape[-1]
    lane = lax.broadcasted_iota(jnp.int32, t.shape, 1)
    first = (lane % 32) < 16
    rot = jnp.where(first, pltpu.roll(t, width - 16, 1), pltpu.roll(t, 16, 1))
    return t * cos + rot * sin


def _inproj_kernel(c_ref, x_ref, mod_ref, nw_ref, rope_ref, w_ref, xbc_ref, pf_ref, pb_ref, *, nct):
    wxbc = w_ref.at[:, W_XBC:W_XBC + SSD_XBC]
    wz = w_ref.at[:, W_Z:W_Z + SSD_WIDTH]
    wq = w_ref.at[:, W_Q:W_Q + ATT_WIDTH]
    wk = w_ref.at[:, W_K:W_K + ATT_KV_WIDTH]
    wv = w_ref.at[:, W_V:W_V + ATT_KV_WIDTH]
    wag = w_ref.at[:, W_AG:W_AG + ATT_WIDTH]
    wglu = w_ref.at[:, W_GLU:W_GLU + 2 * CONV_WIDTH]
    wcg = w_ref.at[:, W_CG:W_CG + CONV_WIDTH]
    wdt = w_ref.at[:, W_DT:W_DT + 2 * LANES]
    x = jnp.where(pl.program_id(1) < nct, c_ref[...], x_ref[...])
    y = x * lax.rsqrt(jnp.mean(x * x, axis=-1, keepdims=True) + EPS) * nw_ref[...]
    h = (y * (1.0 + mod_ref[1:2, :]) + mod_ref[0:1, :]).astype(BF16)
    cos = rope_ref[:, :LANES]
    sin = rope_ref[:, LANES:]
    reps = ATT_WIDTH // LANES
    cos_q = jnp.concatenate([cos] * reps, axis=1)
    sin_q = jnp.concatenate([sin] * reps, axis=1)
    pb_ref[:, PB_Q:PB_Q + ATT_WIDTH] = (_rope(_dot(h, wq[...]), cos_q, sin_q) * ATT_SCALE).astype(BF16)
    pb_ref[:, PB_KV:PB_KV + ATT_KV_WIDTH] = _rope(_dot(h, wk[...]), cos, sin).astype(BF16)
    pb_ref[:, PB_KV + ATT_KV_WIDTH:PB_KV + 2 * ATT_KV_WIDTH] = _dot(h, wv[...]).astype(BF16)
    pf_ref[:, PF_Z:PF_Z + SSD_WIDTH] = jax.nn.silu(_dot(h, wz[...]))
    pf_ref[:, PF_AG:PF_AG + ATT_WIDTH] = jax.nn.silu(_dot(h, wag[...]))
    pf_ref[:, PF_CG:PF_CG + CONV_WIDTH] = jax.nn.silu(_dot(h, wcg[...]))
    glu = _dot(h, wglu[...])
    pf_ref[:, PF_GLU:PF_GLU + CONV_WIDTH] = glu[:, :CONV_WIDTH]
    pf_ref[:, PF_GLU + CONV_WIDTH:PF_GLU + 2 * CONV_WIDTH] = jax.nn.sigmoid(glu[:, CONV_WIDTH:])
    pf_ref[:, PF_DT:PF_DT + 2 * LANES] = _dot(h, wdt[...])
    xbc_ref[...] = _dot(h, wxbc[...])


def _token_specs(tm, d, nct, lat_off, tile_off):
    return [
        pl.BlockSpec((None, tm, d), lambda b, i: (b, jnp.minimum(i + tile_off, nct - 1), 0)),
        pl.BlockSpec((None, tm, d), lambda b, i: (b, jnp.maximum(i + tile_off - nct, 0) + lat_off, 0)),
    ]


def _inproj(csrc, xsrc, lat_off, t, modsel, norm_w, rope_t, w_packed, layer, nct):
    bsz, _, d = xsrc.shape
    tm = ROW_TILE
    row_map = lambda b, i: (b, i, 0)
    in_specs = _token_specs(tm, d, nct, lat_off, 0) + [
        pl.BlockSpec((None, None, 3, d), lambda b, i: (b, jnp.where(i >= nct, 1, 0), 0, 0)),
        pl.BlockSpec((None, 1, d), lambda b, i: (layer, 0, 0)),
        pl.BlockSpec((tm, 2 * LANES), lambda b, i: (i, 0)),
        pl.BlockSpec((None, d, W_WIDTH), lambda b, i: (layer, 0, 0)),
    ]
    widths = (SSD_XBC, PF_WIDTH, PB_WIDTH)
    return pl.pallas_call(
        functools.partial(_inproj_kernel, nct=nct),
        grid=(bsz, t // tm),
        in_specs=in_specs,
        out_specs=[pl.BlockSpec((None, tm, w), row_map) for w in widths],
        out_shape=[jax.ShapeDtypeStruct((bsz, t, w), dt) for w, dt in zip(widths, (F32, F32, BF16))],
        compiler_params=_cparams("parallel", "arbitrary"),
        name="inproj",
    )(csrc, xsrc, modsel, norm_w, rope_t, w_packed)


def _ssd_prep(xs, bc, dt_raw, dtb, alog, fwd):
    q = SSD_CHUNK
    bb = bc[:, :SSD_GN]
    cb16 = bc[:, SSD_GN:]

    groups = [slice(g * SSD_STATE, (g + 1) * SSD_STATE) for g in range(SSD_GROUPS)]
    scores_g = [_dot_nt(cb16[:, gs], bb[:, gs]) for gs in groups]

    lane = lax.broadcasted_iota(jnp.int32, (q, LANES), 1)
    head_lane = lane < 3 * SSD_HEADS
    dt = jnp.where(head_lane, jax.nn.softplus(dt_raw + dtb), 0.0)
    da = dt * (-jnp.exp(alog))
    li = lax.broadcasted_iota(jnp.int32, (q, q), 0)
    si = lax.broadcasted_iota(jnp.int32, (q, q), 1)
    causal = (si <= li) if fwd else (si >= li)
    ones = jnp.where(causal, 1.0, 0.0).astype(BF16)
    da_hi, da_mid, da_lo = _split3(da)
    cs = _dot(ones, da_hi) + _dot(ones, da_mid) + _dot(ones, da_lo)
    total = jnp.sum(da, axis=0, keepdims=True)
    to_end = jnp.exp(total - cs)
    from_start = jnp.where(head_lane, jnp.exp(cs), 0.0)
    chunk_decay = jnp.where(head_lane[:SUBLANES], jnp.exp(total), 0.0)
    cs_t = cs.T

    er = lax.broadcasted_iota(jnp.int32, (LANES, SSD_WIDTH), 0)
    ec = lax.broadcasted_iota(jnp.int32, (LANES, SSD_WIDTH), 1)
    spread = jnp.where((ec // SSD_HEAD_DIM == er % SSD_HEADS) & (er < 3 * SSD_HEADS), 1.0, 0.0).astype(BF16)
    dt_e = _dot(_pack3(dt, SSD_HEADS), spread)
    stacked = jnp.concatenate([dt * to_end, from_start, chunk_decay], axis=0)
    stacked_e = _dot(_pack3(stacked, SSD_HEADS), spread)
    w_e = stacked_e[:q]
    fs_e = stacked_e[q:2 * q]
    cd_e = stacked_e[2 * q:2 * q + 1]

    xd = (xs * dt_e).astype(BF16)
    xdw = (xs * w_e).astype(BF16)
    low = lax.broadcasted_iota(jnp.int32, (q, LANES), 1) < SSD_HEAD_DIM
    zero16 = jnp.zeros((q, LANES), BF16)
    y_diag, new_states = [], []
    for g in range(SSD_GROUPS):
        gs = groups[g]
        gc = slice(g * GROUP_COLS, (g + 1) * GROUP_COLS)
        scores = scores_g[g]
        new_states.append(_dot(bb[:, gs].astype(F32).T.astype(BF16), xdw[:, gc]))
        for pr in range(GROUP_COLS // LANES):
            h0 = (g * GROUP_COLS + pr * LANES) // SSD_HEAD_DIM
            cols = slice(h0 * SSD_HEAD_DIM, h0 * SSD_HEAD_DIM + LANES)
            xpair = xd[:, cols]
            m0 = (scores * jnp.exp(jnp.where(causal, cs[:, h0:h0 + 1] - cs_t[h0:h0 + 1, :], -jnp.inf))).astype(BF16)
            m1 = (scores * jnp.exp(jnp.where(causal, cs[:, h0 + 1:h0 + 2] - cs_t[h0 + 1:h0 + 2, :], -jnp.inf))).astype(BF16)
            block_diag = jnp.concatenate([jnp.where(low, xpair, zero16), jnp.where(low, zero16, xpair)], axis=0)
            y_diag.append(_dot(jnp.concatenate([m0, m1], axis=1), block_diag))
    return y_diag, new_states, cb16, fs_e, cd_e


def _ssd_finish(prep, st_ref):
    y_diag, new_states, cb16, fs_e, cd_e = prep
    ys = []
    for g in range(SSD_GROUPS):
        gs = slice(g * SSD_STATE, (g + 1) * SSD_STATE)
        gc = slice(g * GROUP_COLS, (g + 1) * GROUP_COLS)
        entering = st_ref[g]
        y_off = _dot(cb16[:, gs], entering.astype(BF16)) * fs_e[:, gc]
        st_ref[g] = entering * cd_e[:, gc] + new_states[g]
        per_group = GROUP_COLS // LANES
        for pr in range(per_group):
            ys.append(y_diag[g * per_group + pr] + y_off[:, pr * LANES:(pr + 1) * LANES])
    return jnp.concatenate(ys, axis=1)


def _ssd_bwd_block(k, ncb, nb):
    return jnp.where(k < ncb, ncb - 1 - k, nb + ncb - 1 - k)


def _ssd_kernel(xm_ref, xp_ref, xn_ref, dt_ref, z_ref, cw_ref, cb_ref, dtb_ref, alog_ref, dvec_ref, nw_ref,
                o_ref, yb_ref, st_ref, xs_ref, bc_ref, *, ncb, nb):
    q = SSD_CHUNK
    rows = SSD_STEP
    n_sub = rows // q
    s = pl.program_id(1)

    def conv_block(cb):
        first = (cb == 0) | (cb == ncb)
        last = (cb == ncb - 1) | (cb == nb - 1)
        xext = jnp.concatenate([jnp.where(first, 0.0, xp_ref[...]), xm_ref[...],
                                jnp.where(last, 0.0, xn_ref[...])], axis=0)
        acc = cb_ref[...]
        for k in range(SSD_CONV):
            d = k - (SSD_CONV - 1) // 2
            if d == 0:
                tap = xext[SSD_HALO:SSD_HALO + rows]
            elif d > 0:
                tap = _shift_rows(xext, d)[SSD_HALO:SSD_HALO + rows]
            else:
                tap = _shift_rows(xext, SSD_HALO + d)[:rows]
            acc = acc + cw_ref[k:k + 1, :] * tap
        u = jax.nn.silu(acc)
        dst = pl.ds(pl.multiple_of(cb * rows, rows), rows)
        xs_ref[dst, :] = u[:, :SSD_WIDTH]
        bc_ref[dst, :] = u[:, SSD_WIDTH:].astype(BF16)

    def chunk_rows(blk, j):
        return pl.ds(pl.multiple_of(blk * rows + j * q, q), q)

    @pl.when((s == 0) | (s == nb))
    def _():
        st_ref[...] = jnp.zeros_like(st_ref)

    @pl.when(s <= nb - 1)
    def _():
        conv_block(_ssd_bwd_block(s, ncb, nb))

    @pl.when(s < nb)
    def _():
        blk = _ssd_bwd_block(s, ncb, nb)
        order = tuple(reversed(range(n_sub)))
        preps = [_ssd_prep(xs_ref[chunk_rows(blk, j), :], bc_ref[chunk_rows(blk, j), :], dt_ref[j * q:(j + 1) * q, :],
                           dtb_ref[...], alog_ref[...], False) for j in order]
        for j, prep in zip(order, preps):
            yb_ref[chunk_rows(blk, j), :] = _ssd_finish(prep, st_ref)

    @pl.when(s >= nb)
    def _():
        blk = s - nb
        preps = [_ssd_prep(xs_ref[chunk_rows(blk, j), :], bc_ref[chunk_rows(blk, j), :], dt_ref[j * q:(j + 1) * q, :],
                           dtb_ref[...], alog_ref[...], True) for j in range(n_sub)]
        for j, prep in enumerate(preps):
            y = _ssd_finish(prep, st_ref)
            tot = y + yb_ref[chunk_rows(blk, j), :] + dvec_ref[...] * xs_ref[chunk_rows(blk, j), :]
            gated = tot * z_ref[j * q:(j + 1) * q, :]
            outs = []
            for g in range(SSD_GROUPS):
                gc = slice(g * GROUP_COLS, (g + 1) * GROUP_COLS)
                v = gated[:, gc]
                outs.append(v * lax.rsqrt(jnp.mean(v * v, axis=-1, keepdims=True) + EPS) * nw_ref[:, gc])
            o_ref[j * q:(j + 1) * q, :] = jnp.concatenate(outs, axis=1).astype(BF16)


def _ssd(xbc, pf, conv_w, conv_b, dt_bias, a_log, dvec, norm_w, ctx_len):
    bsz, t, _ = pf.shape
    rows = SSD_STEP
    nb = t // rows
    ncb = ctx_len // rows
    hb = rows // SSD_HALO
    nhalo = t // SSD_HALO
    const2 = lambda b, s: (0, 0)
    xblk = lambda s: _ssd_bwd_block(jnp.minimum(s, nb - 1), ncb, nb)
    cur = lambda s: jnp.where(s < nb, _ssd_bwd_block(s, ncb, nb), s - nb)
    phase = lambda s: jnp.where(s < nb, 1, 0)
    fwd_blk = lambda s: jnp.maximum(s - nb, 0)
    in_specs = [
        pl.BlockSpec((None, rows, SSD_XBC), lambda b, s: (b, xblk(s), 0)),
        pl.BlockSpec((None, SSD_HALO, SSD_XBC), lambda b, s: (b, jnp.maximum(xblk(s) * hb - 1, 0), 0)),
        pl.BlockSpec((None, SSD_HALO, SSD_XBC), lambda b, s: (b, jnp.minimum(xblk(s) * hb + hb, nhalo - 1), 0)),
        pl.BlockSpec((None, rows, LANES), lambda b, s: (b, cur(s), PF_DT // LANES + phase(s))),
        pl.BlockSpec((None, rows, SSD_WIDTH), lambda b, s: (b, fwd_blk(s), PF_Z // SSD_WIDTH)),
        pl.BlockSpec((SSD_CONV, SSD_XBC), const2),
        pl.BlockSpec((1, SSD_XBC), const2),
        pl.BlockSpec((None, 1, LANES), lambda b, s: (phase(s), 0, 0)),
        pl.BlockSpec((None, 1, LANES), lambda b, s: (phase(s), 0, 0)),
        pl.BlockSpec((1, SSD_WIDTH), const2),
        pl.BlockSpec((1, SSD_WIDTH), const2),
    ]
    return pl.pallas_call(
        functools.partial(_ssd_kernel, ncb=ncb, nb=nb),
        grid=(bsz, 2 * nb),
        in_specs=in_specs,
        out_specs=pl.BlockSpec((None, rows, SSD_WIDTH), lambda b, s: (b, fwd_blk(s), 0)),
        out_shape=jax.ShapeDtypeStruct((bsz, t, SSD_WIDTH), BF16),
        scratch_shapes=[
            pltpu.VMEM((t, SSD_WIDTH), F32),
            pltpu.VMEM((SSD_GROUPS, SSD_STATE, GROUP_COLS), F32),
            pltpu.VMEM((t, SSD_WIDTH), F32),
            pltpu.VMEM((t, 2 * SSD_GN), BF16),
        ],
        compiler_params=_cparams("parallel", "arbitrary"),
        name="ssd",
    )(xbc, xbc, xbc, pf, pf, conv_w, conv_b, dt_bias, a_log, dvec, norm_w)


def _swap_halves(blk):
    half = blk.shape[-1] // 2
    return jnp.concatenate([blk[:, half:], blk[:, :half]], axis=1)


def _head_pads(blk, g, low):
    zero = jnp.zeros_like(blk)
    sw = _swap_halves(blk)
    if g == 0:
        return jnp.where(low, blk, zero), jnp.where(low, zero, sw)
    return jnp.where(low, sw, zero), jnp.where(low, zero, blk)


def _attn_kernel(q_ref, kvm_ref, kvp_ref, kvn_ref, kvc_ref, ag_ref, sink_ref, o_ref, pad_ref,
                 *, tile_off, nct, seq, ctx_len):
    qb = ATT_TILE
    blk = ATT_QBLOCK
    wk = blk + 2 * WINDOW
    tile = pl.program_id(1) + tile_off
    lat0 = (tile - nct) * qb
    low_k = lax.broadcasted_iota(jnp.int32, (1, LANES), 1) < ATT_HEAD_DIM

    kv = jnp.concatenate([kvc_ref[...], kvp_ref[...], kvm_ref[...], kvn_ref[...]], axis=0)
    for g in range(ATT_KV_HEADS):
        ka, kb = _head_pads(kv[:, :ATT_KV_WIDTH], g, low_k)
        va, vb = _head_pads(kv[:, ATT_KV_WIDTH:], g, low_k)
        pad_ref[g, 0] = ka
        pad_ref[g, 1] = kb
        pad_ref[g, 2] = va
        pad_ref[g, 3] = vb

    r = lax.broadcasted_iota(jnp.int32, (blk, wk), 0)
    col = lax.broadcasted_iota(jnp.int32, (blk, wk), 1)
    in_band = jnp.abs(col - WINDOW - r) <= WINDOW
    neg = jnp.finfo(F32).min
    rep = ATT_HEADS // ATT_KV_HEADS
    pairs_per_group = rep * ATT_HEAD_DIM // LANES
    for a in range(qb // blk):
        rows = slice(a * blk, (a + 1) * blk)
        kpos = lat0 + a * blk - WINDOW + col
        valid = in_band & (kpos >= 0) & (kpos < seq) & (lat0 >= 0)
        win0 = ctx_len + a * blk
        def operand(g, slot):
            if a == 0:
                return pad_ref[g, slot, 0:win0 + wk, :]
            return jnp.concatenate([pad_ref[g, slot, 0:ctx_len, :], pad_ref[g, slot, win0:win0 + wk, :]], axis=0)

        for g in range(ATT_KV_HEADS):
            for pr in range(pairs_per_group):
                p = g * pairs_per_group + pr
                qp = q_ref[rows, p * LANES:(p + 1) * LANES]
                acc = jnp.zeros((blk, LANES), F32)
                invs = []
                for j in range(2):
                    head = 2 * p + j
                    sink = sink_ref[head:head + 1, 0:1]
                    s = _dot_nt(qp, operand(g, j))
                    s = jnp.concatenate([s[:, :ctx_len], jnp.where(valid, s[:, ctx_len:], neg)], axis=1)
                    m = jnp.maximum(jnp.max(s, axis=-1, keepdims=True), sink)
                    pexp = jnp.exp2(s - m)
                    den = jnp.sum(pexp, axis=-1, keepdims=True) + jnp.exp2(sink - m)
                    acc = acc + _dot(pexp.astype(BF16), operand(g, 2 + j))
                    invs.append(1.0 / den)
                gate = ag_ref[rows, p * LANES:(p + 1) * LANES]
                o_ref[rows, p * LANES:(p + 1) * LANES] = (acc * jnp.where(low_k, invs[0], invs[1]) * gate).astype(BF16)


def _attention(pf, pb, sink_tab, ctx_len, tile_off):
    bsz, t, _ = pf.shape
    qb = ATT_TILE
    nct = ctx_len // qb
    hb = qb // WINDOW
    ctx_hb = ctx_len // WINDOW
    n_hb = t // WINDOW
    ntiles = t // qb - tile_off
    kvw = 2 * ATT_KV_WIDTH
    kv_col = PB_KV // kvw
    in_specs = [
        pl.BlockSpec((None, qb, ATT_WIDTH), lambda b, i: (b, i + tile_off, PB_Q // ATT_WIDTH)),
        pl.BlockSpec((None, qb, kvw), lambda b, i: (b, i + tile_off, kv_col)),
        pl.BlockSpec((None, WINDOW, kvw), lambda b, i: (b, jnp.maximum((i + tile_off) * hb - 1, ctx_hb), kv_col)),
        pl.BlockSpec((None, WINDOW, kvw), lambda b, i: (b, jnp.minimum((i + tile_off) * hb + hb, n_hb - 1), kv_col)),
        pl.BlockSpec((None, ctx_len, kvw), lambda b, i: (b, 0, kv_col)),
        pl.BlockSpec((None, qb, ATT_WIDTH), lambda b, i: (b, i + tile_off, PF_AG // ATT_WIDTH)),
        pl.BlockSpec((ATT_HEADS, LANES), lambda b, i: (0, 0)),
    ]
    return pl.pallas_call(
        functools.partial(_attn_kernel, tile_off=tile_off, nct=nct, seq=t - ctx_len, ctx_len=ctx_len),
        grid=(bsz, ntiles),
        in_specs=in_specs,
        out_specs=pl.BlockSpec((None, qb, ATT_WIDTH), lambda b, i: (b, i + tile_off, 0)),
        out_shape=jax.ShapeDtypeStruct((bsz, t, ATT_WIDTH), BF16),
        scratch_shapes=[pltpu.VMEM((ATT_KV_HEADS, 4, ctx_len + qb + 2 * WINDOW, LANES), BF16)],
        compiler_params=_cparams("parallel", "arbitrary"),
        name="attention",
    )(pb, pb, pb, pb, pb, pf, sink_tab)


def _conf_tile(gm_ref, gp_ref, gn_ref, cg_ref, dww_ref, dwb_ref, lnw_ref, lnb_ref, pww_ref, pwb_ref, hbuf, abuf,
               first, last):
    tc = CONF_TILE

    def glu(v):
        return v[:, :CONV_WIDTH] * v[:, CONV_WIDTH:]

    rows = tc + 2 * CONF_HALO
    ext = jnp.concatenate([jnp.where(first, 0.0, glu(gp_ref[...])), glu(gm_ref[...]),
                           jnp.where(last, 0.0, glu(gn_ref[...]))], axis=0)
    hbuf[0] = ext
    for j in range(1, SUBLANES):
        hbuf[j, 0:rows - SUBLANES, :] = _shift_rows(ext, j)
    base = CONF_HALO - CONV_PAD
    for r in range(tc // CONF_SUB):
        acc = jnp.broadcast_to(dwb_ref[...], (CONF_SUB // SUBLANES, SUBLANES, CONV_WIDTH))
        for k in range(CONV_KERNEL):
            blk, res = divmod(base + k, SUBLANES)
            strip = hbuf[res, pl.ds(r * CONF_SUB + blk * SUBLANES, CONF_SUB), :]
            acc = acc + dww_ref[k] * strip.reshape(CONF_SUB // SUBLANES, SUBLANES, CONV_WIDTH)
        acc = acc.reshape(CONF_SUB, CONV_WIDTH)
        xc = acc - jnp.mean(acc, axis=-1, keepdims=True)
        var = jnp.mean(xc * xc, axis=-1, keepdims=True)
        ln = xc * lax.rsqrt(var + EPS) * lnw_ref[...] + lnb_ref[...]
        abuf[pl.ds(r * CONF_SUB, CONF_SUB), :] = jax.nn.silu(ln)
    out = _dot(abuf[...].astype(BF16), pww_ref[...]) + pwb_ref[...]
    return (out * cg_ref[...]).astype(BF16)


def _conf_kernel(gm_ref, gp_ref, gn_ref, cg_ref, dww_ref, dwb_ref, lnw_ref, lnb_ref, pww_ref, pwb_ref, o_ref,
                 hbuf, abuf, *, tile_off, nct, nt):
    tile = pl.program_id(1) + tile_off
    first = (tile == 0) | (tile == nct)
    last = (tile == nct - 1) | (tile == nt - 1)
    o_ref[...] = _conf_tile(gm_ref, gp_ref, gn_ref, cg_ref, dww_ref, dwb_ref, lnw_ref, lnb_ref, pww_ref, pwb_ref,
                            hbuf, abuf, first, last)


def _conformer(pf, dw_w, dw_b, ln_w, ln_b, pw_w16, layer, pw_b, ctx_len, tile_off):
    bsz, t, _ = pf.shape
    tc = CONF_TILE
    nt = t // tc
    nct = ctx_len // tc
    hb = tc // CONF_HALO
    nhalo = t // CONF_HALO
    main = lambda b, i: (b, i + tile_off, 0)
    const2 = lambda b, i: (0, 0)
    glu_col = PF_GLU // (2 * CONV_WIDTH)
    in_specs = [
        pl.BlockSpec((None, tc, 2 * CONV_WIDTH), lambda b, i: (b, i + tile_off, glu_col)),
        pl.BlockSpec((None, CONF_HALO, 2 * CONV_WIDTH),
                     lambda b, i: (b, jnp.maximum((i + tile_off) * hb - 1, 0), glu_col)),
        pl.BlockSpec((None, CONF_HALO, 2 * CONV_WIDTH),
                     lambda b, i: (b, jnp.minimum((i + tile_off) * hb + hb, nhalo - 1), glu_col)),
        pl.BlockSpec((None, tc, CONV_WIDTH), lambda b, i: (b, i + tile_off, PF_CG // CONV_WIDTH)),
        pl.BlockSpec((CONV_KERNEL, SUBLANES, CONV_WIDTH), lambda b, i: (0, 0, 0)),
        pl.BlockSpec((SUBLANES, CONV_WIDTH), const2),
        pl.BlockSpec((1, CONV_WIDTH), const2),
        pl.BlockSpec((1, CONV_WIDTH), const2),
        pl.BlockSpec((None, CONV_WIDTH, CONV_WIDTH), lambda b, i: (layer, 0, 0)),
        pl.BlockSpec((1, CONV_WIDTH), const2),
    ]
    return pl.pallas_call(
        functools.partial(_conf_kernel, tile_off=tile_off, nct=nct, nt=nt),
        grid=(bsz, nt - tile_off),
        in_specs=in_specs,
        out_specs=pl.BlockSpec((None, tc, CONV_WIDTH), main),
        out_shape=jax.ShapeDtypeStruct((bsz, t, CONV_WIDTH), BF16),
        scratch_shapes=[
            pltpu.VMEM((SUBLANES, tc + 2 * CONF_HALO, CONV_WIDTH), F32),
            pltpu.VMEM((tc, CONV_WIDTH), F32),
        ],
        compiler_params=_cparams("parallel", "arbitrary"),
        name="conformer",
    )(pf, pf, pf, pf, dw_w, dw_b, ln_w, ln_b, pw_w16, pw_b)


def _outproj_kernel(c_ref, x_ref, os_ref, oa_ref, oc_ref, mod_ref, w_ref, fnw_ref, o_ref,
                    *, final, nct, tile_off):
    a0 = SSD_WIDTH
    c0 = SSD_WIDTH + ATT_WIDTH
    u = (_dot(os_ref[...], w_ref[0:a0, :]) + _dot(oa_ref[...], w_ref[a0:c0, :])
         + _dot(oc_ref[...], w_ref[c0:c0 + CONV_WIDTH, :]))
    x = x_ref[...]
    if tile_off < nct:
        x = jnp.where(pl.program_id(1) + tile_off < nct, c_ref[...], x)
    xn = x + mod_ref[2:3, :] * u
    if final:
        xn = xn * lax.rsqrt(jnp.mean(xn * xn, axis=-1, keepdims=True) + EPS) * fnw_ref[...]
    o_ref[...] = xn


def _outproj(csrc, xsrc, lat_off, o_ssd, o_att, o_conv, modsel, w_out16, layer, final_norm_w, nct, final):
    bsz, t, _ = o_ssd.shape
    d = xsrc.shape[-1]
    tm = ROW_TILE
    tile_off = nct if final else 0
    main = lambda b, i: (b, i + tile_off, 0)
    in_specs = _token_specs(tm, d, nct, lat_off, tile_off) + [
        pl.BlockSpec((None, tm, SSD_WIDTH), main),
        pl.BlockSpec((None, tm, ATT_WIDTH), main),
        pl.BlockSpec((None, tm, CONV_WIDTH), main),
        pl.BlockSpec((None, None, 3, d), lambda b, i: (b, jnp.where(i + tile_off >= nct, 1, 0), 0, 0)),
        pl.BlockSpec((None,) + w_out16.shape[1:], lambda b, i: (layer, 0, 0)),
        pl.BlockSpec((1, d), lambda b, i: (0, 0)),
    ]
    rows_out = t - tile_off * tm
    return pl.pallas_call(
        functools.partial(_outproj_kernel, final=final, nct=nct, tile_off=tile_off),
        grid=(bsz, rows_out // tm),
        in_specs=in_specs,
        out_specs=pl.BlockSpec((None, tm, d), lambda b, i: (b, i, 0)),
        out_shape=jax.ShapeDtypeStruct((bsz, rows_out, d), F32),
        compiler_params=_cparams("parallel", "arbitrary"),
        name="outproj_final" if final else "outproj",
    )(csrc, xsrc, o_ssd, o_att, o_conv, modsel, w_out16, final_norm_w)


def _mixout_kernel(c_ref, x_ref, os_ref, oa_ref, gm_ref, gp_ref, gn_ref, cg_ref, dww_ref, dwb_ref, lnw_ref, lnb_ref,
                   pww_ref, pwb_ref, mod_ref, w_ref, fnw_ref, o_ref, hbuf, abuf, *, final, nct, nt, tile_off):
    tile = pl.program_id(1) + tile_off
    first = (tile == 0) | (tile == nct)
    last = (tile == nct - 1) | (tile == nt - 1)
    a0 = SSD_WIDTH
    c0 = SSD_WIDTH + ATT_WIDTH
    u = _dot(os_ref[...], w_ref[0:a0, :]) + _dot(oa_ref[...], w_ref[a0:c0, :])
    oc = _conf_tile(gm_ref, gp_ref, gn_ref, cg_ref, dww_ref, dwb_ref, lnw_ref, lnb_ref, pww_ref, pwb_ref,
                    hbuf, abuf, first, last)
    u = u + _dot(oc, w_ref[c0:c0 + CONV_WIDTH, :])
    x = x_ref[...]
    if tile_off < nct:
        x = jnp.where(tile < nct, c_ref[...], x)
    xn = x + mod_ref[2:3, :] * u
    if final:
        xn = xn * lax.rsqrt(jnp.mean(xn * xn, axis=-1, keepdims=True) + EPS) * fnw_ref[...]
    o_ref[...] = xn


def _mixout(csrc, xsrc, lat_off, o_ssd, o_att, pf, dw_w, dw_b, ln_w, ln_b, pw_w16, pw_b, modsel, w_out16, layer,
            final_norm_w, ctx_len, final):
    bsz, t, _ = o_ssd.shape
    d = xsrc.shape[-1]
    tm = ROW_TILE
    nt = t // tm
    nct = ctx_len // tm
    tile_off = nct if final else 0
    hb = tm // CONF_HALO
    nhalo = t // CONF_HALO
    main = lambda b, i: (b, i + tile_off, 0)
    const2 = lambda b, i: (0, 0)
    glu_col = PF_GLU // (2 * CONV_WIDTH)
    in_specs = _token_specs(tm, d, nct, lat_off, tile_off) + [
        pl.BlockSpec((None, tm, SSD_WIDTH), main),
        pl.BlockSpec((None, tm, ATT_WIDTH), main),
        pl.BlockSpec((None, tm, 2 * CONV_WIDTH), lambda b, i: (b, i + tile_off, glu_col)),
        pl.BlockSpec((None, CONF_HALO, 2 * CONV_WIDTH),
                     lambda b, i: (b, jnp.maximum((i + tile_off) * hb - 1, 0), glu_col)),
        pl.BlockSpec((None, CONF_HALO, 2 * CONV_WIDTH),
                     lambda b, i: (b, jnp.minimum((i + tile_off) * hb + hb, nhalo - 1), glu_col)),
        pl.BlockSpec((None, tm, CONV_WIDTH), lambda b, i: (b, i + tile_off, PF_CG // CONV_WIDTH)),
        pl.BlockSpec((CONV_KERNEL, SUBLANES, CONV_WIDTH), lambda b, i: (0, 0, 0)),
        pl.BlockSpec((SUBLANES, CONV_WIDTH), const2),
        pl.BlockSpec((1, CONV_WIDTH), const2),
        pl.BlockSpec((1, CONV_WIDTH), const2),
        pl.BlockSpec((None, CONV_WIDTH, CONV_WIDTH), lambda b, i: (layer, 0, 0)),
        pl.BlockSpec((1, CONV_WIDTH), const2),
        pl.BlockSpec((None, None, 3, d), lambda b, i: (b, jnp.where(i + tile_off >= nct, 1, 0), 0, 0)),
        pl.BlockSpec((None,) + w_out16.shape[1:], lambda b, i: (layer, 0, 0)),
        pl.BlockSpec((1, d), const2),
    ]
    rows_out = t - tile_off * tm
    return pl.pallas_call(
        functools.partial(_mixout_kernel, final=final, nct=nct, nt=nt, tile_off=tile_off),
        grid=(bsz, rows_out // tm),
        in_specs=in_specs,
        out_specs=pl.BlockSpec((None, tm, d), lambda b, i: (b, i, 0)),
        out_shape=jax.ShapeDtypeStruct((bsz, rows_out, d), F32),
        scratch_shapes=[
            pltpu.VMEM((SUBLANES, tm + 2 * CONF_HALO, CONV_WIDTH), F32),
            pltpu.VMEM((tm, CONV_WIDTH), F32),
        ],
        compiler_params=_cparams("parallel", "arbitrary"),
        name="mixout_final" if final else "mixout",
    )(csrc, xsrc, o_ssd, o_att, pf, pf, pf, pf, dw_w, dw_b, ln_w, ln_b, pw_w16, pw_b, modsel, w_out16, final_norm_w)


def _rope_tables(seq, ctx_len):
    rows = seq // GRID_W
    row = jnp.repeat(jnp.arange(rows, dtype=F32), GRID_W)
    col = jnp.tile(jnp.arange(GRID_W, dtype=F32), rows)
    axis_dim = ATT_HEAD_DIM // 2
    inv_freq = ROPE_BASE ** (-jnp.arange(0, axis_dim, 2, dtype=F32) / axis_dim)
    ar = row[:, None] * inv_freq
    ac = col[:, None] * inv_freq
    cos = jnp.concatenate([jnp.cos(ar), jnp.cos(ar), jnp.cos(ac), jnp.cos(ac)], axis=1)
    sin = jnp.concatenate([-jnp.sin(ar), jnp.sin(ar), -jnp.sin(ac), jnp.sin(ac)], axis=1)
    reps = LANES // ATT_HEAD_DIM
    cos = jnp.concatenate([jnp.ones((ctx_len, LANES), F32), jnp.tile(cos, (1, reps))], axis=0)
    sin = jnp.concatenate([jnp.zeros((ctx_len, LANES), F32), jnp.tile(sin, (1, reps))], axis=0)
    return jnp.concatenate([cos, sin], axis=1)


def _head_lanes(a):
    rep = jnp.concatenate([a, a, a], axis=-1)
    return jnp.pad(rep, [(0, 0)] * (a.ndim - 1) + [(0, LANES - 3 * SSD_HEADS)])


def _pack_w_in(w_in):
    dt0 = SSD_XBC + SSD_WIDTH
    dt_w = w_in[..., dt0:dt0 + 2 * SSD_HEADS]
    packed = jnp.concatenate([w_in[..., :dt0], w_in[..., dt0 + 2 * SSD_HEADS:],
                              _head_lanes(dt_w[..., :SSD_HEADS]), _head_lanes(dt_w[..., SSD_HEADS:])], axis=-1)
    assert packed.shape[-1] == W_WIDTH
    return packed.astype(BF16)


def kernel(x, c, ctx, c_ctx, w_mod, b_mod, norm_w, w_in, ssd_conv_w, ssd_conv_b, ssd_dt_bias, ssd_a_log, ssd_d,
           ssd_norm_w, attn_sink, conv_dw_w, conv_dw_b, conv_ln_w, conv_ln_b, conv_pw_w, conv_pw_b, w_out,
           final_norm_w):
    bsz, seq, d = x.shape
    ctx_len = ctx.shape[1]
    depth = w_in.shape[0]
    assert ctx_len % ROW_TILE == 0 and seq % ROW_TILE == 0 and seq % GRID_W == 0
    assert ROW_TILE == ATT_TILE == CONF_TILE
    nct = ctx_len // ROW_TILE

    cond_rows = -(-(bsz + 1) // SUBLANES) * SUBLANES
    cond = jnp.zeros((cond_rows, d), F32).at[:bsz].set(c).at[bsz].set(c_ctx)
    mod = _modulation(cond, w_mod, b_mod)
    rope_t = _rope_tables(seq, ctx_len)
    csrc, xsrc, lat_off = ctx, x, 0
    w_in16 = _pack_w_in(w_in)
    w_out16 = w_out.astype(BF16)
    pw_w16 = conv_pw_w.astype(BF16)
    norm_w3 = norm_w[:, None, :]

    out = None
    for l in range(depth):
        final = l == depth - 1
        lat = mod[l, :bsz]
        ctxm = jnp.broadcast_to(mod[l, bsz], lat.shape)
        modsel = jnp.stack([ctxm, lat], axis=1).reshape(bsz, 2, 3, d)
        xbc, pf, pb = _inproj(csrc, xsrc, lat_off, ctx_len + seq, modsel, norm_w3, rope_t, w_in16, l, nct)

        o_ssd = _ssd(xbc, pf, ssd_conv_w[l], ssd_conv_b[l][None], _head_lanes(ssd_dt_bias[l])[:, None],
                     _head_lanes(ssd_a_log[l])[:, None], jnp.repeat(ssd_d[l], SSD_HEAD_DIM)[None],
                     ssd_norm_w[l][None], ctx_len)
        tile_off = nct if final else 0
        sink_tab = jnp.broadcast_to(attn_sink[l][:, None] * LOG2E, (ATT_HEADS, LANES))
        o_att = _attention(pf, pb, sink_tab, ctx_len, tile_off)
        dw_w = jnp.broadcast_to(conv_dw_w[l][:, None, :], (CONV_KERNEL, SUBLANES, CONV_WIDTH))
        dw_b = jnp.broadcast_to(conv_dw_b[l][None], (SUBLANES, CONV_WIDTH))
        out = _mixout(csrc, xsrc, lat_off, o_ssd, o_att, pf, dw_w, dw_b, conv_ln_w[l][None], conv_ln_b[l][None], pw_w16,
                      conv_pw_b[l][None], modsel, w_out16, l, final_norm_w[None], ctx_len, final)
        csrc, xsrc, lat_off = out, out, nct
    return out
```

```python
import functools

import jax
import jax.numpy as jnp
from jax import lax
from jax.experimental import pallas as pl
from jax.experimental.pallas import tpu as pltpu

F32 = jnp.float32
BF16 = jnp.bfloat16

GRID_W = 64
EPS = 1e-6

SSD_HEADS = 16
SSD_HEAD_DIM = 64
SSD_WIDTH = SSD_HEADS * SSD_HEAD_DIM
SSD_GROUPS = 2
SSD_STATE = 128
SSD_CONV = 5
SSD_CHUNK = 128
SSD_GN = SSD_GROUPS * SSD_STATE
SSD_XBC = SSD_WIDTH + 2 * SSD_GN
GROUP_COLS = SSD_WIDTH // SSD_GROUPS

ATT_HEADS = 8
ATT_KV_HEADS = 2
ATT_HEAD_DIM = 64
ATT_WIDTH = ATT_HEADS * ATT_HEAD_DIM
ATT_KV_WIDTH = ATT_KV_HEADS * ATT_HEAD_DIM
WINDOW = 128
ROPE_BASE = 10000.0
LOG2E = 1.4426950408889634
ATT_SCALE = ATT_HEAD_DIM ** -0.5 * LOG2E

CONV_WIDTH = 512
CONV_KERNEL = 31
CONV_PAD = (CONV_KERNEL - 1) // 2

LANES = 128

PF_Z = 0
PF_GLU = PF_Z + SSD_WIDTH
PF_AG = PF_GLU + 2 * CONV_WIDTH
PF_CG = PF_AG + ATT_WIDTH
PF_DT = PF_CG + CONV_WIDTH
PF_WIDTH = PF_DT + 2 * LANES
PB_Q = 0
PB_KV = PB_Q + ATT_WIDTH
PB_WIDTH = PB_KV + 2 * ATT_KV_WIDTH

W_XBC = 0
W_Z = W_XBC + SSD_XBC
W_Q = W_Z + SSD_WIDTH
W_K = W_Q + ATT_WIDTH
W_V = W_K + ATT_KV_WIDTH
W_AG = W_V + ATT_KV_WIDTH
W_GLU = W_AG + ATT_WIDTH
W_CG = W_GLU + 2 * CONV_WIDTH
W_DT = W_CG + CONV_WIDTH
W_WIDTH = W_DT + 2 * LANES
SUBLANES = 8
VMEM_LIMIT = 56 * 1024 * 1024

ROW_TILE = 256
ATT_TILE = 256
ATT_QBLOCK = 256
CONF_TILE = 256
CONF_HALO = 16
CONF_SUB = 32
SSD_HALO = SUBLANES
SSD_STEP = 2 * SSD_CHUNK


def _dot(a, b):
    return jnp.dot(a, b, preferred_element_type=F32)


def _dot_nt(a, b):
    return lax.dot_general(a, b, (((1,), (1,)), ((), ())), preferred_element_type=F32)


def _split3(x):
    hi = x.astype(BF16)
    r1 = x - hi.astype(F32)
    mid = r1.astype(BF16)
    lo = (r1 - mid.astype(F32)).astype(BF16)
    return hi, mid, lo


def _pack3(x, width):
    hi, mid, lo = _split3(x)
    lane = lax.broadcasted_iota(jnp.int32, x.shape, 1)
    return jnp.where(lane < width, hi, jnp.where(lane < 2 * width, mid, lo))


def _shift_rows(x, j):
    rows, w = x.shape
    g = x.reshape(rows // SUBLANES, SUBLANES, w)
    r = pltpu.roll(g, SUBLANES - j, 1)
    sub = lax.broadcasted_iota(jnp.int32, (1, SUBLANES, w), 1)
    return jnp.where(sub < SUBLANES - j, r[:-1], r[1:]).reshape(rows - SUBLANES, w)


def _cparams(*sem, flags=None):
    return pltpu.CompilerParams(dimension_semantics=sem, vmem_limit_bytes=VMEM_LIMIT, flags=flags)


def _mod_kernel(c_ref, w_ref, b_ref, o_ref):
    a = jax.nn.silu(c_ref[...]).astype(BF16)
    o_ref[...] = _dot(a, w_ref[...].astype(BF16)) + b_ref[...]


def _modulation(cond, w_mod, b_mod):
    depth, d, n = w_mod.shape
    rows = cond.shape[0]
    tn = n // 2 if (n // 2) % LANES == 0 else n
    return pl.pallas_call(
        _mod_kernel,
        grid=(depth, n // tn),
        in_specs=[
            pl.BlockSpec((rows, d), lambda l, j: (0, 0)),
            pl.BlockSpec((None, d, tn), lambda l, j: (l, 0, j)),
            pl.BlockSpec((None, 1, tn), lambda l, j: (l, 0, j)),
        ],
        out_specs=pl.BlockSpec((None, rows, tn), lambda l, j: (l, 0, j)),
        out_shape=jax.ShapeDtypeStruct((depth, rows, n), F32),
        compiler_params=_cparams("arbitrary", "arbitrary"),
        name="modulation",
    )(cond, w_mod, b_mod.reshape(depth, 1, n))


def _rope(t, cos, sin):
    width = t.shape[-1]
    lane = lax.broadcasted_iota(jnp.int32, t.shape, 1)
    first = (lane % 32) < 16
    rot = jnp.where(first, pltpu.roll(t, width - 16, 1), pltpu.roll(t, 16, 1))
    return t * cos + rot * sin


def _inproj_kernel(c_ref, x_ref, mod_ref, nw_ref, rope_ref, w_ref, xbc_ref, pf_ref, pb_ref, *, nct):
    wxbc = w_ref.at[:, W_XBC:W_XBC + SSD_XBC]
    wz = w_ref.at[:, W_Z:W_Z + SSD_WIDTH]
    wq = w_ref.at[:, W_Q:W_Q + ATT_WIDTH]
    wk = w_ref.at[:, W_K:W_K + ATT_KV_WIDTH]
    wv = w_ref.at[:, W_V:W_V + ATT_KV_WIDTH]
    wag = w_ref.at[:, W_AG:W_AG + ATT_WIDTH]
    wglu = w_ref.at[:, W_GLU:W_GLU + 2 * CONV_WIDTH]
    wcg = w_ref.at[:, W_CG:W_CG + CONV_WIDTH]
    wdt = w_ref.at[:, W_DT:W_DT + 2 * LANES]
    x = jnp.where(pl.program_id(1) < nct, c_ref[...], x_ref[...])
    y = x * lax.rsqrt(jnp.mean(x * x, axis=-1, keepdims=True) + EPS) * nw_ref[...]
    h = (y * (1.0 + mod_ref[1:2, :]) + mod_ref[0:1, :]).astype(BF16)
    cos = rope_ref[:, :LANES]
    sin = rope_ref[:, LANES:]
    reps = ATT_WIDTH // LANES
    cos_q = jnp.concatenate([cos] * reps, axis=1)
    sin_q = jnp.concatenate([sin] * reps, axis=1)
    pb_ref[:, PB_Q:PB_Q + ATT_WIDTH] = (_rope(_dot(h, wq[...]), cos_q, sin_q) * ATT_SCALE).astype(BF16)
    pb_ref[:, PB_KV:PB_KV + ATT_KV_WIDTH] = _rope(_dot(h, wk[...]), cos, sin).astype(BF16)
    pb_ref[:, PB_KV + ATT_KV_WIDTH:PB_KV + 2 * ATT_KV_WIDTH] = _dot(h, wv[...]).astype(BF16)
    pf_ref[:, PF_Z:PF_Z + SSD_WIDTH] = jax.nn.silu(_dot(h, wz[...]))
    pf_ref[:, PF_AG:PF_AG + ATT_WIDTH] = jax.nn.silu(_dot(h, wag[...]))
    pf_ref[:, PF_CG:PF_CG + CONV_WIDTH] = jax.nn.silu(_dot(h, wcg[...]))
    glu = _dot(h, wglu[...])
    pf_ref[:, PF_GLU:PF_GLU + CONV_WIDTH] = glu[:, :CONV_WIDTH]
    pf_ref[:, PF_GLU + CONV_WIDTH:PF_GLU + 2 * CONV_WIDTH] = jax.nn.sigmoid(glu[:, CONV_WIDTH:])
    pf_ref[:, PF_DT:PF_DT + 2 * LANES] = _dot(h, wdt[...])
    xbc_ref[...] = _dot(h, wxbc[...])


def _token_specs(tm, d, nct, lat_off, tile_off):
    return [
        pl.BlockSpec((None, tm, d), lambda b, i: (b, jnp.minimum(i + tile_off, nct - 1), 0)),
        pl.BlockSpec((None, tm, d), lambda b, i: (b, jnp.maximum(i + tile_off - nct, 0) + lat_off, 0)),
    ]


def _inproj(csrc, xsrc, lat_off, t, modsel, norm_w, rope_t, w_packed, layer, nct):
    bsz, _, d = xsrc.shape
    tm = ROW_TILE
    row_map = lambda b, i: (b, i, 0)
    in_specs = _token_specs(tm, d, nct, lat_off, 0) + [
        pl.BlockSpec((None, None, 3, d), lambda b, i: (b, jnp.where(i >= nct, 1, 0), 0, 0)),
        pl.BlockSpec((None, 1, d), lambda b, i: (layer, 0, 0)),
        pl.BlockSpec((tm, 2 * LANES), lambda b, i: (i, 0)),
        pl.BlockSpec((None, d, W_WIDTH), lambda b, i: (layer, 0, 0)),
    ]
    widths = (SSD_XBC, PF_WIDTH, PB_WIDTH)
    return pl.pallas_call(
        functools.partial(_inproj_kernel, nct=nct),
        grid=(bsz, t // tm),
        in_specs=in_specs,
        out_specs=[pl.BlockSpec((None, tm, w), row_map) for w in widths],
        out_shape=[jax.ShapeDtypeStruct((bsz, t, w), dt) for w, dt in zip(widths, (F32, F32, BF16))],
        compiler_params=_cparams("parallel", "arbitrary"),
        name="inproj",
    )(csrc, xsrc, modsel, norm_w, rope_t, w_packed)


def _ssd_prep(xs, bc, dt_raw, dtb, alog, fwd):
    q = SSD_CHUNK
    bb = bc[:, :SSD_GN]
    cb16 = bc[:, SSD_GN:]

    groups = [slice(g * SSD_STATE, (g + 1) * SSD_STATE) for g in range(SSD_GROUPS)]
    scores_g = [_dot_nt(cb16[:, gs], bb[:, gs]) for gs in groups]

    lane = lax.broadcasted_iota(jnp.int32, (q, LANES), 1)
    head_lane = lane < 3 * SSD_HEADS
    dt = jnp.where(head_lane, jax.nn.softplus(dt_raw + dtb), 0.0)
    da = dt * (-jnp.exp(alog))
    li = lax.broadcasted_iota(jnp.int32, (q, q), 0)
    si = lax.broadcasted_iota(jnp.int32, (q, q), 1)
    causal = (si <= li) if fwd else (si >= li)
    ones = jnp.where(causal, 1.0, 0.0).astype(BF16)
    cs3 = _dot(ones, jnp.concatenate(_split3(da), axis=1))
    cs = cs3[:, :LANES] + cs3[:, LANES:2 * LANES] + cs3[:, 2 * LANES:]
    total = jnp.sum(da, axis=0, keepdims=True)
    to_end = jnp.exp(total - cs)
    from_start = jnp.where(head_lane, jnp.exp(cs), 0.0)
    chunk_decay = jnp.where(head_lane[:SUBLANES], jnp.exp(total), 0.0)
    cs_t = cs.T

    er = lax.broadcasted_iota(jnp.int32, (LANES, SSD_WIDTH), 0)
    ec = lax.broadcasted_iota(jnp.int32, (LANES, SSD_WIDTH), 1)
    spread = jnp.where((ec // SSD_HEAD_DIM == er % SSD_HEADS) & (er < 3 * SSD_HEADS), 1.0, 0.0).astype(BF16)
    dt_e = _dot(_pack3(dt, SSD_HEADS), spread)
    stacked = jnp.concatenate([dt * to_end, from_start, chunk_decay], axis=0)
    stacked_e = _dot(_pack3(stacked, SSD_HEADS), spread)
    w_e = stacked_e[:q]
    fs_e = stacked_e[q:2 * q]
    cd_e = stacked_e[2 * q:2 * q + 1]

    xd = (xs * dt_e).astype(BF16)
    xdw = (xs * w_e).astype(BF16)
    low = lax.broadcasted_iota(jnp.int32, (q, LANES), 1) < SSD_HEAD_DIM
    zero16 = jnp.zeros((q, LANES), BF16)
    y_diag, new_states = [], []
    for g in range(SSD_GROUPS):
        gs = groups[g]
        gc = slice(g * GROUP_COLS, (g + 1) * GROUP_COLS)
        scores = scores_g[g]
        new_states.append(_dot(bb[:, gs].astype(F32).T.astype(BF16), xdw[:, gc]))
        for pr in range(GROUP_COLS // LANES):
            h0 = (g * GROUP_COLS + pr * LANES) // SSD_HEAD_DIM
            cols = slice(h0 * SSD_HEAD_DIM, h0 * SSD_HEAD_DIM + LANES)
            xpair = xd[:, cols]
            m0 = (scores * jnp.exp(jnp.where(causal, cs[:, h0:h0 + 1] - cs_t[h0:h0 + 1, :], -jnp.inf))).astype(BF16)
            m1 = (scores * jnp.exp(jnp.where(causal, cs[:, h0 + 1:h0 + 2] - cs_t[h0 + 1:h0 + 2, :], -jnp.inf))).astype(BF16)
            block_diag = jnp.concatenate([jnp.where(low, xpair, zero16), jnp.where(low, zero16, xpair)], axis=0)
            y_diag.append(_dot(jnp.concatenate([m0, m1], axis=1), block_diag))
    return y_diag, new_states, cb16, fs_e, cd_e


def _ssd_finish(prep, st_ref):
    y_diag, new_states, cb16, fs_e, cd_e = prep
    ys = []
    for g in range(SSD_GROUPS):
        gs = slice(g * SSD_STATE, (g + 1) * SSD_STATE)
        gc = slice(g * GROUP_COLS, (g + 1) * GROUP_COLS)
        entering = st_ref[g]
        y_off = _dot(cb16[:, gs], entering.astype(BF16)) * fs_e[:, gc]
        st_ref[g] = entering * cd_e[:, gc] + new_states[g]
        per_group = GROUP_COLS // LANES
        for pr in range(per_group):
            ys.append(y_diag[g * per_group + pr] + y_off[:, pr * LANES:(pr + 1) * LANES])
    return jnp.concatenate(ys, axis=1)


def _ssd_bwd_block(k, ncb, nb):
    return jnp.where(k < ncb, ncb - 1 - k, nb + ncb - 1 - k)


def _ssd_kernel(xm_ref, xp_ref, xn_ref, dt_ref, z_ref, cw_ref, cb_ref, dtb_ref, alog_ref, dvec_ref, nw_ref,
                o_ref, yb_ref, st_ref, xs_ref, bc_ref, *, ncb, nb):
    q = SSD_CHUNK
    rows = SSD_STEP
    n_sub = rows // q
    s = pl.program_id(1)

    def conv_block(cb):
        first = (cb == 0) | (cb == ncb)
        last = (cb == ncb - 1) | (cb == nb - 1)
        xext = jnp.concatenate([jnp.where(first, 0.0, xp_ref[...]), xm_ref[...],
                                jnp.where(last, 0.0, xn_ref[...])], axis=0)
        acc = cb_ref[...]
        for k in range(SSD_CONV):
            d = k - (SSD_CONV - 1) // 2
            if d == 0:
                tap = xext[SSD_HALO:SSD_HALO + rows]
            elif d > 0:
                tap = _shift_rows(xext, d)[SSD_HALO:SSD_HALO + rows]
            else:
                tap = _shift_rows(xext, SSD_HALO + d)[:rows]
            acc = acc + cw_ref[k:k + 1, :] * tap
        u = jax.nn.silu(acc)
        dst = pl.ds(pl.multiple_of(cb * rows, rows), rows)
        xs_ref[dst, :] = u[:, :SSD_WIDTH]
        bc_ref[dst, :] = u[:, SSD_WIDTH:].astype(BF16)

    def chunk_rows(blk, j):
        return pl.ds(pl.multiple_of(blk * rows + j * q, q), q)

    @pl.when((s == 0) | (s == nb))
    def _():
        st_ref[...] = jnp.zeros_like(st_ref)

    @pl.when(s <= nb - 1)
    def _():
        conv_block(_ssd_bwd_block(s, ncb, nb))

    @pl.when(s < nb)
    def _():
        blk = _ssd_bwd_block(s, ncb, nb)
        order = tuple(reversed(range(n_sub)))
        preps = [_ssd_prep(xs_ref[chunk_rows(blk, j), :], bc_ref[chunk_rows(blk, j), :], dt_ref[j * q:(j + 1) * q, :],
                           dtb_ref[...], alog_ref[...], False) for j in order]
        for j, prep in zip(order, preps):
            yb_ref[chunk_rows(blk, j), :] = _ssd_finish(prep, st_ref)

    @pl.when(s >= nb)
    def _():
        blk = s - nb
        preps = [_ssd_prep(xs_ref[chunk_rows(blk, j), :], bc_ref[chunk_rows(blk, j), :], dt_ref[j * q:(j + 1) * q, :],
                           dtb_ref[...], alog_ref[...], True) for j in range(n_sub)]
        for j, prep in enumerate(preps):
            y = _ssd_finish(prep, st_ref)
            tot = y + yb_ref[chunk_rows(blk, j), :] + dvec_ref[...] * xs_ref[chunk_rows(blk, j), :]
            gated = tot * z_ref[j * q:(j + 1) * q, :]
            outs = []
            for g in range(SSD_GROUPS):
                gc = slice(g * GROUP_COLS, (g + 1) * GROUP_COLS)
                v = gated[:, gc]
                outs.append(v * lax.rsqrt(jnp.mean(v * v, axis=-1, keepdims=True) + EPS) * nw_ref[:, gc])
            o_ref[j * q:(j + 1) * q, :] = jnp.concatenate(outs, axis=1).astype(BF16)


def _ssd(xbc, pf, conv_w, conv_b, dt_bias, a_log, dvec, norm_w, ctx_len):
    bsz, t, _ = pf.shape
    rows = SSD_STEP
    nb = t // rows
    ncb = ctx_len // rows
    hb = rows // SSD_HALO
    nhalo = t // SSD_HALO
    const2 = lambda b, s: (0, 0)
    xblk = lambda s: _ssd_bwd_block(jnp.minimum(s, nb - 1), ncb, nb)
    cur = lambda s: jnp.where(s < nb, _ssd_bwd_block(s, ncb, nb), s - nb)
    phase = lambda s: jnp.where(s < nb, 1, 0)
    fwd_blk = lambda s: jnp.maximum(s - nb, 0)
    in_specs = [
        pl.BlockSpec((None, rows, SSD_XBC), lambda b, s: (b, xblk(s), 0)),
        pl.BlockSpec((None, SSD_HALO, SSD_XBC), lambda b, s: (b, jnp.maximum(xblk(s) * hb - 1, 0), 0)),
        pl.BlockSpec((None, SSD_HALO, SSD_XBC), lambda b, s: (b, jnp.minimum(xblk(s) * hb + hb, nhalo - 1), 0)),
        pl.BlockSpec((None, rows, LANES), lambda b, s: (b, cur(s), PF_DT // LANES + phase(s))),
        pl.BlockSpec((None, rows, SSD_WIDTH), lambda b, s: (b, fwd_blk(s), PF_Z // SSD_WIDTH)),
        pl.BlockSpec((SSD_CONV, SSD_XBC), const2),
        pl.BlockSpec((1, SSD_XBC), const2),
        pl.BlockSpec((None, 1, LANES), lambda b, s: (phase(s), 0, 0)),
        pl.BlockSpec((None, 1, LANES), lambda b, s: (phase(s), 0, 0)),
        pl.BlockSpec((1, SSD_WIDTH), const2),
        pl.BlockSpec((1, SSD_WIDTH), const2),
    ]
    return pl.pallas_call(
        functools.partial(_ssd_kernel, ncb=ncb, nb=nb),
        grid=(bsz, 2 * nb),
        in_specs=in_specs,
        out_specs=pl.BlockSpec((None, rows, SSD_WIDTH), lambda b, s: (b, fwd_blk(s), 0)),
        out_shape=jax.ShapeDtypeStruct((bsz, t, SSD_WIDTH), BF16),
        scratch_shapes=[
            pltpu.VMEM((t, SSD_WIDTH), F32),
            pltpu.VMEM((SSD_GROUPS, SSD_STATE, GROUP_COLS), F32),
            pltpu.VMEM((t, SSD_WIDTH), F32),
            pltpu.VMEM((t, 2 * SSD_GN), BF16),
        ],
        compiler_params=_cparams("parallel", "arbitrary"),
        name="ssd",
    )(xbc, xbc, xbc, pf, pf, conv_w, conv_b, dt_bias, a_log, dvec, norm_w)


def _swap_halves(blk):
    half = blk.shape[-1] // 2
    return jnp.concatenate([blk[:, half:], blk[:, :half]], axis=1)


def _head_pads(blk, g, low):
    zero = jnp.zeros_like(blk)
    sw = _swap_halves(blk)
    if g == 0:
        return jnp.where(low, blk, zero), jnp.where(low, zero, sw)
    return jnp.where(low, sw, zero), jnp.where(low, zero, blk)


def _attn_kernel(q_ref, kvm_ref, kvp_ref, kvn_ref, kvc_ref, ag_ref, sink_ref, o_ref, pad_ref,
                 *, tile_off, nct, seq, ctx_len):
    qb = ATT_TILE
    blk = ATT_QBLOCK
    wk = blk + 2 * WINDOW
    tile = pl.program_id(1) + tile_off
    lat0 = (tile - nct) * qb
    low_k = lax.broadcasted_iota(jnp.int32, (1, LANES), 1) < ATT_HEAD_DIM

    kv = jnp.concatenate([kvc_ref[...], kvp_ref[...], kvm_ref[...], kvn_ref[...]], axis=0)
    for g in range(ATT_KV_HEADS):
        ka, kb = _head_pads(kv[:, :ATT_KV_WIDTH], g, low_k)
        va, vb = _head_pads(kv[:, ATT_KV_WIDTH:], g, low_k)
        pad_ref[g, 0] = ka
        pad_ref[g, 1] = kb
        pad_ref[g, 2] = va
        pad_ref[g, 3] = vb

    r = lax.broadcasted_iota(jnp.int32, (blk, wk), 0)
    col = lax.broadcasted_iota(jnp.int32, (blk, wk), 1)
    in_band = jnp.abs(col - WINDOW - r) <= WINDOW
    neg = jnp.finfo(F32).min
    rep = ATT_HEADS // ATT_KV_HEADS
    pairs_per_group = rep * ATT_HEAD_DIM // LANES
    for a in range(qb // blk):
        rows = slice(a * blk, (a + 1) * blk)
        kpos = lat0 + a * blk - WINDOW + col
        valid = in_band & (kpos >= 0) & (kpos < seq) & (lat0 >= 0)
        win0 = ctx_len + a * blk
        def operand(g, slot):
            if a == 0:
                return pad_ref[g, slot, 0:win0 + wk, :]
            return jnp.concatenate([pad_ref[g, slot, 0:ctx_len, :], pad_ref[g, slot, win0:win0 + wk, :]], axis=0)

        for g in range(ATT_KV_HEADS):
            for pr in range(pairs_per_group):
                p = g * pairs_per_group + pr
                qp = q_ref[rows, p * LANES:(p + 1) * LANES]
                acc = jnp.zeros((blk, LANES), F32)
                invs = []
                for j in range(2):
                    head = 2 * p + j
                    sink = sink_ref[head:head + 1, 0:1]
                    s = _dot_nt(qp, operand(g, j))
                    s = jnp.concatenate([s[:, :ctx_len], jnp.where(valid, s[:, ctx_len:], neg)], axis=1)
                    m = jnp.maximum(jnp.max(s, axis=-1, keepdims=True), sink)
                    pexp = jnp.exp2(s - m)
                    den = jnp.sum(pexp, axis=-1, keepdims=True) + jnp.exp2(sink - m)
                    acc = acc + _dot(pexp.astype(BF16), operand(g, 2 + j))
                    invs.append(1.0 / den)
                gate = ag_ref[rows, p * LANES:(p + 1) * LANES]
                o_ref[rows, p * LANES:(p + 1) * LANES] = (acc * jnp.where(low_k, invs[0], invs[1]) * gate).astype(BF16)


def _attention(pf, pb, sink_tab, ctx_len, tile_off):
    bsz, t, _ = pf.shape
    qb = ATT_TILE
    nct = ctx_len // qb
    hb = qb // WINDOW
    ctx_hb = ctx_len // WINDOW
    n_hb = t // WINDOW
    ntiles = t // qb - tile_off
    kvw = 2 * ATT_KV_WIDTH
    kv_col = PB_KV // kvw
    in_specs = [
        pl.BlockSpec((None, qb, ATT_WIDTH), lambda b, i: (b, i + tile_off, PB_Q // ATT_WIDTH)),
        pl.BlockSpec((None, qb, kvw), lambda b, i: (b, i + tile_off, kv_col)),
        pl.BlockSpec((None, WINDOW, kvw), lambda b, i: (b, jnp.maximum((i + tile_off) * hb - 1, ctx_hb), kv_col)),
        pl.BlockSpec((None, WINDOW, kvw), lambda b, i: (b, jnp.minimum((i + tile_off) * hb + hb, n_hb - 1), kv_col)),
        pl.BlockSpec((None, ctx_len, kvw), lambda b, i: (b, 0, kv_col)),
        pl.BlockSpec((None, qb, ATT_WIDTH), lambda b, i: (b, i + tile_off, PF_AG // ATT_WIDTH)),
        pl.BlockSpec((ATT_HEADS, LANES), lambda b, i: (0, 0)),
    ]
    return pl.pallas_call(
        functools.partial(_attn_kernel, tile_off=tile_off, nct=nct, seq=t - ctx_len, ctx_len=ctx_len),
        grid=(bsz, ntiles),
        in_specs=in_specs,
        out_specs=pl.BlockSpec((None, qb, ATT_WIDTH), lambda b, i: (b, i + tile_off, 0)),
        out_shape=jax.ShapeDtypeStruct((bsz, t, ATT_WIDTH), BF16),
        scratch_shapes=[pltpu.VMEM((ATT_KV_HEADS, 4, ctx_len + qb + 2 * WINDOW, LANES), BF16)],
        compiler_params=_cparams("parallel", "arbitrary"),
        name="attention",
    )(pb, pb, pb, pb, pb, pf, sink_tab)


def _conf_tile(gm_ref, gp_ref, gn_ref, cg_ref, dww_ref, dwb_ref, lnw_ref, lnb_ref, pww_ref, pwb_ref, hbuf, abuf,
               first, last):
    tc = CONF_TILE

    def glu(v):
        return v[:, :CONV_WIDTH] * v[:, CONV_WIDTH:]

    rows = tc + 2 * CONF_HALO
    ext = jnp.concatenate([jnp.where(first, 0.0, glu(gp_ref[...])), glu(gm_ref[...]),
                           jnp.where(last, 0.0, glu(gn_ref[...]))], axis=0)
    hbuf[0] = ext
    for j in range(1, SUBLANES):
        hbuf[j, 0:rows - SUBLANES, :] = _shift_rows(ext, j)
    base = CONF_HALO - CONV_PAD
    for r in range(tc // CONF_SUB):
        acc = jnp.broadcast_to(dwb_ref[...], (CONF_SUB // SUBLANES, SUBLANES, CONV_WIDTH))
        for k in range(CONV_KERNEL):
            blk, res = divmod(base + k, SUBLANES)
            strip = hbuf[res, pl.ds(r * CONF_SUB + blk * SUBLANES, CONF_SUB), :]
            acc = acc + dww_ref[k] * strip.reshape(CONF_SUB // SUBLANES, SUBLANES, CONV_WIDTH)
        acc = acc.reshape(CONF_SUB, CONV_WIDTH)
        xc = acc - jnp.mean(acc, axis=-1, keepdims=True)
        var = jnp.mean(xc * xc, axis=-1, keepdims=True)
        ln = xc * lax.rsqrt(var + EPS) * lnw_ref[...] + lnb_ref[...]
        abuf[pl.ds(r * CONF_SUB, CONF_SUB), :] = jax.nn.silu(ln)
    out = _dot(abuf[...].astype(BF16), pww_ref[...]) + pwb_ref[...]
    return (out * cg_ref[...]).astype(BF16)


def _conf_kernel(gm_ref, gp_ref, gn_ref, cg_ref, dww_ref, dwb_ref, lnw_ref, lnb_ref, pww_ref, pwb_ref, o_ref,
                 hbuf, abuf, *, tile_off, nct, nt):
    tile = pl.program_id(1) + tile_off
    first = (tile == 0) | (tile == nct)
    last = (tile == nct - 1) | (tile == nt - 1)
    o_ref[...] = _conf_tile(gm_ref, gp_ref, gn_ref, cg_ref, dww_ref, dwb_ref, lnw_ref, lnb_ref, pww_ref, pwb_ref,
                            hbuf, abuf, first, last)


def _conformer(pf, dw_w, dw_b, ln_w, ln_b, pw_w16, layer, pw_b, ctx_len, tile_off):
    bsz, t, _ = pf.shape
    tc = CONF_TILE
    nt = t // tc
    nct = ctx_len // tc
    hb = tc // CONF_HALO
    nhalo = t // CONF_HALO
    main = lambda b, i: (b, i + tile_off, 0)
    const2 = lambda b, i: (0, 0)
    glu_col = PF_GLU // (2 * CONV_WIDTH)
    in_specs = [
        pl.BlockSpec((None, tc, 2 * CONV_WIDTH), lambda b, i: (b, i + tile_off, glu_col)),
        pl.BlockSpec((None, CONF_HALO, 2 * CONV_WIDTH),
                     lambda b, i: (b, jnp.maximum((i + tile_off) * hb - 1, 0), glu_col)),
        pl.BlockSpec((None, CONF_HALO, 2 * CONV_WIDTH),
                     lambda b, i: (b, jnp.minimum((i + tile_off) * hb + hb, nhalo - 1), glu_col)),
        pl.BlockSpec((None, tc, CONV_WIDTH), lambda b, i: (b, i + tile_off, PF_CG // CONV_WIDTH)),
        pl.BlockSpec((CONV_KERNEL, SUBLANES, CONV_WIDTH), lambda b, i: (0, 0, 0)),
        pl.BlockSpec((SUBLANES, CONV_WIDTH), const2),
        pl.BlockSpec((1, CONV_WIDTH), const2),
        pl.BlockSpec((1, CONV_WIDTH), const2),
        pl.BlockSpec((None, CONV_WIDTH, CONV_WIDTH), lambda b, i: (layer, 0, 0)),
        pl.BlockSpec((1, CONV_WIDTH), const2),
    ]
    return pl.pallas_call(
        functools.partial(_conf_kernel, tile_off=tile_off, nct=nct, nt=nt),
        grid=(bsz, nt - tile_off),
        in_specs=in_specs,
        out_specs=pl.BlockSpec((None, tc, CONV_WIDTH), main),
        out_shape=jax.ShapeDtypeStruct((bsz, t, CONV_WIDTH), BF16),
        scratch_shapes=[
            pltpu.VMEM((SUBLANES, tc + 2 * CONF_HALO, CONV_WIDTH), F32),
            pltpu.VMEM((tc, CONV_WIDTH), F32),
        ],
        compiler_params=_cparams("parallel", "arbitrary"),
        name="conformer",
    )(pf, pf, pf, pf, dw_w, dw_b, ln_w, ln_b, pw_w16, pw_b)


def _outproj_kernel(c_ref, x_ref, os_ref, oa_ref, oc_ref, mod_ref, w_ref, fnw_ref, o_ref,
                    *, final, nct, tile_off):
    a0 = SSD_WIDTH
    c0 = SSD_WIDTH + ATT_WIDTH
    u = (_dot(os_ref[...], w_ref[0:a0, :]) + _dot(oa_ref[...], w_ref[a0:c0, :])
         + _dot(oc_ref[...], w_ref[c0:c0 + CONV_WIDTH, :]))
    x = x_ref[...]
    if tile_off < nct:
        x = jnp.where(pl.program_id(1) + tile_off < nct, c_ref[...], x)
    xn = x + mod_ref[2:3, :] * u
    if final:
        xn = xn * lax.rsqrt(jnp.mean(xn * xn, axis=-1, keepdims=True) + EPS) * fnw_ref[...]
    o_ref[...] = xn


def _outproj(csrc, xsrc, lat_off, o_ssd, o_att, o_conv, modsel, w_out16, layer, final_norm_w, nct, final):
    bsz, t, _ = o_ssd.shape
    d = xsrc.shape[-1]
    tm = ROW_TILE
    tile_off = nct if final else 0
    main = lambda b, i: (b, i + tile_off, 0)
    in_specs = _token_specs(tm, d, nct, lat_off, tile_off) + [
        pl.BlockSpec((None, tm, SSD_WIDTH), main),
        pl.BlockSpec((None, tm, ATT_WIDTH), main),
        pl.BlockSpec((None, tm, CONV_WIDTH), main),
        pl.BlockSpec((None, None, 3, d), lambda b, i: (b, jnp.where(i + tile_off >= nct, 1, 0), 0, 0)),
        pl.BlockSpec((None,) + w_out16.shape[1:], lambda b, i: (layer, 0, 0)),
        pl.BlockSpec((1, d), lambda b, i: (0, 0)),
    ]
    rows_out = t - tile_off * tm
    return pl.pallas_call(
        functools.partial(_outproj_kernel, final=final, nct=nct, tile_off=tile_off),
        grid=(bsz, rows_out // tm),
        in_specs=in_specs,
        out_specs=pl.BlockSpec((None, tm, d), lambda b, i: (b, i, 0)),
        out_shape=jax.ShapeDtypeStruct((bsz, rows_out, d), F32),
        compiler_params=_cparams("parallel", "arbitrary"),
        name="outproj_final" if final else "outproj",
    )(csrc, xsrc, o_ssd, o_att, o_conv, modsel, w_out16, final_norm_w)


def _mixout_kernel(c_ref, x_ref, os_ref, oa_ref, gm_ref, gp_ref, gn_ref, cg_ref, dww_ref, dwb_ref, lnw_ref, lnb_ref,
                   pww_ref, pwb_ref, mod_ref, w_ref, fnw_ref, o_ref, hbuf, abuf, *, final, nct, nt, tile_off):
    tile = pl.program_id(1) + tile_off
    first = (tile == 0) | (tile == nct)
    last = (tile == nct - 1) | (tile == nt - 1)
    a0 = SSD_WIDTH
    c0 = SSD_WIDTH + ATT_WIDTH
    u = _dot(os_ref[...], w_ref[0:a0, :]) + _dot(oa_ref[...], w_ref[a0:c0, :])
    oc = _conf_tile(gm_ref, gp_ref, gn_ref, cg_ref, dww_ref, dwb_ref, lnw_ref, lnb_ref, pww_ref, pwb_ref,
                    hbuf, abuf, first, last)
    u = u + _dot(oc, w_ref[c0:c0 + CONV_WIDTH, :])
    x = x_ref[...]
    if tile_off < nct:
        x = jnp.where(tile < nct, c_ref[...], x)
    xn = x + mod_ref[2:3, :] * u
    if final:
        xn = xn * lax.rsqrt(jnp.mean(xn * xn, axis=-1, keepdims=True) + EPS) * fnw_ref[...]
    o_ref[...] = xn


def _mixout(csrc, xsrc, lat_off, o_ssd, o_att, pf, dw_w, dw_b, ln_w, ln_b, pw_w16, pw_b, modsel, w_out16, layer,
            final_norm_w, ctx_len, final):
    bsz, t, _ = o_ssd.shape
    d = xsrc.shape[-1]
    tm = ROW_TILE
    nt = t // tm
    nct = ctx_len // tm
    tile_off = nct if final else 0
    hb = tm // CONF_HALO
    nhalo = t // CONF_HALO
    main = lambda b, i: (b, i + tile_off, 0)
    const2 = lambda b, i: (0, 0)
    glu_col = PF_GLU // (2 * CONV_WIDTH)
    in_specs = _token_specs(tm, d, nct, lat_off, tile_off) + [
        pl.BlockSpec((None, tm, SSD_WIDTH), main),
        pl.BlockSpec((None, tm, ATT_WIDTH), main),
        pl.BlockSpec((None, tm, 2 * CONV_WIDTH), lambda b, i: (b, i + tile_off, glu_col)),
        pl.BlockSpec((None, CONF_HALO, 2 * CONV_WIDTH),
                     lambda b, i: (b, jnp.maximum((i + tile_off) * hb - 1, 0), glu_col)),
        pl.BlockSpec((None, CONF_HALO, 2 * CONV_WIDTH),
                     lambda b, i: (b, jnp.minimum((i + tile_off) * hb + hb, nhalo - 1), glu_col)),
        pl.BlockSpec((None, tm, CONV_WIDTH), lambda b, i: (b, i + tile_off, PF_CG // CONV_WIDTH)),
        pl.BlockSpec((CONV_KERNEL, SUBLANES, CONV_WIDTH), lambda b, i: (0, 0, 0)),
        pl.BlockSpec((SUBLANES, CONV_WIDTH), const2),
        pl.BlockSpec((1, CONV_WIDTH), const2),
        pl.BlockSpec((1, CONV_WIDTH), const2),
        pl.BlockSpec((None, CONV_WIDTH, CONV_WIDTH), lambda b, i: (layer, 0, 0)),
        pl.BlockSpec((1, CONV_WIDTH), const2),
        pl.BlockSpec((None, None, 3, d), lambda b, i: (b, jnp.where(i + tile_off >= nct, 1, 0), 0, 0)),
        pl.BlockSpec((None,) + w_out16.shape[1:], lambda b, i: (layer, 0, 0)),
        pl.BlockSpec((1, d), const2),
    ]
    rows_out = t - tile_off * tm
    return pl.pallas_call(
        functools.partial(_mixout_kernel, final=final, nct=nct, nt=nt, tile_off=tile_off),
        grid=(bsz, rows_out // tm),
        in_specs=in_specs,
        out_specs=pl.BlockSpec((None, tm, d), lambda b, i: (b, i, 0)),
        out_shape=jax.ShapeDtypeStruct((bsz, rows_out, d), F32),
        scratch_shapes=[
            pltpu.VMEM((SUBLANES, tm + 2 * CONF_HALO, CONV_WIDTH), F32),
            pltpu.VMEM((tm, CONV_WIDTH), F32),
        ],
        compiler_params=_cparams("parallel", "arbitrary"),
        name="mixout_final" if final else "mixout",
    )(csrc, xsrc, o_ssd, o_att, pf, pf, pf, pf, dw_w, dw_b, ln_w, ln_b, pw_w16, pw_b, modsel, w_out16, final_norm_w)


def _rope_tables(seq, ctx_len):
    rows = seq // GRID_W
    row = jnp.repeat(jnp.arange(rows, dtype=F32), GRID_W)
    col = jnp.tile(jnp.arange(GRID_W, dtype=F32), rows)
    axis_dim = ATT_HEAD_DIM // 2
    inv_freq = ROPE_BASE ** (-jnp.arange(0, axis_dim, 2, dtype=F32) / axis_dim)
    ar = row[:, None] * inv_freq
    ac = col[:, None] * inv_freq
    cos = jnp.concatenate([jnp.cos(ar), jnp.cos(ar), jnp.cos(ac), jnp.cos(ac)], axis=1)
    sin = jnp.concatenate([-jnp.sin(ar), jnp.sin(ar), -jnp.sin(ac), jnp.sin(ac)], axis=1)
    reps = LANES // ATT_HEAD_DIM
    cos = jnp.concatenate([jnp.ones((ctx_len, LANES), F32), jnp.tile(cos, (1, reps))], axis=0)
    sin = jnp.concatenate([jnp.zeros((ctx_len, LANES), F32), jnp.tile(sin, (1, reps))], axis=0)
    return jnp.concatenate([cos, sin], axis=1)


def _head_lanes(a):
    rep = jnp.concatenate([a, a, a], axis=-1)
    return jnp.pad(rep, [(0, 0)] * (a.ndim - 1) + [(0, LANES - 3 * SSD_HEADS)])


def _pack_w_in(w_in):
    dt0 = SSD_XBC + SSD_WIDTH
    dt_w = w_in[..., dt0:dt0 + 2 * SSD_HEADS]
    packed = jnp.concatenate([w_in[..., :dt0], w_in[..., dt0 + 2 * SSD_HEADS:],
                              _head_lanes(dt_w[..., :SSD_HEADS]), _head_lanes(dt_w[..., SSD_HEADS:])], axis=-1)
    assert packed.shape[-1] == W_WIDTH
    return packed.astype(BF16)


def kernel(x, c, ctx, c_ctx, w_mod, b_mod, norm_w, w_in, ssd_conv_w, ssd_conv_b, ssd_dt_bias, ssd_a_log, ssd_d,
           ssd_norm_w, attn_sink, conv_dw_w, conv_dw_b, conv_ln_w, conv_ln_b, conv_pw_w, conv_pw_b, w_out,
           final_norm_w):
    bsz, seq, d = x.shape
    ctx_len = ctx.shape[1]
    depth = w_in.shape[0]
    assert ctx_len % ROW_TILE == 0 and seq % ROW_TILE == 0 and seq % GRID_W == 0
    assert ROW_TILE == ATT_TILE == CONF_TILE
    nct = ctx_len // ROW_TILE

    cond_rows = -(-(bsz + 1) // SUBLANES) * SUBLANES
    cond = jnp.zeros((cond_rows, d), F32).at[:bsz].set(c).at[bsz].set(c_ctx)
    mod = _modulation(cond, w_mod, b_mod)
    rope_t = _rope_tables(seq, ctx_len)
    csrc, xsrc, lat_off = ctx, x, 0
    w_in16 = _pack_w_in(w_in)
    w_out16 = w_out.astype(BF16)
    pw_w16 = conv_pw_w.astype(BF16)
    norm_w3 = norm_w[:, None, :]

    out = None
    for l in range(depth):
        final = l == depth - 1
        lat = mod[l, :bsz]
        ctxm = jnp.broadcast_to(mod[l, bsz], lat.shape)
        modsel = jnp.stack([ctxm, lat], axis=1).reshape(bsz, 2, 3, d)
        xbc, pf, pb = _inproj(csrc, xsrc, lat_off, ctx_len + seq, modsel, norm_w3, rope_t, w_in16, l, nct)

        o_ssd = _ssd(xbc, pf, ssd_conv_w[l], ssd_conv_b[l][None], _head_lanes(ssd_dt_bias[l])[:, None],
                     _head_lanes(ssd_a_log[l])[:, None], jnp.repeat(ssd_d[l], SSD_HEAD_DIM)[None],
                     ssd_norm_w[l][None], ctx_len)
        tile_off = nct if final else 0
        sink_tab = jnp.broadcast_to(attn_sink[l][:, None] * LOG2E, (ATT_HEADS, LANES))
        o_att = _attention(pf, pb, sink_tab, ctx_len, tile_off)
        dw_w = jnp.broadcast_to(conv_dw_w[l][:, None, :], (CONV_KERNEL, SUBLANES, CONV_WIDTH))
        dw_b = jnp.broadcast_to(conv_dw_b[l][None], (SUBLANES, CONV_WIDTH))
        out = _mixout(csrc, xsrc, lat_off, o_ssd, o_att, pf, dw_w, dw_b, conv_ln_w[l][None], conv_ln_b[l][None], pw_w16,
                      conv_pw_b[l][None], modsel, w_out16, l, final_norm_w[None], ctx_len, final)
        csrc, xsrc, lat_off = out, out, nct
    return out
```

```python
import functools

import jax
import jax.numpy as jnp
from jax import lax
from jax.experimental import pallas as pl
from jax.experimental.pallas import tpu as pltpu

F32 = jnp.float32
BF16 = jnp.bfloat16

GRID_W = 64
EPS = 1e-6

SSD_HEADS = 16
SSD_HEAD_DIM = 64
SSD_WIDTH = SSD_HEADS * SSD_HEAD_DIM
SSD_GROUPS = 2
SSD_STATE = 128
SSD_CONV = 5
SSD_CHUNK = 128
SSD_GN = SSD_GROUPS * SSD_STATE
SSD_XBC = SSD_WIDTH + 2 * SSD_GN
GROUP_COLS = SSD_WIDTH // SSD_GROUPS

ATT_HEADS = 8
ATT_KV_HEADS = 2
ATT_HEAD_DIM = 64
ATT_WIDTH = ATT_HEADS * ATT_HEAD_DIM
ATT_KV_WIDTH = ATT_KV_HEADS * ATT_HEAD_DIM
WINDOW = 128
ROPE_BASE = 10000.0
LOG2E = 1.4426950408889634
ATT_SCALE = ATT_HEAD_DIM ** -0.5 * LOG2E

CONV_WIDTH = 512
CONV_KERNEL = 31
CONV_PAD = (CONV_KERNEL - 1) // 2

LANES = 128

PF_Z = 0
PF_GLU = PF_Z + SSD_WIDTH
PF_AG = PF_GLU + 2 * CONV_WIDTH
PF_CG = PF_AG + ATT_WIDTH
PF_DT = PF_CG + CONV_WIDTH
PF_WIDTH = PF_DT + 2 * LANES
PB_Q = 0
PB_KV = PB_Q + ATT_WIDTH
PB_WIDTH = PB_KV + 2 * ATT_KV_WIDTH

W_XBC = 0
W_Z = W_XBC + SSD_XBC
W_Q = W_Z + SSD_WIDTH
W_K = W_Q + ATT_WIDTH
W_V = W_K + ATT_KV_WIDTH
W_AG = W_V + ATT_KV_WIDTH
W_GLU = W_AG + ATT_WIDTH
W_CG = W_GLU + 2 * CONV_WIDTH
W_DT = W_CG + CONV_WIDTH
W_WIDTH = W_DT + 2 * LANES
SUBLANES = 8
VMEM_LIMIT = 56 * 1024 * 1024

ROW_TILE = 256
ATT_TILE = 256
ATT_QBLOCK = 256
CONF_TILE = 256
CONF_HALO = 16
CONF_SUB = 32
SSD_HALO = SUBLANES
SSD_STEP = 2 * SSD_CHUNK


def _dot(a, b):
    return jnp.dot(a, b, preferred_element_type=F32)


def _dot_nt(a, b):
    return lax.dot_general(a, b, (((1,), (1,)), ((), ())), preferred_element_type=F32)


def _split3(x):
    hi = x.astype(BF16)
    r1 = x - hi.astype(F32)
    mid = r1.astype(BF16)
    lo = (r1 - mid.astype(F32)).astype(BF16)
    return hi, mid, lo


def _pack3(x, width):
    hi, mid, lo = _split3(x)
    lane = lax.broadcasted_iota(jnp.int32, x.shape, 1)
    return jnp.where(lane < width, hi, jnp.where(lane < 2 * width, mid, lo))


def _shift_rows(x, j):
    rows, w = x.shape
    g = x.reshape(rows // SUBLANES, SUBLANES, w)
    r = pltpu.roll(g, SUBLANES - j, 1)
    sub = lax.broadcasted_iota(jnp.int32, (1, SUBLANES, w), 1)
    return jnp.where(sub < SUBLANES - j, r[:-1], r[1:]).reshape(rows - SUBLANES, w)


def _cparams(*sem):
    return pltpu.CompilerParams(dimension_semantics=sem, vmem_limit_bytes=VMEM_LIMIT)


def _mod_kernel(c_ref, w_ref, b_ref, o_ref):
    a = jax.nn.silu(c_ref[...]).astype(BF16)
    o_ref[...] = _dot(a, w_ref[...].astype(BF16)) + b_ref[...]


def _modulation(cond, w_mod, b_mod):
    depth, d, n = w_mod.shape
    rows = cond.shape[0]
    tn = n // 2 if (n // 2) % LANES == 0 else n
    return pl.pallas_call(
        _mod_kernel,
        grid=(depth, n // tn),
        in_specs=[
            pl.BlockSpec((rows, d), lambda l, j: (0, 0)),
            pl.BlockSpec((None, d, tn), lambda l, j: (l, 0, j)),
            pl.BlockSpec((None, 1, tn), lambda l, j: (l, 0, j)),
        ],
        out_specs=pl.BlockSpec((None, rows, tn), lambda l, j: (l, 0, j)),
        out_shape=jax.ShapeDtypeStruct((depth, rows, n), F32),
        compiler_params=_cparams("arbitrary", "arbitrary"),
        name="modulation",
    )(cond, w_mod, b_mod.reshape(depth, 1, n))


def _rope(t, cos, sin):
    width = t.shape[-1]
    lane = lax.broadcasted_iota(jnp.int32, t.shape, 1)
    first = (lane % 32) < 16
    rot = jnp.where(first, pltpu.roll(t, width - 16, 1), pltpu.roll(t, 16, 1))
    return t * cos + rot * sin


def _inproj_kernel(c_ref, x_ref, mod_ref, nw_ref, rope_ref, w_ref, xbc_ref, pf_ref, pb_ref, *, nct):
    wxbc = w_ref.at[:, W_XBC:W_XBC + SSD_XBC]
    wz = w_ref.at[:, W_Z:W_Z + SSD_WIDTH]
    wq = w_ref.at[:, W_Q:W_Q + ATT_WIDTH]
    wk = w_ref.at[:, W_K:W_K + ATT_KV_WIDTH]
    wv = w_ref.at[:, W_V:W_V + ATT_KV_WIDTH]
    wag = w_ref.at[:, W_AG:W_AG + ATT_WIDTH]
    wglu = w_ref.at[:, W_GLU:W_GLU + 2 * CONV_WIDTH]
    wcg = w_ref.at[:, W_CG:W_CG + CONV_WIDTH]
    wdt = w_ref.at[:, W_DT:W_DT + 2 * LANES]
    x = jnp.where(pl.program_id(1) < nct, c_ref[...], x_ref[...])
    y = x * lax.rsqrt(jnp.mean(x * x, axis=-1, keepdims=True) + EPS) * nw_ref[...]
    h = (y * (1.0 + mod_ref[1:2, :]) + mod_ref[0:1, :]).astype(BF16)
    cos = rope_ref[:, :LANES]
    sin = rope_ref[:, LANES:]
    reps = ATT_WIDTH // LANES
    cos_q = jnp.concatenate([cos] * reps, axis=1)
    sin_q = jnp.concatenate([sin] * reps, axis=1)
    pb_ref[:, PB_Q:PB_Q + ATT_WIDTH] = (_rope(_dot(h, wq[...]), cos_q, sin_q) * ATT_SCALE).astype(BF16)
    pb_ref[:, PB_KV:PB_KV + ATT_KV_WIDTH] = _rope(_dot(h, wk[...]), cos, sin).astype(BF16)
    pb_ref[:, PB_KV + ATT_KV_WIDTH:PB_KV + 2 * ATT_KV_WIDTH] = _dot(h, wv[...]).astype(BF16)
    pf_ref[:, PF_Z:PF_Z + SSD_WIDTH] = jax.nn.silu(_dot(h, wz[...]))
    pf_ref[:, PF_AG:PF_AG + ATT_WIDTH] = jax.nn.silu(_dot(h, wag[...]))
    pf_ref[:, PF_CG:PF_CG + CONV_WIDTH] = jax.nn.silu(_dot(h, wcg[...]))
    glu = _dot(h, wglu[...])
    pf_ref[:, PF_GLU:PF_GLU + CONV_WIDTH] = glu[:, :CONV_WIDTH]
    pf_ref[:, PF_GLU + CONV_WIDTH:PF_GLU + 2 * CONV_WIDTH] = jax.nn.sigmoid(glu[:, CONV_WIDTH:])
    pf_ref[:, PF_DT:PF_DT + 2 * LANES] = _dot(h, wdt[...])
    xbc_ref[...] = _dot(h, wxbc[...])


def _token_specs(tm, d, nct, lat_off, tile_off):
    return [
        pl.BlockSpec((None, tm, d), lambda b, i: (b, jnp.minimum(i + tile_off, nct - 1), 0)),
        pl.BlockSpec((None, tm, d), lambda b, i: (b, jnp.maximum(i + tile_off - nct, 0) + lat_off, 0)),
    ]


def _inproj(csrc, xsrc, lat_off, t, modsel, norm_w, rope_t, w_packed, layer, nct):
    bsz, _, d = xsrc.shape
    tm = ROW_TILE
    row_map = lambda b, i: (b, i, 0)
    in_specs = _token_specs(tm, d, nct, lat_off, 0) + [
        pl.BlockSpec((None, None, 3, d), lambda b, i: (b, jnp.where(i >= nct, 1, 0), 0, 0)),
        pl.BlockSpec((None, 1, d), lambda b, i: (layer, 0, 0)),
        pl.BlockSpec((tm, 2 * LANES), lambda b, i: (i, 0)),
        pl.BlockSpec((None, d, W_WIDTH), lambda b, i: (layer, 0, 0)),
    ]
    widths = (SSD_XBC, PF_WIDTH, PB_WIDTH)
    return pl.pallas_call(
        functools.partial(_inproj_kernel, nct=nct),
        grid=(bsz, t // tm),
        in_specs=in_specs,
        out_specs=[pl.BlockSpec((None, tm, w), row_map) for w in widths],
        out_shape=[jax.ShapeDtypeStruct((bsz, t, w), dt) for w, dt in zip(widths, (F32, F32, BF16))],
        compiler_params=_cparams("parallel", "arbitrary"),
        name="inproj",
    )(csrc, xsrc, modsel, norm_w, rope_t, w_packed)


def _ssd_prep(xs, bc, dt_raw, dtb, alog, fwd):
    q = SSD_CHUNK
    bb = bc[:, :SSD_GN]
    cb16 = bc[:, SSD_GN:]

    groups = [slice(g * SSD_STATE, (g + 1) * SSD_STATE) for g in range(SSD_GROUPS)]
    scores_g = [_dot_nt(cb16[:, gs], bb[:, gs]) for gs in groups]

    lane = lax.broadcasted_iota(jnp.int32, (q, LANES), 1)
    head_lane = lane < 3 * SSD_HEADS
    dt = jnp.where(head_lane, jax.nn.softplus(dt_raw + dtb), 0.0)
    da = dt * (-jnp.exp(alog))
    li = lax.broadcasted_iota(jnp.int32, (q, q), 0)
    si = lax.broadcasted_iota(jnp.int32, (q, q), 1)
    causal = (si <= li) if fwd else (si >= li)
    ones = jnp.where(causal, 1.0, 0.0).astype(BF16)
    cs3 = _dot(ones, jnp.concatenate(_split3(da), axis=1))
    cs = cs3[:, :LANES] + cs3[:, LANES:2 * LANES] + cs3[:, 2 * LANES:]
    total = jnp.sum(da, axis=0, keepdims=True)
    to_end = jnp.exp(total - cs)
    from_start = jnp.where(head_lane, jnp.exp(cs), 0.0)
    chunk_decay = jnp.where(head_lane[:SUBLANES], jnp.exp(total), 0.0)
    cs_t = cs.T

    er = lax.broadcasted_iota(jnp.int32, (LANES, SSD_WIDTH), 0)
    ec = lax.broadcasted_iota(jnp.int32, (LANES, SSD_WIDTH), 1)
    spread = jnp.where((ec // SSD_HEAD_DIM == er % SSD_HEADS) & (er < 3 * SSD_HEADS), 1.0, 0.0).astype(BF16)
    dt_e = _dot(_pack3(dt, SSD_HEADS), spread)
    stacked = jnp.concatenate([dt * to_end, from_start, chunk_decay], axis=0)
    stacked_e = _dot(_pack3(stacked, SSD_HEADS), spread)
    w_e = stacked_e[:q]
    fs_e = stacked_e[q:2 * q]
    cd_e = stacked_e[2 * q:2 * q + 1]

    xd = (xs * dt_e).astype(BF16)
    xdw = (xs * w_e).astype(BF16)
    low = lax.broadcasted_iota(jnp.int32, (q, LANES), 1) < SSD_HEAD_DIM
    zero16 = jnp.zeros((q, LANES), BF16)
    y_diag, new_states = [], []
    for g in range(SSD_GROUPS):
        gs = groups[g]
        gc = slice(g * GROUP_COLS, (g + 1) * GROUP_COLS)
        scores = scores_g[g]
        new_states.append(_dot(bb[:, gs].astype(F32).T.astype(BF16), xdw[:, gc]))
        for pr in range(GROUP_COLS // LANES):
            h0 = (g * GROUP_COLS + pr * LANES) // SSD_HEAD_DIM
            cols = slice(h0 * SSD_HEAD_DIM, h0 * SSD_HEAD_DIM + LANES)
            xpair = xd[:, cols]
            m0 = (scores * jnp.exp(jnp.where(causal, cs[:, h0:h0 + 1] - cs_t[h0:h0 + 1, :], -jnp.inf))).astype(BF16)
            m1 = (scores * jnp.exp(jnp.where(causal, cs[:, h0 + 1:h0 + 2] - cs_t[h0 + 1:h0 + 2, :], -jnp.inf))).astype(BF16)
            block_diag = jnp.concatenate([jnp.where(low, xpair, zero16), jnp.where(low, zero16, xpair)], axis=0)
            y_diag.append(_dot(jnp.concatenate([m0, m1], axis=1), block_diag))
    return y_diag, new_states, cb16, fs_e, cd_e


def _ssd_finish(prep, st_ref):
    y_diag, new_states, cb16, fs_e, cd_e = prep
    ys = []
    for g in range(SSD_GROUPS):
        gs = slice(g * SSD_STATE, (g + 1) * SSD_STATE)
        gc = slice(g * GROUP_COLS, (g + 1) * GROUP_COLS)
        entering = st_ref[g]
        y_off = _dot(cb16[:, gs], entering.astype(BF16)) * fs_e[:, gc]
        st_ref[g] = entering * cd_e[:, gc] + new_states[g]
        per_group = GROUP_COLS // LANES
        for pr in range(per_group):
            ys.append(y_diag[g * per_group + pr] + y_off[:, pr * LANES:(pr + 1) * LANES])
    return jnp.concatenate(ys, axis=1)


def _ssd_bwd_block(k, ncb, nb):
    return jnp.where(k < ncb, ncb - 1 - k, nb + ncb - 1 - k)


def _ssd_kernel(xm_ref, xp_ref, xn_ref, dt_ref, z_ref, cw_ref, cb_ref, dtb_ref, alog_ref, dvec_ref, nw_ref,
                o_ref, yb_ref, st_ref, xs_ref, bc_ref, *, ncb, nb):
    q = SSD_CHUNK
    rows = SSD_STEP
    n_sub = rows // q
    s = pl.program_id(1)

    def conv_block(cb):
        first = (cb == 0) | (cb == ncb)
        last = (cb == ncb - 1) | (cb == nb - 1)
        xext = jnp.concatenate([jnp.where(first, 0.0, xp_ref[...]), xm_ref[...],
                                jnp.where(last, 0.0, xn_ref[...])], axis=0)
        acc = cb_ref[...]
        for k in range(SSD_CONV):
            d = k - (SSD_CONV - 1) // 2
            if d == 0:
                tap = xext[SSD_HALO:SSD_HALO + rows]
            elif d > 0:
                tap = _shift_rows(xext, d)[SSD_HALO:SSD_HALO + rows]
            else:
                tap = _shift_rows(xext, SSD_HALO + d)[:rows]
            acc = acc + cw_ref[k:k + 1, :] * tap
        u = jax.nn.silu(acc)
        dst = pl.ds(pl.multiple_of(cb * rows, rows), rows)
        xs_ref[dst, :] = u[:, :SSD_WIDTH]
        bc_ref[dst, :] = u[:, SSD_WIDTH:].astype(BF16)

    def chunk_rows(blk, j):
        return pl.ds(pl.multiple_of(blk * rows + j * q, q), q)

    @pl.when((s == 0) | (s == nb))
    def _():
        st_ref[...] = jnp.zeros_like(st_ref)

    @pl.when(s <= nb - 1)
    def _():
        conv_block(_ssd_bwd_block(s, ncb, nb))

    @pl.when(s < nb)
    def _():
        blk = _ssd_bwd_block(s, ncb, nb)
        order = tuple(reversed(range(n_sub)))
        preps = [_ssd_prep(xs_ref[chunk_rows(blk, j), :], bc_ref[chunk_rows(blk, j), :], dt_ref[j * q:(j + 1) * q, :],
                           dtb_ref[...], alog_ref[...], False) for j in order]
        for j, prep in zip(order, preps):
            yb_ref[chunk_rows(blk, j), :] = _ssd_finish(prep, st_ref)

    @pl.when(s >= nb)
    def _():
        blk = s - nb
        preps = [_ssd_prep(xs_ref[chunk_rows(blk, j), :], bc_ref[chunk_rows(blk, j), :], dt_ref[j * q:(j + 1) * q, :],
                           dtb_ref[...], alog_ref[...], True) for j in range(n_sub)]
        for j, prep in enumerate(preps):
            y = _ssd_finish(prep, st_ref)
            tot = y + yb_ref[chunk_rows(blk, j), :] + dvec_ref[...] * xs_ref[chunk_rows(blk, j), :]
            gated = tot * z_ref[j * q:(j + 1) * q, :]
            outs = []
            for g in range(SSD_GROUPS):
                gc = slice(g * GROUP_COLS, (g + 1) * GROUP_COLS)
                v = gated[:, gc]
                outs.append(v * lax.rsqrt(jnp.mean(v * v, axis=-1, keepdims=True) + EPS) * nw_ref[:, gc])
            o_ref[j * q:(j + 1) * q, :] = jnp.concatenate(outs, axis=1).astype(BF16)


def _ssd(xbc, pf, conv_w, conv_b, dt_bias, a_log, dvec, norm_w, ctx_len):
    bsz, t, _ = pf.shape
    rows = SSD_STEP
    nb = t // rows
    ncb = ctx_len // rows
    hb = rows // SSD_HALO
    nhalo = t // SSD_HALO
    const2 = lambda b, s: (0, 0)
    xblk = lambda s: _ssd_bwd_block(jnp.minimum(s, nb - 1), ncb, nb)
    cur = lambda s: jnp.where(s < nb, _ssd_bwd_block(s, ncb, nb), s - nb)
    phase = lambda s: jnp.where(s < nb, 1, 0)
    fwd_blk = lambda s: jnp.maximum(s - nb, 0)
    in_specs = [
        pl.BlockSpec((None, rows, SSD_XBC), lambda b, s: (b, xblk(s), 0)),
        pl.BlockSpec((None, SSD_HALO, SSD_XBC), lambda b, s: (b, jnp.maximum(xblk(s) * hb - 1, 0), 0)),
        pl.BlockSpec((None, SSD_HALO, SSD_XBC), lambda b, s: (b, jnp.minimum(xblk(s) * hb + hb, nhalo - 1), 0)),
        pl.BlockSpec((None, rows, LANES), lambda b, s: (b, cur(s), PF_DT // LANES + phase(s))),
        pl.BlockSpec((None, rows, SSD_WIDTH), lambda b, s: (b, fwd_blk(s), PF_Z // SSD_WIDTH)),
        pl.BlockSpec((SSD_CONV, SSD_XBC), const2),
        pl.BlockSpec((1, SSD_XBC), const2),
        pl.BlockSpec((None, 1, LANES), lambda b, s: (phase(s), 0, 0)),
        pl.BlockSpec((None, 1, LANES), lambda b, s: (phase(s), 0, 0)),
        pl.BlockSpec((1, SSD_WIDTH), const2),
        pl.BlockSpec((1, SSD_WIDTH), const2),
    ]
    return pl.pallas_call(
        functools.partial(_ssd_kernel, ncb=ncb, nb=nb),
        grid=(bsz, 2 * nb),
        in_specs=in_specs,
        out_specs=pl.BlockSpec((None, rows, SSD_WIDTH), lambda b, s: (b, fwd_blk(s), 0)),
        out_shape=jax.ShapeDtypeStruct((bsz, t, SSD_WIDTH), BF16),
        scratch_shapes=[
            pltpu.VMEM((t, SSD_WIDTH), F32),
            pltpu.VMEM((SSD_GROUPS, SSD_STATE, GROUP_COLS), F32),
            pltpu.VMEM((t, SSD_WIDTH), F32),
            pltpu.VMEM((t, 2 * SSD_GN), BF16),
        ],
        compiler_params=_cparams("parallel", "arbitrary"),
        name="ssd",
    )(xbc, xbc, xbc, pf, pf, conv_w, conv_b, dt_bias, a_log, dvec, norm_w)


def _swap_halves(blk):
    half = blk.shape[-1] // 2
    return jnp.concatenate([blk[:, half:], blk[:, :half]], axis=1)


def _head_pads(blk, g, low):
    zero = jnp.zeros_like(blk)
    sw = _swap_halves(blk)
    if g == 0:
        return jnp.where(low, blk, zero), jnp.where(low, zero, sw)
    return jnp.where(low, sw, zero), jnp.where(low, zero, blk)


def _attn_kernel(q_ref, kvm_ref, kvp_ref, kvn_ref, kvc_ref, ag_ref, sink_ref, o_ref, pad_ref,
                 *, tile_off, nct, seq, ctx_len):
    qb = ATT_TILE
    blk = ATT_QBLOCK
    wk = blk + 2 * WINDOW
    tile = pl.program_id(1) + tile_off
    lat0 = (tile - nct) * qb
    low_k = lax.broadcasted_iota(jnp.int32, (1, LANES), 1) < ATT_HEAD_DIM

    kv = jnp.concatenate([kvc_ref[...], kvp_ref[...], kvm_ref[...], kvn_ref[...]], axis=0)
    for g in range(ATT_KV_HEADS):
        ka, kb = _head_pads(kv[:, :ATT_KV_WIDTH], g, low_k)
        va, vb = _head_pads(kv[:, ATT_KV_WIDTH:], g, low_k)
        pad_ref[g, 0] = ka
        pad_ref[g, 1] = kb
        pad_ref[g, 2] = va
        pad_ref[g, 3] = vb

    r = lax.broadcasted_iota(jnp.int32, (blk, wk), 0)
    col = lax.broadcasted_iota(jnp.int32, (blk, wk), 1)
    in_band = jnp.abs(col - WINDOW - r) <= WINDOW
    neg = jnp.finfo(F32).min
    rep = ATT_HEADS // ATT_KV_HEADS
    pairs_per_group = rep * ATT_HEAD_DIM // LANES
    for a in range(qb // blk):
        rows = slice(a * blk, (a + 1) * blk)
        kpos = lat0 + a * blk - WINDOW + col
        valid = in_band & (kpos >= 0) & (kpos < seq) & (lat0 >= 0)
        win0 = ctx_len + a * blk
        def operand(g, slot):
            if a == 0:
                return pad_ref[g, slot, 0:win0 + wk, :]
            return jnp.concatenate([pad_ref[g, slot, 0:ctx_len, :], pad_ref[g, slot, win0:win0 + wk, :]], axis=0)

        for g in range(ATT_KV_HEADS):
            for pr in range(pairs_per_group):
                p = g * pairs_per_group + pr
                qp = q_ref[rows, p * LANES:(p + 1) * LANES]
                acc = jnp.zeros((blk, LANES), F32)
                invs = []
                for j in range(2):
                    head = 2 * p + j
                    sink = sink_ref[head:head + 1, 0:1]
                    s = _dot_nt(qp, operand(g, j))
                    s = jnp.concatenate([s[:, :ctx_len], jnp.where(valid, s[:, ctx_len:], neg)], axis=1)
                    m = jnp.maximum(jnp.max(s, axis=-1, keepdims=True), sink)
                    pexp = jnp.exp2(s - m)
                    den = jnp.sum(pexp, axis=-1, keepdims=True) + jnp.exp2(sink - m)
                    acc = acc + _dot(pexp.astype(BF16), operand(g, 2 + j))
                    invs.append(1.0 / den)
                gate = ag_ref[rows, p * LANES:(p + 1) * LANES]
                o_ref[rows, p * LANES:(p + 1) * LANES] = (acc * jnp.where(low_k, invs[0], invs[1]) * gate).astype(BF16)


def _attention(pf, pb, sink_tab, ctx_len, tile_off):
    bsz, t, _ = pf.shape
    qb = ATT_TILE
    nct = ctx_len // qb
    hb = qb // WINDOW
    ctx_hb = ctx_len // WINDOW
    n_hb = t // WINDOW
    ntiles = t // qb - tile_off
    kvw = 2 * ATT_KV_WIDTH
    kv_col = PB_KV // kvw
    in_specs = [
        pl.BlockSpec((None, qb, ATT_WIDTH), lambda b, i: (b, i + tile_off, PB_Q // ATT_WIDTH)),
        pl.BlockSpec((None, qb, kvw), lambda b, i: (b, i + tile_off, kv_col)),
        pl.BlockSpec((None, WINDOW, kvw), lambda b, i: (b, jnp.maximum((i + tile_off) * hb - 1, ctx_hb), kv_col)),
        pl.BlockSpec((None, WINDOW, kvw), lambda b, i: (b, jnp.minimum((i + tile_off) * hb + hb, n_hb - 1), kv_col)),
        pl.BlockSpec((None, ctx_len, kvw), lambda b, i: (b, 0, kv_col)),
        pl.BlockSpec((None, qb, ATT_WIDTH), lambda b, i: (b, i + tile_off, PF_AG // ATT_WIDTH)),
        pl.BlockSpec((ATT_HEADS, LANES), lambda b, i: (0, 0)),
    ]
    return pl.pallas_call(
        functools.partial(_attn_kernel, tile_off=tile_off, nct=nct, seq=t - ctx_len, ctx_len=ctx_len),
        grid=(bsz, ntiles),
        in_specs=in_specs,
        out_specs=pl.BlockSpec((None, qb, ATT_WIDTH), lambda b, i: (b, i + tile_off, 0)),
        out_shape=jax.ShapeDtypeStruct((bsz, t, ATT_WIDTH), BF16),
        scratch_shapes=[pltpu.VMEM((ATT_KV_HEADS, 4, ctx_len + qb + 2 * WINDOW, LANES), BF16)],
        compiler_params=_cparams("parallel", "arbitrary"),
        name="attention",
    )(pb, pb, pb, pb, pb, pf, sink_tab)


def _conf_tile(gm_ref, gp_ref, gn_ref, cg_ref, dww_ref, dwb_ref, lnw_ref, lnb_ref, pww_ref, pwb_ref, hbuf, abuf,
               first, last):
    tc = CONF_TILE

    def glu(v):
        return v[:, :CONV_WIDTH] * v[:, CONV_WIDTH:]

    rows = tc + 2 * CONF_HALO
    ext = jnp.concatenate([jnp.where(first, 0.0, glu(gp_ref[...])), glu(gm_ref[...]),
                           jnp.where(last, 0.0, glu(gn_ref[...]))], axis=0)
    hbuf[0] = ext
    for j in range(1, SUBLANES):
        hbuf[j, 0:rows - SUBLANES, :] = _shift_rows(ext, j)
    base = CONF_HALO - CONV_PAD
    for r in range(tc // CONF_SUB):
        acc = jnp.broadcast_to(dwb_ref[...], (CONF_SUB // SUBLANES, SUBLANES, CONV_WIDTH))
        for k in range(CONV_KERNEL):
            blk, res = divmod(base + k, SUBLANES)
            strip = hbuf[res, pl.ds(r * CONF_SUB + blk * SUBLANES, CONF_SUB), :]
            acc = acc + dww_ref[k] * strip.reshape(CONF_SUB // SUBLANES, SUBLANES, CONV_WIDTH)
        acc = acc.reshape(CONF_SUB, CONV_WIDTH)
        xc = acc - jnp.mean(acc, axis=-1, keepdims=True)
        var = jnp.mean(xc * xc, axis=-1, keepdims=True)
        ln = xc * lax.rsqrt(var + EPS) * lnw_ref[...] + lnb_ref[...]
        abuf[pl.ds(r * CONF_SUB, CONF_SUB), :] = jax.nn.silu(ln)
    out = _dot(abuf[...].astype(BF16), pww_ref[...]) + pwb_ref[...]
    return (out * cg_ref[...]).astype(BF16)


def _mixout_kernel(c_ref, x_ref, os_ref, oa_ref, gm_ref, gp_ref, gn_ref, cg_ref, dww_ref, dwb_ref, lnw_ref, lnb_ref,
                   pww_ref, pwb_ref, mod_ref, w_ref, fnw_ref, o_ref, hbuf, abuf, *, final, nct, nt, tile_off):
    tile = pl.program_id(1) + tile_off
    first = (tile == 0) | (tile == nct)
    last = (tile == nct - 1) | (tile == nt - 1)
    a0 = SSD_WIDTH
    c0 = SSD_WIDTH + ATT_WIDTH
    u = _dot(os_ref[...], w_ref[0:a0, :]) + _dot(oa_ref[...], w_ref[a0:c0, :])
    oc = _conf_tile(gm_ref, gp_ref, gn_ref, cg_ref, dww_ref, dwb_ref, lnw_ref, lnb_ref, pww_ref, pwb_ref,
                    hbuf, abuf, first, last)
    u = u + _dot(oc, w_ref[c0:c0 + CONV_WIDTH, :])
    x = x_ref[...]
    if tile_off < nct:
        x = jnp.where(tile < nct, c_ref[...], x)
    xn = x + mod_ref[2:3, :] * u
    if final:
        xn = xn * lax.rsqrt(jnp.mean(xn * xn, axis=-1, keepdims=True) + EPS) * fnw_ref[...]
    o_ref[...] = xn


def _mixout(csrc, xsrc, lat_off, o_ssd, o_att, pf, dw_w, dw_b, ln_w, ln_b, pw_w16, pw_b, modsel, w_out16, layer,
            final_norm_w, ctx_len, final):
    bsz, t, _ = o_ssd.shape
    d = xsrc.shape[-1]
    tm = ROW_TILE
    nt = t // tm
    nct = ctx_len // tm
    tile_off = nct if final else 0
    hb = tm // CONF_HALO
    nhalo = t // CONF_HALO
    main = lambda b, i: (b, i + tile_off, 0)
    const2 = lambda b, i: (0, 0)
    glu_col = PF_GLU // (2 * CONV_WIDTH)
    in_specs = _token_specs(tm, d, nct, lat_off, tile_off) + [
        pl.BlockSpec((None, tm, SSD_WIDTH), main),
        pl.BlockSpec((None, tm, ATT_WIDTH), main),
        pl.BlockSpec((None, tm, 2 * CONV_WIDTH), lambda b, i: (b, i + tile_off, glu_col)),
        pl.BlockSpec((None, CONF_HALO, 2 * CONV_WIDTH),
                     lambda b, i: (b, jnp.maximum((i + tile_off) * hb - 1, 0), glu_col)),
        pl.BlockSpec((None, CONF_HALO, 2 * CONV_WIDTH),
                     lambda b, i: (b, jnp.minimum((i + tile_off) * hb + hb, nhalo - 1), glu_col)),
        pl.BlockSpec((None, tm, CONV_WIDTH), lambda b, i: (b, i + tile_off, PF_CG // CONV_WIDTH)),
        pl.BlockSpec((CONV_KERNEL, SUBLANES, CONV_WIDTH), lambda b, i: (0, 0, 0)),
        pl.BlockSpec((SUBLANES, CONV_WIDTH), const2),
        pl.BlockSpec((1, CONV_WIDTH), const2),
        pl.BlockSpec((1, CONV_WIDTH), const2),
        pl.BlockSpec((None, CONV_WIDTH, CONV_WIDTH), lambda b, i: (layer, 0, 0)),
        pl.BlockSpec((1, CONV_WIDTH), const2),
        pl.BlockSpec((None, None, 3, d), lambda b, i: (b, jnp.where(i + tile_off >= nct, 1, 0), 0, 0)),
        pl.BlockSpec((None,) + w_out16.shape[1:], lambda b, i: (layer, 0, 0)),
        pl.BlockSpec((1, d), const2),
    ]
    rows_out = t - tile_off * tm
    return pl.pallas_call(
        functools.partial(_mixout_kernel, final=final, nct=nct, nt=nt, tile_off=tile_off),
        grid=(bsz, rows_out // tm),
        in_specs=in_specs,
        out_specs=pl.BlockSpec((None, tm, d), lambda b, i: (b, i, 0)),
        out_shape=jax.ShapeDtypeStruct((bsz, rows_out, d), F32),
        scratch_shapes=[
            pltpu.VMEM((SUBLANES, tm + 2 * CONF_HALO, CONV_WIDTH), F32),
            pltpu.VMEM((tm, CONV_WIDTH), F32),
        ],
        compiler_params=_cparams("parallel", "arbitrary"),
        name="mixout_final" if final else "mixout",
    )(csrc, xsrc, o_ssd, o_att, pf, pf, pf, pf, dw_w, dw_b, ln_w, ln_b, pw_w16, pw_b, modsel, w_out16, final_norm_w)


def _rope_tables(seq, ctx_len):
    rows = seq // GRID_W
    row = jnp.repeat(jnp.arange(rows, dtype=F32), GRID_W)
    col = jnp.tile(jnp.arange(GRID_W, dtype=F32), rows)
    axis_dim = ATT_HEAD_DIM // 2
    inv_freq = ROPE_BASE ** (-jnp.arange(0, axis_dim, 2, dtype=F32) / axis_dim)
    ar = row[:, None] * inv_freq
    ac = col[:, None] * inv_freq
    cos = jnp.concatenate([jnp.cos(ar), jnp.cos(ar), jnp.cos(ac), jnp.cos(ac)], axis=1)
    sin = jnp.concatenate([-jnp.sin(ar), jnp.sin(ar), -jnp.sin(ac), jnp.sin(ac)], axis=1)
    reps = LANES // ATT_HEAD_DIM
    cos = jnp.concatenate([jnp.ones((ctx_len, LANES), F32), jnp.tile(cos, (1, reps))], axis=0)
    sin = jnp.concatenate([jnp.zeros((ctx_len, LANES), F32), jnp.tile(sin, (1, reps))], axis=0)
    return jnp.concatenate([cos, sin], axis=1)


def _head_lanes(a):
    rep = jnp.concatenate([a, a, a], axis=-1)
    return jnp.pad(rep, [(0, 0)] * (a.ndim - 1) + [(0, LANES - 3 * SSD_HEADS)])


def _pack_w_in(w_in):
    dt0 = SSD_XBC + SSD_WIDTH
    dt_w = w_in[..., dt0:dt0 + 2 * SSD_HEADS]
    packed = jnp.concatenate([w_in[..., :dt0], w_in[..., dt0 + 2 * SSD_HEADS:],
                              _head_lanes(dt_w[..., :SSD_HEADS]), _head_lanes(dt_w[..., SSD_HEADS:])], axis=-1)
    assert packed.shape[-1] == W_WIDTH
    return packed.astype(BF16)


def kernel(x, c, ctx, c_ctx, w_mod, b_mod, norm_w, w_in, ssd_conv_w, ssd_conv_b, ssd_dt_bias, ssd_a_log, ssd_d,
           ssd_norm_w, attn_sink, conv_dw_w, conv_dw_b, conv_ln_w, conv_ln_b, conv_pw_w, conv_pw_b, w_out,
           final_norm_w):
    bsz, seq, d = x.shape
    ctx_len = ctx.shape[1]
    depth = w_in.shape[0]
    assert ctx_len % ROW_TILE == 0 and seq % ROW_TILE == 0 and seq % GRID_W == 0
    assert ROW_TILE == ATT_TILE == CONF_TILE
    nct = ctx_len // ROW_TILE

    cond_rows = -(-(bsz + 1) // SUBLANES) * SUBLANES
    cond = jnp.zeros((cond_rows, d), F32).at[:bsz].set(c).at[bsz].set(c_ctx)
    mod = _modulation(cond, w_mod, b_mod)
    rope_t = _rope_tables(seq, ctx_len)
    csrc, xsrc, lat_off = ctx, x, 0
    w_in16 = _pack_w_in(w_in)
    w_out16 = w_out.astype(BF16)
    pw_w16 = conv_pw_w.astype(BF16)
    norm_w3 = norm_w[:, None, :]

    out = None
    for l in range(depth):
        final = l == depth - 1
        lat = mod[l, :bsz]
        ctxm = jnp.broadcast_to(mod[l, bsz], lat.shape)
        modsel = jnp.stack([ctxm, lat], axis=1).reshape(bsz, 2, 3, d)
        xbc, pf, pb = _inproj(csrc, xsrc, lat_off, ctx_len + seq, modsel, norm_w3, rope_t, w_in16, l, nct)

        o_ssd = _ssd(xbc, pf, ssd_conv_w[l], ssd_conv_b[l][None], _head_lanes(ssd_dt_bias[l])[:, None],
                     _head_lanes(ssd_a_log[l])[:, None], jnp.repeat(ssd_d[l], SSD_HEAD_DIM)[None],
                     ssd_norm_w[l][None], ctx_len)
        tile_off = nct if final else 0
        sink_tab = jnp.broadcast_to(attn_sink[l][:, None] * LOG2E, (ATT_HEADS, LANES))
        o_att = _attention(pf, pb, sink_tab, ctx_len, tile_off)
        dw_w = jnp.broadcast_to(conv_dw_w[l][:, None, :], (CONV_KERNEL, SUBLANES, CONV_WIDTH))
        dw_b = jnp.broadcast_to(conv_dw_b[l][None], (SUBLANES, CONV_WIDTH))
        out = _mixout(csrc, xsrc, lat_off, o_ssd, o_att, pf, dw_w, dw_b, conv_ln_w[l][None], conv_ln_b[l][None], pw_w16,
                      conv_pw_b[l][None], modsel, w_out16, l, final_norm_w[None], ctx_len, final)
        csrc, xsrc, lat_off = out, out, nct
    return out
```

```python
import functools

import jax
import jax.numpy as jnp
from jax import lax
from jax.experimental import pallas as pl
from jax.experimental.pallas import tpu as pltpu

F32 = jnp.float32
BF16 = jnp.bfloat16

GRID_W = 64
EPS = 1e-6

SSD_HEADS = 16
SSD_HEAD_DIM = 64
SSD_WIDTH = SSD_HEADS * SSD_HEAD_DIM
SSD_GROUPS = 2
SSD_STATE = 128
SSD_CONV = 5
SSD_CHUNK = 128
SSD_GN = SSD_GROUPS * SSD_STATE
SSD_XBC = SSD_WIDTH + 2 * SSD_GN
GROUP_COLS = SSD_WIDTH // SSD_GROUPS

ATT_HEADS = 8
ATT_KV_HEADS = 2
ATT_HEAD_DIM = 64
ATT_WIDTH = ATT_HEADS * ATT_HEAD_DIM
ATT_KV_WIDTH = ATT_KV_HEADS * ATT_HEAD_DIM
WINDOW = 128
ROPE_BASE = 10000.0
LOG2E = 1.4426950408889634
ATT_SCALE = ATT_HEAD_DIM ** -0.5 * LOG2E

CONV_WIDTH = 512
CONV_KERNEL = 31
CONV_PAD = (CONV_KERNEL - 1) // 2

LANES = 128

PF_Z = 0
PF_GLU = PF_Z + SSD_WIDTH
PF_AG = PF_GLU + 2 * CONV_WIDTH
PF_CG = PF_AG + ATT_WIDTH
PF_DT = PF_CG + CONV_WIDTH
PF_WIDTH = PF_DT + 2 * LANES
PB_Q = 0
PB_KV = PB_Q + ATT_WIDTH
PB_WIDTH = PB_KV + 2 * ATT_KV_WIDTH

W_XBC = 0
W_Z = W_XBC + SSD_XBC
W_Q = W_Z + SSD_WIDTH
W_K = W_Q + ATT_WIDTH
W_V = W_K + ATT_KV_WIDTH
W_AG = W_V + ATT_KV_WIDTH
W_GLU = W_AG + ATT_WIDTH
W_CG = W_GLU + 2 * CONV_WIDTH
W_DT = W_CG + CONV_WIDTH
W_WIDTH = W_DT + 2 * LANES
SUBLANES = 8
VMEM_LIMIT = 56 * 1024 * 1024

ROW_TILE = 256
ATT_TILE = 256
ATT_QBLOCK = 256
CONF_TILE = 256
CONF_HALO = 16
CONF_SUB = 32
SSD_HALO = SUBLANES
SSD_STEP = 2 * SSD_CHUNK


def _dot(a, b):
    return jnp.dot(a, b, preferred_element_type=F32)


def _dot_nt(a, b):
    return lax.dot_general(a, b, (((1,), (1,)), ((), ())), preferred_element_type=F32)


def _split3(x):
    hi = x.astype(BF16)
    r1 = x - hi.astype(F32)
    mid = r1.astype(BF16)
    lo = (r1 - mid.astype(F32)).astype(BF16)
    return hi, mid, lo


def _pack3(x, width):
    hi, mid, lo = _split3(x)
    lane = lax.broadcasted_iota(jnp.int32, x.shape, 1)
    return jnp.where(lane < width, hi, jnp.where(lane < 2 * width, mid, lo))


def _shift_rows(x, j):
    rows, w = x.shape
    g = x.reshape(rows // SUBLANES, SUBLANES, w)
    r = pltpu.roll(g, SUBLANES - j, 1)
    sub = lax.broadcasted_iota(jnp.int32, (1, SUBLANES, w), 1)
    return jnp.where(sub < SUBLANES - j, r[:-1], r[1:]).reshape(rows - SUBLANES, w)


def _cparams(*sem):
    return pltpu.CompilerParams(dimension_semantics=sem, vmem_limit_bytes=VMEM_LIMIT)


def _mod_kernel(c_ref, w_ref, b_ref, o_ref):
    a = jax.nn.silu(c_ref[...]).astype(BF16)
    o_ref[...] = _dot(a, w_ref[...].astype(BF16)) + b_ref[...]


def _modulation(cond, w_mod, b_mod):
    depth, d, n = w_mod.shape
    rows = cond.shape[0]
    tn = n // 2 if (n // 2) % LANES == 0 else n
    return pl.pallas_call(
        _mod_kernel,
        grid=(depth, n // tn),
        in_specs=[
            pl.BlockSpec((rows, d), lambda l, j: (0, 0)),
            pl.BlockSpec((None, d, tn), lambda l, j: (l, 0, j)),
            pl.BlockSpec((None, 1, tn), lambda l, j: (l, 0, j)),
        ],
        out_specs=pl.BlockSpec((None, rows, tn), lambda l, j: (l, 0, j)),
        out_shape=jax.ShapeDtypeStruct((depth, rows, n), F32),
        compiler_params=_cparams("arbitrary", "arbitrary"),
        name="modulation",
    )(cond, w_mod, b_mod.reshape(depth, 1, n))


def _rope(t, cos, sin):
    width = t.shape[-1]
    lane = lax.broadcasted_iota(jnp.int32, t.shape, 1)
    first = (lane % 32) < 16
    rot = jnp.where(first, pltpu.roll(t, width - 16, 1), pltpu.roll(t, 16, 1))
    return t * cos + rot * sin


def _inproj_kernel(c_ref, x_ref, mod_ref, nw_ref, rope_ref, w_ref, xbc_ref, pf_ref, pb_ref, *, nct):
    wxbc = w_ref.at[:, W_XBC:W_XBC + SSD_XBC]
    wz = w_ref.at[:, W_Z:W_Z + SSD_WIDTH]
    wq = w_ref.at[:, W_Q:W_Q + ATT_WIDTH]
    wk = w_ref.at[:, W_K:W_K + ATT_KV_WIDTH]
    wv = w_ref.at[:, W_V:W_V + ATT_KV_WIDTH]
    wag = w_ref.at[:, W_AG:W_AG + ATT_WIDTH]
    wglu = w_ref.at[:, W_GLU:W_GLU + 2 * CONV_WIDTH]
    wcg = w_ref.at[:, W_CG:W_CG + CONV_WIDTH]
    wdt = w_ref.at[:, W_DT:W_DT + 2 * LANES]
    x = jnp.where(pl.program_id(1) < nct, c_ref[...], x_ref[...])
    y = x * lax.rsqrt(jnp.mean(x * x, axis=-1, keepdims=True) + EPS) * nw_ref[...]
    h = (y * (1.0 + mod_ref[1:2, :]) + mod_ref[0:1, :]).astype(BF16)
    cos = rope_ref[:, :LANES]
    sin = rope_ref[:, LANES:]
    reps = ATT_WIDTH // LANES
    cos_q = jnp.concatenate([cos] * reps, axis=1)
    sin_q = jnp.concatenate([sin] * reps, axis=1)
    pb_ref[:, PB_Q:PB_Q + ATT_WIDTH] = (_rope(_dot(h, wq[...]), cos_q, sin_q) * ATT_SCALE).astype(BF16)
    pb_ref[:, PB_KV:PB_KV + ATT_KV_WIDTH] = _rope(_dot(h, wk[...]), cos, sin).astype(BF16)
    pb_ref[:, PB_KV + ATT_KV_WIDTH:PB_KV + 2 * ATT_KV_WIDTH] = _dot(h, wv[...]).astype(BF16)
    pf_ref[:, PF_Z:PF_Z + SSD_WIDTH] = jax.nn.silu(_dot(h, wz[...]))
    pf_ref[:, PF_AG:PF_AG + ATT_WIDTH] = jax.nn.silu(_dot(h, wag[...]))
    pf_ref[:, PF_CG:PF_CG + CONV_WIDTH] = jax.nn.silu(_dot(h, wcg[...]))
    glu = _dot(h, wglu[...])
    pf_ref[:, PF_GLU:PF_GLU + CONV_WIDTH] = glu[:, :CONV_WIDTH]
    pf_ref[:, PF_GLU + CONV_WIDTH:PF_GLU + 2 * CONV_WIDTH] = jax.nn.sigmoid(glu[:, CONV_WIDTH:])
    pf_ref[:, PF_DT:PF_DT + 2 * LANES] = _dot(h, wdt[...])
    xbc_ref[...] = _dot(h, wxbc[...])


def _token_specs(tm, d, nct, lat_off, tile_off):
    return [
        pl.BlockSpec((None, tm, d), lambda b, i: (b, jnp.minimum(i + tile_off, nct - 1), 0)),
        pl.BlockSpec((None, tm, d), lambda b, i: (b, jnp.maximum(i + tile_off - nct, 0) + lat_off, 0)),
    ]


def _inproj(csrc, xsrc, lat_off, t, modsel, norm_w, rope_t, w_packed, layer, nct):
    bsz, _, d = xsrc.shape
    tm = ROW_TILE
    row_map = lambda b, i: (b, i, 0)
    in_specs = _token_specs(tm, d, nct, lat_off, 0) + [
        pl.BlockSpec((None, None, 3, d), lambda b, i: (b, jnp.where(i >= nct, 1, 0), 0, 0)),
        pl.BlockSpec((None, 1, d), lambda b, i: (layer, 0, 0)),
        pl.BlockSpec((tm, 2 * LANES), lambda b, i: (i, 0)),
        pl.BlockSpec((None, d, W_WIDTH), lambda b, i: (layer, 0, 0)),
    ]
    widths = (SSD_XBC, PF_WIDTH, PB_WIDTH)
    return pl.pallas_call(
        functools.partial(_inproj_kernel, nct=nct),
        grid=(bsz, t // tm),
        in_specs=in_specs,
        out_specs=[pl.BlockSpec((None, tm, w), row_map) for w in widths],
        out_shape=[jax.ShapeDtypeStruct((bsz, t, w), dt) for w, dt in zip(widths, (F32, F32, BF16))],
        compiler_params=_cparams("parallel", "arbitrary"),
        name="inproj",
    )(csrc, xsrc, modsel, norm_w, rope_t, w_packed)


def _ssd_prep(xs, bc, dt_raw, dtb, alog, fwd):
    q = SSD_CHUNK
    bb = bc[:, :SSD_GN]
    cb16 = bc[:, SSD_GN:]

    groups = [slice(g * SSD_STATE, (g + 1) * SSD_STATE) for g in range(SSD_GROUPS)]
    scores_g = [_dot_nt(cb16[:, gs], bb[:, gs]) for gs in groups]

    lane = lax.broadcasted_iota(jnp.int32, (q, LANES), 1)
    head_lane = lane < 3 * SSD_HEADS
    dt = jnp.where(head_lane, jax.nn.softplus(dt_raw + dtb), 0.0)
    da = dt * (-jnp.exp(alog))
    li = lax.broadcasted_iota(jnp.int32, (q, q), 0)
    si = lax.broadcasted_iota(jnp.int32, (q, q), 1)
    causal = (si <= li) if fwd else (si >= li)
    ones = jnp.where(causal, 1.0, 0.0).astype(BF16)
    cs3 = _dot(ones, jnp.concatenate(_split3(da), axis=1))
    cs = cs3[:, :LANES] + cs3[:, LANES:2 * LANES] + cs3[:, 2 * LANES:]
    total = jnp.sum(da, axis=0, keepdims=True)
    to_end = jnp.exp(total - cs)
    from_start = jnp.where(head_lane, jnp.exp(cs), 0.0)
    chunk_decay = jnp.where(head_lane[:SUBLANES], jnp.exp(total), 0.0)
    cs_t = cs.T

    er = lax.broadcasted_iota(jnp.int32, (LANES, SSD_WIDTH), 0)
    ec = lax.broadcasted_iota(jnp.int32, (LANES, SSD_WIDTH), 1)
    spread = jnp.where((ec // SSD_HEAD_DIM == er % SSD_HEADS) & (er < 3 * SSD_HEADS), 1.0, 0.0).astype(BF16)
    dt_e = _dot(_pack3(dt, SSD_HEADS), spread)
    stacked = jnp.concatenate([dt * to_end, from_start, chunk_decay], axis=0)
    stacked_e = _dot(_pack3(stacked, SSD_HEADS), spread)
    w_e = stacked_e[:q]
    fs_e = stacked_e[q:2 * q]
    cd_e = stacked_e[2 * q:2 * q + 1]

    xd = (xs * dt_e).astype(BF16)
    xdw = (xs * w_e).astype(BF16)
    low = lax.broadcasted_iota(jnp.int32, (q, LANES), 1) < SSD_HEAD_DIM
    zero16 = jnp.zeros((q, LANES), BF16)
    y_diag, new_states = [], []
    for g in range(SSD_GROUPS):
        gs = groups[g]
        gc = slice(g * GROUP_COLS, (g + 1) * GROUP_COLS)
        scores = scores_g[g]
        new_states.append(_dot(bb[:, gs].astype(F32).T.astype(BF16), xdw[:, gc]))
        for pr in range(GROUP_COLS // LANES):
            h0 = (g * GROUP_COLS + pr * LANES) // SSD_HEAD_DIM
            cols = slice(h0 * SSD_HEAD_DIM, h0 * SSD_HEAD_DIM + LANES)
            xpair = xd[:, cols]
            m0 = (scores * jnp.exp(jnp.where(causal, cs[:, h0:h0 + 1] - cs_t[h0:h0 + 1, :], -jnp.inf))).astype(BF16)
            m1 = (scores * jnp.exp(jnp.where(causal, cs[:, h0 + 1:h0 + 2] - cs_t[h0 + 1:h0 + 2, :], -jnp.inf))).astype(BF16)
            block_diag = jnp.concatenate([jnp.where(low, xpair, zero16), jnp.where(low, zero16, xpair)], axis=0)
            y_diag.append(_dot(jnp.concatenate([m0, m1], axis=1), block_diag))
    return y_diag, new_states, cb16, fs_e, cd_e


def _ssd_finish(prep, st_ref):
    y_diag, new_states, cb16, fs_e, cd_e = prep
    ys = []
    for g in range(SSD_GROUPS):
        gs = slice(g * SSD_STATE, (g + 1) * SSD_STATE)
        gc = slice(g * GROUP_COLS, (g + 1) * GROUP_COLS)
        entering = st_ref[g]
        y_off = _dot(cb16[:, gs], entering.astype(BF16)) * fs_e[:, gc]
        st_ref[g] = entering * cd_e[:, gc] + new_states[g]
        per_group = GROUP_COLS // LANES
        for pr in range(per_group):
            ys.append(y_diag[g * per_group + pr] + y_off[:, pr * LANES:(pr + 1) * LANES])
    return jnp.concatenate(ys, axis=1)


def _ssd_bwd_block(k, ncb, nb):
    return jnp.where(k < ncb, ncb - 1 - k, nb + ncb - 1 - k)


def _ssd_kernel(xm_ref, xp_ref, xn_ref, dt_ref, z_ref, cw_ref, cb_ref, dtb_ref, alog_ref, dvec_ref, nw_ref,
                o_ref, yb_ref, st_ref, xs_ref, bc_ref, *, ncb, nb):
    q = SSD_CHUNK
    rows = SSD_STEP
    n_sub = rows // q
    s = pl.program_id(1)

    def conv_block(cb):
        first = (cb == 0) | (cb == ncb)
        last = (cb == ncb - 1) | (cb == nb - 1)
        xext = jnp.concatenate([jnp.where(first, 0.0, xp_ref[...]), xm_ref[...],
                                jnp.where(last, 0.0, xn_ref[...])], axis=0)
        acc = cb_ref[...]
        for k in range(SSD_CONV):
            d = k - (SSD_CONV - 1) // 2
            if d == 0:
                tap = xext[SSD_HALO:SSD_HALO + rows]
            elif d > 0:
                tap = _shift_rows(xext, d)[SSD_HALO:SSD_HALO + rows]
            else:
                tap = _shift_rows(xext, SSD_HALO + d)[:rows]
            acc = acc + cw_ref[k:k + 1, :] * tap
        u = jax.nn.silu(acc)
        dst = pl.ds(pl.multiple_of(cb * rows, rows), rows)
        xs_ref[dst, :] = u[:, :SSD_WIDTH]
        bc_ref[dst, :] = u[:, SSD_WIDTH:].astype(BF16)

    def chunk_rows(blk, j):
        return pl.ds(pl.multiple_of(blk * rows + j * q, q), q)

    @pl.when((s == 0) | (s == nb))
    def _():
        st_ref[...] = jnp.zeros_like(st_ref)

    @pl.when(s <= nb - 1)
    def _():
        conv_block(_ssd_bwd_block(s, ncb, nb))

    @pl.when(s < nb)
    def _():
        blk = _ssd_bwd_block(s, ncb, nb)
        order = tuple(reversed(range(n_sub)))
        preps = [_ssd_prep(xs_ref[chunk_rows(blk, j), :], bc_ref[chunk_rows(blk, j), :], dt_ref[j * q:(j + 1) * q, :],
                           dtb_ref[...], alog_ref[...], False) for j in order]
        for j, prep in zip(order, preps):
            yb_ref[chunk_rows(blk, j), :] = _ssd_finish(prep, st_ref)

    @pl.when(s >= nb)
    def _():
        blk = s - nb
        preps = [_ssd_prep(xs_ref[chunk_rows(blk, j), :], bc_ref[chunk_rows(blk, j), :], dt_ref[j * q:(j + 1) * q, :],
                           dtb_ref[...], alog_ref[...], True) for j in range(n_sub)]
        for j, prep in enumerate(preps):
            y = _ssd_finish(prep, st_ref)
            tot = y + yb_ref[chunk_rows(blk, j), :] + dvec_ref[...] * xs_ref[chunk_rows(blk, j), :]
            gated = tot * z_ref[j * q:(j + 1) * q, :]
            outs = []
            for g in range(SSD_GROUPS):
                gc = slice(g * GROUP_COLS, (g + 1) * GROUP_COLS)
                v = gated[:, gc]
                outs.append(v * lax.rsqrt(jnp.mean(v * v, axis=-1, keepdims=True) + EPS) * nw_ref[:, gc])
            o_ref[j * q:(j + 1) * q, :] = jnp.concatenate(outs, axis=1).astype(BF16)


def _ssd(xbc, pf, conv_w, conv_b, dt_bias, a_log, dvec, norm_w, ctx_len):
    bsz, t, _ = pf.shape
    rows = SSD_STEP
    nb = t // rows
    ncb = ctx_len // rows
    hb = rows // SSD_HALO
    nhalo = t // SSD_HALO
    const2 = lambda b, s: (0, 0)
    xblk = lambda s: _ssd_bwd_block(jnp.minimum(s, nb - 1), ncb, nb)
    cur = lambda s: jnp.where(s < nb, _ssd_bwd_block(s, ncb, nb), s - nb)
    phase = lambda s: jnp.where(s < nb, 1, 0)
    fwd_blk = lambda s: jnp.maximum(s - nb, 0)
    in_specs = [
        pl.BlockSpec((None, rows, SSD_XBC), lambda b, s: (b, xblk(s), 0)),
        pl.BlockSpec((None, SSD_HALO, SSD_XBC), lambda b, s: (b, jnp.maximum(xblk(s) * hb - 1, 0), 0)),
        pl.BlockSpec((None, SSD_HALO, SSD_XBC), lambda b, s: (b, jnp.minimum(xblk(s) * hb + hb, nhalo - 1), 0)),
        pl.BlockSpec((None, rows, LANES), lambda b, s: (b, cur(s), PF_DT // LANES + phase(s))),
        pl.BlockSpec((None, rows, SSD_WIDTH), lambda b, s: (b, fwd_blk(s), PF_Z // SSD_WIDTH)),
        pl.BlockSpec((SSD_CONV, SSD_XBC), const2),
        pl.BlockSpec((1, SSD_XBC), const2),
        pl.BlockSpec((None, 1, LANES), lambda b, s: (phase(s), 0, 0)),
        pl.BlockSpec((None, 1, LANES), lambda b, s: (phase(s), 0, 0)),
        pl.BlockSpec((1, SSD_WIDTH), const2),
        pl.BlockSpec((1, SSD_WIDTH), const2),
    ]
    return pl.pallas_call(
        functools.partial(_ssd_kernel, ncb=ncb, nb=nb),
        grid=(bsz, 2 * nb),
        in_specs=in_specs,
        out_specs=pl.BlockSpec((None, rows, SSD_WIDTH), lambda b, s: (b, fwd_blk(s), 0)),
        out_shape=jax.ShapeDtypeStruct((bsz, t, SSD_WIDTH), BF16),
        scratch_shapes=[
            pltpu.VMEM((t, SSD_WIDTH), F32),
            pltpu.VMEM((SSD_GROUPS, SSD_STATE, GROUP_COLS), F32),
            pltpu.VMEM((t, SSD_WIDTH), F32),
            pltpu.VMEM((t, 2 * SSD_GN), BF16),
        ],
        compiler_params=_cparams("parallel", "arbitrary"),
        name="ssd",
    )(xbc, xbc, xbc, pf, pf, conv_w, conv_b, dt_bias, a_log, dvec, norm_w)


def _swap_halves(blk):
    half = blk.shape[-1] // 2
    return jnp.concatenate([blk[:, half:], blk[:, :half]], axis=1)


def _head_pads(blk, g, low):
    zero = jnp.zeros_like(blk)
    sw = _swap_halves(blk)
    if g == 0:
        return jnp.where(low, blk, zero), jnp.where(low, zero, sw)
    return jnp.where(low, sw, zero), jnp.where(low, zero, blk)


def _attn_kernel(q_ref, kvm_ref, kvp_ref, kvn_ref, kvc_ref, ag_ref, sink_ref, o_ref, pad_ref,
                 *, tile_off, nct, seq, ctx_len):
    qb = ATT_TILE
    blk = ATT_QBLOCK
    wk = blk + 2 * WINDOW
    tile = pl.program_id(1) + tile_off
    lat0 = (tile - nct) * qb
    low_k = lax.broadcasted_iota(jnp.int32, (1, LANES), 1) < ATT_HEAD_DIM

    kv = jnp.concatenate([kvc_ref[...], kvp_ref[...], kvm_ref[...], kvn_ref[...]], axis=0)
    for g in range(ATT_KV_HEADS):
        ka, kb = _head_pads(kv[:, :ATT_KV_WIDTH], g, low_k)
        va, vb = _head_pads(kv[:, ATT_KV_WIDTH:], g, low_k)
        pad_ref[g, 0] = ka
        pad_ref[g, 1] = kb
        pad_ref[g, 2] = va
        pad_ref[g, 3] = vb

    r = lax.broadcasted_iota(jnp.int32, (blk, wk), 0)
    col = lax.broadcasted_iota(jnp.int32, (blk, wk), 1)
    in_band = jnp.abs(col - WINDOW - r) <= WINDOW
    neg = jnp.finfo(F32).min
    rep = ATT_HEADS // ATT_KV_HEADS
    pairs_per_group = rep * ATT_HEAD_DIM // LANES
    for a in range(qb // blk):
        rows = slice(a * blk, (a + 1) * blk)
        kpos = lat0 + a * blk - WINDOW + col
        valid = in_band & (kpos >= 0) & (kpos < seq) & (lat0 >= 0)
        win0 = ctx_len + a * blk
        def operand(g, slot):
            if a == 0:
                return pad_ref[g, slot, 0:win0 + wk, :]
            return jnp.concatenate([pad_ref[g, slot, 0:ctx_len, :], pad_ref[g, slot, win0:win0 + wk, :]], axis=0)

        for g in range(ATT_KV_HEADS):
            for pr in range(pairs_per_group):
                p = g * pairs_per_group + pr
                qp = q_ref[rows, p * LANES:(p + 1) * LANES]
                acc = jnp.zeros((blk, LANES), F32)
                invs = []
                for j in range(2):
                    head = 2 * p + j
                    sink = sink_ref[head:head + 1, 0:1]
                    s = _dot_nt(qp, operand(g, j))
                    s = jnp.concatenate([s[:, :ctx_len], jnp.where(valid, s[:, ctx_len:], neg)], axis=1)
                    m = jnp.maximum(jnp.max(s, axis=-1, keepdims=True), sink)
                    pexp = jnp.exp2(s - m)
                    den = jnp.sum(pexp, axis=-1, keepdims=True) + jnp.exp2(sink - m)
                    acc = acc + _dot(pexp.astype(BF16), operand(g, 2 + j))
                    invs.append(1.0 / den)
                gate = ag_ref[rows, p * LANES:(p + 1) * LANES]
                o_ref[rows, p * LANES:(p + 1) * LANES] = (acc * jnp.where(low_k, invs[0], invs[1]) * gate).astype(BF16)


def _attention(pf, pb, sink_tab, ctx_len, tile_off):
    bsz, t, _ = pf.shape
    qb = ATT_TILE
    nct = ctx_len // qb
    hb = qb // WINDOW
    ctx_hb = ctx_len // WINDOW
    n_hb = t // WINDOW
    ntiles = t // qb - tile_off
    kvw = 2 * ATT_KV_WIDTH
    kv_col = PB_KV // kvw
    in_specs = [
        pl.BlockSpec((None, qb, ATT_WIDTH), lambda b, i: (b, i + tile_off, PB_Q // ATT_WIDTH)),
        pl.BlockSpec((None, qb, kvw), lambda b, i: (b, i + tile_off, kv_col)),
        pl.BlockSpec((None, WINDOW, kvw), lambda b, i: (b, jnp.maximum((i + tile_off) * hb - 1, ctx_hb), kv_col)),
        pl.BlockSpec((None, WINDOW, kvw), lambda b, i: (b, jnp.minimum((i + tile_off) * hb + hb, n_hb - 1), kv_col)),
        pl.BlockSpec((None, ctx_len, kvw), lambda b, i: (b, 0, kv_col)),
        pl.BlockSpec((None, qb, ATT_WIDTH), lambda b, i: (b, i + tile_off, PF_AG // ATT_WIDTH)),
        pl.BlockSpec((ATT_HEADS, LANES), lambda b, i: (0, 0)),
    ]
    return pl.pallas_call(
        functools.partial(_attn_kernel, tile_off=tile_off, nct=nct, seq=t - ctx_len, ctx_len=ctx_len),
        grid=(bsz, ntiles),
        in_specs=in_specs,
        out_specs=pl.BlockSpec((None, qb, ATT_WIDTH), lambda b, i: (b, i, 0)),
        out_shape=jax.ShapeDtypeStruct((bsz, ntiles * qb, ATT_WIDTH), BF16),
        scratch_shapes=[pltpu.VMEM((ATT_KV_HEADS, 4, ctx_len + qb + 2 * WINDOW, LANES), BF16)],
        compiler_params=_cparams("parallel", "arbitrary"),
        name="attention",
    )(pb, pb, pb, pb, pb, pf, sink_tab)


def _conf_tile(gm_ref, gp_ref, gn_ref, cg_ref, dww_ref, dwb_ref, lnw_ref, lnb_ref, pww_ref, pwb_ref, hbuf, abuf,
               first, last):
    tc = CONF_TILE

    def glu(v):
        return v[:, :CONV_WIDTH] * v[:, CONV_WIDTH:]

    rows = tc + 2 * CONF_HALO
    ext = jnp.concatenate([jnp.where(first, 0.0, glu(gp_ref[...])), glu(gm_ref[...]),
                           jnp.where(last, 0.0, glu(gn_ref[...]))], axis=0)
    hbuf[0] = ext
    for j in range(1, SUBLANES):
        hbuf[j, 0:rows - SUBLANES, :] = _shift_rows(ext, j)
    base = CONF_HALO - CONV_PAD
    for r in range(tc // CONF_SUB):
        acc = jnp.broadcast_to(dwb_ref[...], (CONF_SUB // SUBLANES, SUBLANES, CONV_WIDTH))
        for k in range(CONV_KERNEL):
            blk, res = divmod(base + k, SUBLANES)
            strip = hbuf[res, pl.ds(r * CONF_SUB + blk * SUBLANES, CONF_SUB), :]
            acc = acc + dww_ref[k] * strip.reshape(CONF_SUB // SUBLANES, SUBLANES, CONV_WIDTH)
        acc = acc.reshape(CONF_SUB, CONV_WIDTH)
        xc = acc - jnp.mean(acc, axis=-1, keepdims=True)
        var = jnp.mean(xc * xc, axis=-1, keepdims=True)
        ln = xc * lax.rsqrt(var + EPS) * lnw_ref[...] + lnb_ref[...]
        abuf[pl.ds(r * CONF_SUB, CONF_SUB), :] = jax.nn.silu(ln)
    out = _dot(abuf[...].astype(BF16), pww_ref[...]) + pwb_ref[...]
    return (out * cg_ref[...]).astype(BF16)


def _mixout_kernel(c_ref, x_ref, os_ref, oa_ref, gm_ref, gp_ref, gn_ref, cg_ref, dww_ref, dwb_ref, lnw_ref, lnb_ref,
                   pww_ref, pwb_ref, mod_ref, w_ref, fnw_ref, o_ref, hbuf, abuf, *, final, nct, nt, tile_off):
    tile = pl.program_id(1) + tile_off
    first = (tile == 0) | (tile == nct)
    last = (tile == nct - 1) | (tile == nt - 1)
    a0 = SSD_WIDTH
    c0 = SSD_WIDTH + ATT_WIDTH
    u = _dot(os_ref[...], w_ref[0:a0, :]) + _dot(oa_ref[...], w_ref[a0:c0, :])
    oc = _conf_tile(gm_ref, gp_ref, gn_ref, cg_ref, dww_ref, dwb_ref, lnw_ref, lnb_ref, pww_ref, pwb_ref,
                    hbuf, abuf, first, last)
    u = u + _dot(oc, w_ref[c0:c0 + CONV_WIDTH, :])
    x = x_ref[...]
    if tile_off < nct:
        x = jnp.where(tile < nct, c_ref[...], x)
    xn = x + mod_ref[2:3, :] * u
    if final:
        xn = xn * lax.rsqrt(jnp.mean(xn * xn, axis=-1, keepdims=True) + EPS) * fnw_ref[...]
    o_ref[...] = xn


def _mixout(csrc, xsrc, lat_off, o_ssd, o_att, pf, dw_w, dw_b, ln_w, ln_b, pw_w16, pw_b, modsel, w_out16, layer,
            final_norm_w, ctx_len, final):
    bsz, t, _ = o_ssd.shape
    d = xsrc.shape[-1]
    tm = ROW_TILE
    nt = t // tm
    nct = ctx_len // tm
    tile_off = nct if final else 0
    hb = tm // CONF_HALO
    nhalo = t // CONF_HALO
    main = lambda b, i: (b, i + tile_off, 0)
    const2 = lambda b, i: (0, 0)
    glu_col = PF_GLU // (2 * CONV_WIDTH)
    in_specs = _token_specs(tm, d, nct, lat_off, tile_off) + [
        pl.BlockSpec((None, tm, SSD_WIDTH), main),
        pl.BlockSpec((None, tm, ATT_WIDTH), lambda b, i: (b, i, 0)),
        pl.BlockSpec((None, tm, 2 * CONV_WIDTH), lambda b, i: (b, i + tile_off, glu_col)),
        pl.BlockSpec((None, CONF_HALO, 2 * CONV_WIDTH),
                     lambda b, i: (b, jnp.maximum((i + tile_off) * hb - 1, 0), glu_col)),
        pl.BlockSpec((None, CONF_HALO, 2 * CONV_WIDTH),
                     lambda b, i: (b, jnp.minimum((i + tile_off) * hb + hb, nhalo - 1), glu_col)),
        pl.BlockSpec((None, tm, CONV_WIDTH), lambda b, i: (b, i + tile_off, PF_CG // CONV_WIDTH)),
        pl.BlockSpec((CONV_KERNEL, SUBLANES, CONV_WIDTH), lambda b, i: (0, 0, 0)),
        pl.BlockSpec((SUBLANES, CONV_WIDTH), const2),
        pl.BlockSpec((1, CONV_WIDTH), const2),
        pl.BlockSpec((1, CONV_WIDTH), const2),
        pl.BlockSpec((None, CONV_WIDTH, CONV_WIDTH), lambda b, i: (layer, 0, 0)),
        pl.BlockSpec((1, CONV_WIDTH), const2),
        pl.BlockSpec((None, None, 3, d), lambda b, i: (b, jnp.where(i + tile_off >= nct, 1, 0), 0, 0)),
        pl.BlockSpec((None,) + w_out16.shape[1:], lambda b, i: (layer, 0, 0)),
        pl.BlockSpec((1, d), const2),
    ]
    rows_out = t - tile_off * tm
    return pl.pallas_call(
        functools.partial(_mixout_kernel, final=final, nct=nct, nt=nt, tile_off=tile_off),
        grid=(bsz, rows_out // tm),
        in_specs=in_specs,
        out_specs=pl.BlockSpec((None, tm, d), lambda b, i: (b, i, 0)),
        out_shape=jax.ShapeDtypeStruct((bsz, rows_out, d), F32),
        scratch_shapes=[
            pltpu.VMEM((SUBLANES, tm + 2 * CONF_HALO, CONV_WIDTH), F32),
            pltpu.VMEM((tm, CONV_WIDTH), F32),
        ],
        compiler_params=_cparams("parallel", "arbitrary"),
        name="mixout_final" if final else "mixout",
    )(csrc, xsrc, o_ssd, o_att, pf, pf, pf, pf, dw_w, dw_b, ln_w, ln_b, pw_w16, pw_b, modsel, w_out16, final_norm_w)


def _rope_tables(seq, ctx_len):
    rows = seq // GRID_W
    row = jnp.repeat(jnp.arange(rows, dtype=F32), GRID_W)
    col = jnp.tile(jnp.arange(GRID_W, dtype=F32), rows)
    axis_dim = ATT_HEAD_DIM // 2
    inv_freq = ROPE_BASE ** (-jnp.arange(0, axis_dim, 2, dtype=F32) / axis_dim)
    ar = row[:, None] * inv_freq
    ac = col[:, None] * inv_freq
    cos = jnp.concatenate([jnp.cos(ar), jnp.cos(ar), jnp.cos(ac), jnp.cos(ac)], axis=1)
    sin = jnp.concatenate([-jnp.sin(ar), jnp.sin(ar), -jnp.sin(ac), jnp.sin(ac)], axis=1)
    reps = LANES // ATT_HEAD_DIM
    cos = jnp.concatenate([jnp.ones((ctx_len, LANES), F32), jnp.tile(cos, (1, reps))], axis=0)
    sin = jnp.concatenate([jnp.zeros((ctx_len, LANES), F32), jnp.tile(sin, (1, reps))], axis=0)
    return jnp.concatenate([cos, sin], axis=1)


def _head_lanes(a):
    rep = jnp.concatenate([a, a, a], axis=-1)
    return jnp.pad(rep, [(0, 0)] * (a.ndim - 1) + [(0, LANES - 3 * SSD_HEADS)])


def _pack_w_in(w_in):
    dt0 = SSD_XBC + SSD_WIDTH
    dt_w = w_in[..., dt0:dt0 + 2 * SSD_HEADS]
    packed = jnp.concatenate([w_in[..., :dt0], w_in[..., dt0 + 2 * SSD_HEADS:],
                              _head_lanes(dt_w[..., :SSD_HEADS]), _head_lanes(dt_w[..., SSD_HEADS:])], axis=-1)
    assert packed.shape[-1] == W_WIDTH
    return packed.astype(BF16)


def kernel(x, c, ctx, c_ctx, w_mod, b_mod, norm_w, w_in, ssd_conv_w, ssd_conv_b, ssd_dt_bias, ssd_a_log, ssd_d,
           ssd_norm_w, attn_sink, conv_dw_w, conv_dw_b, conv_ln_w, conv_ln_b, conv_pw_w, conv_pw_b, w_out,
           final_norm_w):
    bsz, seq, d = x.shape
    ctx_len = ctx.shape[1]
    depth = w_in.shape[0]
    assert ctx_len % ROW_TILE == 0 and seq % ROW_TILE == 0 and seq % GRID_W == 0
    assert ROW_TILE == ATT_TILE == CONF_TILE
    nct = ctx_len // ROW_TILE

    cond_rows = -(-(bsz + 1) // SUBLANES) * SUBLANES
    cond = jnp.zeros((cond_rows, d), F32).at[:bsz].set(c).at[bsz].set(c_ctx)
    mod = _modulation(cond, w_mod, b_mod)
    rope_t = _rope_tables(seq, ctx_len)
    csrc, xsrc, lat_off = ctx, x, 0
    w_in16 = _pack_w_in(w_in)
    w_out16 = w_out.astype(BF16)
    pw_w16 = conv_pw_w.astype(BF16)
    norm_w3 = norm_w[:, None, :]

    out = None
    for l in range(depth):
        final = l == depth - 1
        lat = mod[l, :bsz]
        ctxm = jnp.broadcast_to(mod[l, bsz], lat.shape)
        modsel = jnp.stack([ctxm, lat], axis=1).reshape(bsz, 2, 3, d)
        xbc, pf, pb = _inproj(csrc, xsrc, lat_off, ctx_len + seq, modsel, norm_w3, rope_t, w_in16, l, nct)

        o_ssd = _ssd(xbc, pf, ssd_conv_w[l], ssd_conv_b[l][None], _head_lanes(ssd_dt_bias[l])[:, None],
                     _head_lanes(ssd_a_log[l])[:, None], jnp.repeat(ssd_d[l], SSD_HEAD_DIM)[None],
                     ssd_norm_w[l][None], ctx_len)
        tile_off = nct if final else 0
        sink_tab = jnp.broadcast_to(attn_sink[l][:, None] * LOG2E, (ATT_HEADS, LANES))
        o_att = _attention(pf, pb, sink_tab, ctx_len, tile_off)
        dw_w = jnp.broadcast_to(conv_dw_w[l][:, None, :], (CONV_KERNEL, SUBLANES, CONV_WIDTH))
        dw_b = jnp.broadcast_to(conv_dw_b[l][None], (SUBLANES, CONV_WIDTH))
        out = _mixout(csrc, xsrc, lat_off, o_ssd, o_att, pf, dw_w, dw_b, conv_ln_w[l][None], conv_ln_b[l][None], pw_w16,
                      conv_pw_b[l][None], modsel, w_out16, l, final_norm_w[None], ctx_len, final)
        csrc, xsrc, lat_off = out, out, nct
    return out
```

```python
import functools

import jax
import jax.numpy as jnp
from jax import lax
from jax.experimental import pallas as pl
from jax.experimental.pallas import tpu as pltpu

F32 = jnp.float32
BF16 = jnp.bfloat16

GRID_W = 64
EPS = 1e-6

SSD_HEADS = 16
SSD_HEAD_DIM = 64
SSD_WIDTH = SSD_HEADS * SSD_HEAD_DIM
SSD_GROUPS = 2
SSD_STATE = 128
SSD_CONV = 5
SSD_CHUNK = 128
SSD_GN = SSD_GROUPS * SSD_STATE
SSD_XBC = SSD_WIDTH + 2 * SSD_GN
GROUP_COLS = SSD_WIDTH // SSD_GROUPS

ATT_HEADS = 8
ATT_KV_HEADS = 2
ATT_HEAD_DIM = 64
ATT_WIDTH = ATT_HEADS * ATT_HEAD_DIM
ATT_KV_WIDTH = ATT_KV_HEADS * ATT_HEAD_DIM
WINDOW = 128
ROPE_BASE = 10000.0
LOG2E = 1.4426950408889634
ATT_SCALE = ATT_HEAD_DIM ** -0.5 * LOG2E

CONV_WIDTH = 512
CONV_KERNEL = 31
CONV_PAD = (CONV_KERNEL - 1) // 2

LANES = 128

PF_Z = 0
PF_GLU = PF_Z + SSD_WIDTH
PF_AG = PF_GLU + 2 * CONV_WIDTH
PF_CG = PF_AG + ATT_WIDTH
PF_DT = PF_CG + CONV_WIDTH
PF_WIDTH = PF_DT + 2 * LANES
PB_Q = 0
PB_KV = PB_Q + ATT_WIDTH
PB_WIDTH = PB_KV + 2 * ATT_KV_WIDTH

W_XBC = 0
W_Z = W_XBC + SSD_XBC
W_Q = W_Z + SSD_WIDTH
W_K = W_Q + ATT_WIDTH
W_V = W_K + ATT_KV_WIDTH
W_AG = W_V + ATT_KV_WIDTH
W_GLU = W_AG + ATT_WIDTH
W_CG = W_GLU + 2 * CONV_WIDTH
W_DT = W_CG + CONV_WIDTH
W_WIDTH = W_DT + 2 * LANES
SUBLANES = 8
VMEM_LIMIT = 56 * 1024 * 1024

ROW_TILE = 256
ATT_TILE = 256
ATT_QBLOCK = 256
CONF_TILE = 256
CONF_HALO = 16
CONF_SUB = 32
SSD_HALO = SUBLANES
SSD_STEP = 2 * SSD_CHUNK


def _dot(a, b):
    return jnp.dot(a, b, preferred_element_type=F32)


def _dot_nt(a, b):
    return lax.dot_general(a, b, (((1,), (1,)), ((), ())), preferred_element_type=F32)


def _split3(x):
    hi = x.astype(BF16)
    r1 = x - hi.astype(F32)
    mid = r1.astype(BF16)
    lo = (r1 - mid.astype(F32)).astype(BF16)
    return hi, mid, lo


def _pack3(x, width):
    hi, mid, lo = _split3(x)
    lane = lax.broadcasted_iota(jnp.int32, x.shape, 1)
    return jnp.where(lane < width, hi, jnp.where(lane < 2 * width, mid, lo))


def _shift_rows(x, j):
    rows, w = x.shape
    g = x.reshape(rows // SUBLANES, SUBLANES, w)
    r = pltpu.roll(g, SUBLANES - j, 1)
    sub = lax.broadcasted_iota(jnp.int32, (1, SUBLANES, w), 1)
    return jnp.where(sub < SUBLANES - j, r[:-1], r[1:]).reshape(rows - SUBLANES, w)


def _cparams(*sem):
    return pltpu.CompilerParams(dimension_semantics=sem, vmem_limit_bytes=VMEM_LIMIT)


def _mod_kernel(c_ref, w_ref, b_ref, o_ref):
    a = jax.nn.silu(c_ref[...]).astype(BF16)
    o_ref[...] = _dot(a, w_ref[...].astype(BF16)) + b_ref[...]


def _modulation(cond, w_mod, b_mod):
    depth, d, n = w_mod.shape
    rows = cond.shape[0]
    tn = n // 2 if (n // 2) % LANES == 0 else n
    return pl.pallas_call(
        _mod_kernel,
        grid=(depth, n // tn),
        in_specs=[
            pl.BlockSpec((rows, d), lambda l, j: (0, 0)),
            pl.BlockSpec((None, d, tn), lambda l, j: (l, 0, j)),
            pl.BlockSpec((None, 1, tn), lambda l, j: (l, 0, j)),
        ],
        out_specs=pl.BlockSpec((None, rows, tn), lambda l, j: (l, 0, j)),
        out_shape=jax.ShapeDtypeStruct((depth, rows, n), F32),
        compiler_params=_cparams("arbitrary", "arbitrary"),
        name="modulation",
    )(cond, w_mod, b_mod.reshape(depth, 1, n))


def _rope(t, cos, sin):
    width = t.shape[-1]
    lane = lax.broadcasted_iota(jnp.int32, t.shape, 1)
    first = (lane % 32) < 16
    rot = jnp.where(first, pltpu.roll(t, width - 16, 1), pltpu.roll(t, 16, 1))
    return t * cos + rot * sin


def _inproj_kernel(c_ref, x_ref, mod_ref, nw_ref, rope_ref, w_ref, xbc_ref, pf_ref, pb_ref, *, nct):
    wxbc = w_ref.at[:, W_XBC:W_XBC + SSD_XBC]
    wz = w_ref.at[:, W_Z:W_Z + SSD_WIDTH]
    wq = w_ref.at[:, W_Q:W_Q + ATT_WIDTH]
    wk = w_ref.at[:, W_K:W_K + ATT_KV_WIDTH]
    wv = w_ref.at[:, W_V:W_V + ATT_KV_WIDTH]
    wag = w_ref.at[:, W_AG:W_AG + ATT_WIDTH]
    wglu = w_ref.at[:, W_GLU:W_GLU + 2 * CONV_WIDTH]
    wcg = w_ref.at[:, W_CG:W_CG + CONV_WIDTH]
    wdt = w_ref.at[:, W_DT:W_DT + 2 * LANES]
    x = jnp.where(pl.program_id(1) < nct, c_ref[...], x_ref[...])
    y = x * lax.rsqrt(jnp.mean(x * x, axis=-1, keepdims=True) + EPS) * nw_ref[...]
    h = (y * (1.0 + mod_ref[1:2, :]) + mod_ref[0:1, :]).astype(BF16)
    cos = rope_ref[:, :LANES]
    sin = rope_ref[:, LANES:]
    reps = ATT_WIDTH // LANES
    cos_q = jnp.concatenate([cos] * reps, axis=1)
    sin_q = jnp.concatenate([sin] * reps, axis=1)
    pb_ref[:, PB_Q:PB_Q + ATT_WIDTH] = (_rope(_dot(h, wq[...]), cos_q, sin_q) * ATT_SCALE).astype(BF16)
    pb_ref[:, PB_KV:PB_KV + ATT_KV_WIDTH] = _rope(_dot(h, wk[...]), cos, sin).astype(BF16)
    pb_ref[:, PB_KV + ATT_KV_WIDTH:PB_KV + 2 * ATT_KV_WIDTH] = _dot(h, wv[...]).astype(BF16)
    pf_ref[:, PF_Z:PF_Z + SSD_WIDTH] = jax.nn.silu(_dot(h, wz[...]))
    pf_ref[:, PF_AG:PF_AG + ATT_WIDTH] = jax.nn.silu(_dot(h, wag[...]))
    pf_ref[:, PF_CG:PF_CG + CONV_WIDTH] = jax.nn.silu(_dot(h, wcg[...]))
    glu = _dot(h, wglu[...])
    pf_ref[:, PF_GLU:PF_GLU + CONV_WIDTH] = glu[:, :CONV_WIDTH]
    pf_ref[:, PF_GLU + CONV_WIDTH:PF_GLU + 2 * CONV_WIDTH] = jax.nn.sigmoid(glu[:, CONV_WIDTH:])
    pf_ref[:, PF_DT:PF_DT + 2 * LANES] = _dot(h, wdt[...])
    xbc_ref[...] = _dot(h, wxbc[...])


def _token_specs(tm, d, nct, lat_off, tile_off):
    return [
        pl.BlockSpec((None, tm, d), lambda b, i: (b, jnp.minimum(i + tile_off, nct - 1), 0)),
        pl.BlockSpec((None, tm, d), lambda b, i: (b, jnp.maximum(i + tile_off - nct, 0) + lat_off, 0)),
    ]


def _inproj(csrc, xsrc, lat_off, t, modsel, norm_w, rope_t, w_packed, layer, nct):
    bsz, _, d = xsrc.shape
    tm = ROW_TILE
    row_map = lambda b, i: (b, i, 0)
    in_specs = _token_specs(tm, d, nct, lat_off, 0) + [
        pl.BlockSpec((None, None, 3, d), lambda b, i: (b, jnp.where(i >= nct, 1, 0), 0, 0)),
        pl.BlockSpec((None, 1, d), lambda b, i: (layer, 0, 0)),
        pl.BlockSpec((tm, 2 * LANES), lambda b, i: (i, 0)),
        pl.BlockSpec((None, d, W_WIDTH), lambda b, i: (layer, 0, 0)),
    ]
    widths = (SSD_XBC, PF_WIDTH, PB_WIDTH)
    return pl.pallas_call(
        functools.partial(_inproj_kernel, nct=nct),
        grid=(bsz, t // tm),
        in_specs=in_specs,
        out_specs=[pl.BlockSpec((None, tm, w), row_map) for w in widths],
        out_shape=[jax.ShapeDtypeStruct((bsz, t, w), dt) for w, dt in zip(widths, (F32, F32, BF16))],
        compiler_params=_cparams("parallel", "arbitrary"),
        name="inproj",
    )(csrc, xsrc, modsel, norm_w, rope_t, w_packed)


def _ssd_prep(xs, bc, dt_raw, dtb, alog, fwd):
    q = SSD_CHUNK
    bb = bc[:, :SSD_GN]
    cb16 = bc[:, SSD_GN:]

    groups = [slice(g * SSD_STATE, (g + 1) * SSD_STATE) for g in range(SSD_GROUPS)]
    scores_g = [_dot_nt(cb16[:, gs], bb[:, gs]) for gs in groups]

    lane = lax.broadcasted_iota(jnp.int32, (q, LANES), 1)
    head_lane = lane < 3 * SSD_HEADS
    dt = jnp.where(head_lane, jax.nn.softplus(dt_raw + dtb), 0.0)
    da = dt * (-jnp.exp(alog))
    li = lax.broadcasted_iota(jnp.int32, (q, q), 0)
    si = lax.broadcasted_iota(jnp.int32, (q, q), 1)
    causal = (si <= li) if fwd else (si >= li)
    ones = jnp.where(causal, 1.0, 0.0).astype(BF16)
    cs3 = _dot(ones, jnp.concatenate(_split3(da), axis=1))
    cs = cs3[:, :LANES] + cs3[:, LANES:2 * LANES] + cs3[:, 2 * LANES:]
    total = jnp.sum(da, axis=0, keepdims=True)
    to_end = jnp.exp(total - cs)
    from_start = jnp.where(head_lane, jnp.exp(cs), 0.0)
    chunk_decay = jnp.where(head_lane[:SUBLANES], jnp.exp(total), 0.0)
    cs_t = cs.T

    er = lax.broadcasted_iota(jnp.int32, (LANES, SSD_WIDTH), 0)
    ec = lax.broadcasted_iota(jnp.int32, (LANES, SSD_WIDTH), 1)
    spread = jnp.where((ec // SSD_HEAD_DIM == er % SSD_HEADS) & (er < 3 * SSD_HEADS), 1.0, 0.0).astype(BF16)
    dt_e = _dot(_pack3(dt, SSD_HEADS), spread)
    stacked = jnp.concatenate([dt * to_end, from_start, chunk_decay], axis=0)
    stacked_e = _dot(_pack3(stacked, SSD_HEADS), spread)
    w_e = stacked_e[:q]
    fs_e = stacked_e[q:2 * q]
    cd_e = stacked_e[2 * q:2 * q + 1]

    xd = (xs * dt_e).astype(BF16)
    xdw = (xs * w_e).astype(BF16)
    low = lax.broadcasted_iota(jnp.int32, (q, LANES), 1) < SSD_HEAD_DIM
    zero16 = jnp.zeros((q, LANES), BF16)
    y_diag, new_states = [], []
    for g in range(SSD_GROUPS):
        gs = groups[g]
        gc = slice(g * GROUP_COLS, (g + 1) * GROUP_COLS)
        scores = scores_g[g]
        new_states.append(_dot(bb[:, gs].astype(F32).T.astype(BF16), xdw[:, gc]))
        for pr in range(GROUP_COLS // LANES):
            h0 = (g * GROUP_COLS + pr * LANES) // SSD_HEAD_DIM
            cols = slice(h0 * SSD_HEAD_DIM, h0 * SSD_HEAD_DIM + LANES)
            xpair = xd[:, cols]
            m0 = (scores * jnp.exp(jnp.where(causal, cs[:, h0:h0 + 1] - cs_t[h0:h0 + 1, :], -jnp.inf))).astype(BF16)
            m1 = (scores * jnp.exp(jnp.where(causal, cs[:, h0 + 1:h0 + 2] - cs_t[h0 + 1:h0 + 2, :], -jnp.inf))).astype(BF16)
            block_diag = jnp.concatenate([jnp.where(low, xpair, zero16), jnp.where(low, zero16, xpair)], axis=0)
            y_diag.append(_dot(jnp.concatenate([m0, m1], axis=1), block_diag))
    return y_diag, new_states, cb16, fs_e, cd_e


def _ssd_finish(prep, st_ref):
    y_diag, new_states, cb16, fs_e, cd_e = prep
    ys = []
    for g in range(SSD_GROUPS):
        gs = slice(g * SSD_STATE, (g + 1) * SSD_STATE)
        gc = slice(g * GROUP_COLS, (g + 1) * GROUP_COLS)
        entering = st_ref[g]
        y_off = _dot(cb16[:, gs], entering.astype(BF16)) * fs_e[:, gc]
        st_ref[g] = entering * cd_e[:, gc] + new_states[g]
        per_group = GROUP_COLS // LANES
        for pr in range(per_group):
            ys.append(y_diag[g * per_group + pr] + y_off[:, pr * LANES:(pr + 1) * LANES])
    return jnp.concatenate(ys, axis=1)


def _ssd_bwd_block(k, ncb, nb):
    return jnp.where(k < ncb, ncb - 1 - k, nb + ncb - 1 - k)


def _ssd_kernel(xm_ref, xp_ref, xn_ref, dt_ref, z_ref, cw_ref, cb_ref, dtb_ref, alog_ref, dvec_ref, nw_ref,
                o_ref, yb_ref, st_ref, xs_ref, bc_ref, *, ncb, nb):
    q = SSD_CHUNK
    rows = SSD_STEP
    n_sub = rows // q
    s = pl.program_id(1)

    def conv_block(cb):
        first = (cb == 0) | (cb == ncb)
        last = (cb == ncb - 1) | (cb == nb - 1)
        xext = jnp.concatenate([jnp.where(first, 0.0, xp_ref[...]), xm_ref[...],
                                jnp.where(last, 0.0, xn_ref[...])], axis=0)
        acc = cb_ref[...]
        for k in range(SSD_CONV):
            d = k - (SSD_CONV - 1) // 2
            if d == 0:
                tap = xext[SSD_HALO:SSD_HALO + rows]
            elif d > 0:
                tap = _shift_rows(xext, d)[SSD_HALO:SSD_HALO + rows]
            else:
                tap = _shift_rows(xext, SSD_HALO + d)[:rows]
            acc = acc + cw_ref[k:k + 1, :] * tap
        u = jax.nn.silu(acc)
        dst = pl.ds(pl.multiple_of(cb * rows, rows), rows)
        xs_ref[dst, :] = u[:, :SSD_WIDTH]
        bc_ref[dst, :] = u[:, SSD_WIDTH:].astype(BF16)

    def chunk_rows(blk, j):
        return pl.ds(pl.multiple_of(blk * rows + j * q, q), q)

    @pl.when((s == 0) | (s == nb))
    def _():
        st_ref[...] = jnp.zeros_like(st_ref)

    @pl.when(s <= nb - 1)
    def _():
        conv_block(_ssd_bwd_block(s, ncb, nb))

    @pl.when(s < nb)
    def _():
        blk = _ssd_bwd_block(s, ncb, nb)
        order = tuple(reversed(range(n_sub)))
        preps = [_ssd_prep(xs_ref[chunk_rows(blk, j), :], bc_ref[chunk_rows(blk, j), :], dt_ref[j * q:(j + 1) * q, :],
                           dtb_ref[...], alog_ref[...], False) for j in order]
        for j, prep in zip(order, preps):
            yb_ref[chunk_rows(blk, j), :] = _ssd_finish(prep, st_ref)

    @pl.when(s >= nb)
    def _():
        blk = s - nb
        preps = [_ssd_prep(xs_ref[chunk_rows(blk, j), :], bc_ref[chunk_rows(blk, j), :], dt_ref[j * q:(j + 1) * q, :],
                           dtb_ref[...], alog_ref[...], True) for j in range(n_sub)]
        for j, prep in enumerate(preps):
            y = _ssd_finish(prep, st_ref)
            tot = y + yb_ref[chunk_rows(blk, j), :] + dvec_ref[...] * xs_ref[chunk_rows(blk, j), :]
            gated = tot * z_ref[j * q:(j + 1) * q, :]
            outs = []
            for g in range(SSD_GROUPS):
                gc = slice(g * GROUP_COLS, (g + 1) * GROUP_COLS)
                v = gated[:, gc]
                outs.append(v * lax.rsqrt(jnp.mean(v * v, axis=-1, keepdims=True) + EPS) * nw_ref[:, gc])
            o_ref[j * q:(j + 1) * q, :] = jnp.concatenate(outs, axis=1).astype(BF16)


def _ssd(xbc, pf, conv_w, conv_b, dt_bias, a_log, dvec, norm_w, ctx_len):
    bsz, t, _ = pf.shape
    rows = SSD_STEP
    nb = t // rows
    ncb = ctx_len // rows
    hb = rows // SSD_HALO
    nhalo = t // SSD_HALO
    const2 = lambda b, s: (0, 0)
    xblk = lambda s: _ssd_bwd_block(jnp.minimum(s, nb - 1), ncb, nb)
    cur = lambda s: jnp.where(s < nb, _ssd_bwd_block(s, ncb, nb), s - nb)
    phase = lambda s: jnp.where(s < nb, 1, 0)
    fwd_blk = lambda s: jnp.maximum(s - nb, 0)
    in_specs = [
        pl.BlockSpec((None, rows, SSD_XBC), lambda b, s: (b, xblk(s), 0)),
        pl.BlockSpec((None, SSD_HALO, SSD_XBC), lambda b, s: (b, jnp.maximum(xblk(s) * hb - 1, 0), 0)),
        pl.BlockSpec((None, SSD_HALO, SSD_XBC), lambda b, s: (b, jnp.minimum(xblk(s) * hb + hb, nhalo - 1), 0)),
        pl.BlockSpec((None, rows, LANES), lambda b, s: (b, cur(s), PF_DT // LANES + phase(s))),
        pl.BlockSpec((None, rows, SSD_WIDTH), lambda b, s: (b, fwd_blk(s), PF_Z // SSD_WIDTH)),
        pl.BlockSpec((SSD_CONV, SSD_XBC), const2),
        pl.BlockSpec((1, SSD_XBC), const2),
        pl.BlockSpec((None, 1, LANES), lambda b, s: (phase(s), 0, 0)),
        pl.BlockSpec((None, 1, LANES), lambda b, s: (phase(s), 0, 0)),
        pl.BlockSpec((1, SSD_WIDTH), const2),
        pl.BlockSpec((1, SSD_WIDTH), const2),
    ]
    return pl.pallas_call(
        functools.partial(_ssd_kernel, ncb=ncb, nb=nb),
        grid=(bsz, 2 * nb),
        in_specs=in_specs,
        out_specs=pl.BlockSpec((None, rows, SSD_WIDTH), lambda b, s: (b, fwd_blk(s), 0)),
        out_shape=jax.ShapeDtypeStruct((bsz, t, SSD_WIDTH), BF16),
        scratch_shapes=[
            pltpu.VMEM((t, SSD_WIDTH), F32),
            pltpu.VMEM((SSD_GROUPS, SSD_STATE, GROUP_COLS), F32),
            pltpu.VMEM((t, SSD_WIDTH), F32),
            pltpu.VMEM((t, 2 * SSD_GN), BF16),
        ],
        compiler_params=_cparams("parallel", "arbitrary"),
        name="ssd",
    )(xbc, xbc, xbc, pf, pf, conv_w, conv_b, dt_bias, a_log, dvec, norm_w)


def _swap_halves(blk):
    half = blk.shape[-1] // 2
    return jnp.concatenate([blk[:, half:], blk[:, :half]], axis=1)


def _head_pads(blk, g, low):
    zero = jnp.zeros_like(blk)
    sw = _swap_halves(blk)
    if g == 0:
        return jnp.where(low, blk, zero), jnp.where(low, zero, sw)
    return jnp.where(low, sw, zero), jnp.where(low, zero, blk)


def _attn_kernel(q_ref, kvm_ref, kvp_ref, kvn_ref, kvc_ref, ag_ref, sink_ref, o_ref, pad_ref,
                 *, tile_off, nct, seq, ctx_len):
    qb = ATT_TILE
    blk = ATT_QBLOCK
    wk = blk + 2 * WINDOW
    tile = pl.program_id(1) + tile_off
    lat0 = (tile - nct) * qb
    low_k = lax.broadcasted_iota(jnp.int32, (1, LANES), 1) < ATT_HEAD_DIM

    kv = jnp.concatenate([kvc_ref[...], kvp_ref[...], kvm_ref[...], kvn_ref[...]], axis=0)
    for g in range(ATT_KV_HEADS):
        ka, kb = _head_pads(kv[:, :ATT_KV_WIDTH], g, low_k)
        va, vb = _head_pads(kv[:, ATT_KV_WIDTH:], g, low_k)
        pad_ref[g, 0] = ka
        pad_ref[g, 1] = kb
        pad_ref[g, 2] = va
        pad_ref[g, 3] = vb

    r = lax.broadcasted_iota(jnp.int32, (blk, wk), 0)
    col = lax.broadcasted_iota(jnp.int32, (blk, wk), 1)
    in_band = jnp.abs(col - WINDOW - r) <= WINDOW
    neg = jnp.finfo(F32).min
    rep = ATT_HEADS // ATT_KV_HEADS
    pairs_per_group = rep * ATT_HEAD_DIM // LANES
    for a in range(qb // blk):
        rows = slice(a * blk, (a + 1) * blk)
        kpos = lat0 + a * blk - WINDOW + col
        valid = in_band & (kpos >= 0) & (kpos < seq) & (lat0 >= 0)
        win0 = ctx_len + a * blk
        def operand(g, slot):
            if a == 0:
                return pad_ref[g, slot, 0:win0 + wk, :]
            return jnp.concatenate([pad_ref[g, slot, 0:ctx_len, :], pad_ref[g, slot, win0:win0 + wk, :]], axis=0)

        for g in range(ATT_KV_HEADS):
            for pr in range(pairs_per_group):
                p = g * pairs_per_group + pr
                qp = q_ref[rows, p * LANES:(p + 1) * LANES]
                acc = jnp.zeros((blk, LANES), F32)
                invs = []
                for j in range(2):
                    head = 2 * p + j
                    sink = sink_ref[head:head + 1, 0:1]
                    s = _dot_nt(qp, operand(g, j))
                    s = jnp.concatenate([s[:, :ctx_len], jnp.where(valid, s[:, ctx_len:], neg)], axis=1)
                    m = jnp.maximum(jnp.max(s, axis=-1, keepdims=True), sink)
                    pexp = jnp.exp2(s - m)
                    den = jnp.sum(pexp, axis=-1, keepdims=True) + jnp.exp2(sink - m)
                    acc = acc + _dot(pexp.astype(BF16), operand(g, 2 + j))
                    invs.append(1.0 / den)
                gate = ag_ref[rows, p * LANES:(p + 1) * LANES]
                o_ref[rows, p * LANES:(p + 1) * LANES] = (acc * jnp.where(low_k, invs[0], invs[1]) * gate).astype(BF16)


def _attention(pf, pb, sink_tab, ctx_len, tile_off):
    bsz, t, _ = pf.shape
    qb = ATT_TILE
    nct = ctx_len // qb
    hb = qb // WINDOW
    ctx_hb = ctx_len // WINDOW
    n_hb = t // WINDOW
    ntiles = t // qb - tile_off
    kvw = 2 * ATT_KV_WIDTH
    kv_col = PB_KV // kvw
    in_specs = [
        pl.BlockSpec((None, qb, ATT_WIDTH), lambda b, i: (b, i + tile_off, PB_Q // ATT_WIDTH)),
        pl.BlockSpec((None, qb, kvw), lambda b, i: (b, i + tile_off, kv_col)),
        pl.BlockSpec((None, WINDOW, kvw), lambda b, i: (b, jnp.maximum((i + tile_off) * hb - 1, ctx_hb), kv_col)),
        pl.BlockSpec((None, WINDOW, kvw), lambda b, i: (b, jnp.minimum((i + tile_off) * hb + hb, n_hb - 1), kv_col)),
        pl.BlockSpec((None, ctx_len, kvw), lambda b, i: (b, 0, kv_col)),
        pl.BlockSpec((None, qb, ATT_WIDTH), lambda b, i: (b, i + tile_off, PF_AG // ATT_WIDTH)),
        pl.BlockSpec((ATT_HEADS, LANES), lambda b, i: (0, 0)),
    ]
    return pl.pallas_call(
        functools.partial(_attn_kernel, tile_off=tile_off, nct=nct, seq=t - ctx_len, ctx_len=ctx_len),
        grid=(bsz, ntiles),
        in_specs=in_specs,
        out_specs=pl.BlockSpec((None, qb, ATT_WIDTH), lambda b, i: (b, i, 0)),
        out_shape=jax.ShapeDtypeStruct((bsz, ntiles * qb, ATT_WIDTH), BF16),
        scratch_shapes=[pltpu.VMEM((ATT_KV_HEADS, 4, ctx_len + qb + 2 * WINDOW, LANES), BF16)],
        compiler_params=_cparams("parallel", "arbitrary"),
        name="attention",
    )(pb, pb, pb, pb, pb, pf, sink_tab)


def _conf_tile(gm_ref, gp_ref, gn_ref, cg_ref, dww_ref, dwb_ref, lnw_ref, lnb_ref, pww_ref, pwb_ref, hbuf, abuf,
               first, last):
    tc = CONF_TILE

    def glu(v):
        return v[:, :CONV_WIDTH] * v[:, CONV_WIDTH:]

    rows = tc + 2 * CONF_HALO
    ext = jnp.concatenate([jnp.where(first, 0.0, glu(gp_ref[...])), glu(gm_ref[...]),
                           jnp.where(last, 0.0, glu(gn_ref[...]))], axis=0)
    hbuf[0] = ext
    for j in range(1, SUBLANES):
        hbuf[j, 0:rows - SUBLANES, :] = _shift_rows(ext, j)
    base = CONF_HALO - CONV_PAD
    for r in range(tc // CONF_SUB):
        acc = jnp.broadcast_to(dwb_ref[...], (CONF_SUB // SUBLANES, SUBLANES, CONV_WIDTH))
        for k in range(CONV_KERNEL):
            blk, res = divmod(base + k, SUBLANES)
            strip = hbuf[res, pl.ds(r * CONF_SUB + blk * SUBLANES, CONF_SUB), :]
            acc = acc + dww_ref[k] * strip.reshape(CONF_SUB // SUBLANES, SUBLANES, CONV_WIDTH)
        acc = acc.reshape(CONF_SUB, CONV_WIDTH)
        xc = acc - jnp.mean(acc, axis=-1, keepdims=True)
        var = jnp.mean(xc * xc, axis=-1, keepdims=True)
        ln = xc * lax.rsqrt(var + EPS) * lnw_ref[...] + lnb_ref[...]
        abuf[pl.ds(r * CONF_SUB, CONF_SUB), :] = jax.nn.silu(ln)
    out = _dot(abuf[...].astype(BF16), pww_ref[...]) + pwb_ref[...]
    return (out * cg_ref[...]).astype(BF16)


def _mixout_kernel(c_ref, x_ref, os_ref, oa_ref, gm_ref, gp_ref, gn_ref, cg_ref, dww_ref, dwb_ref, lnw_ref, lnb_ref,
                   pww_ref, pwb_ref, mod_ref, w_ref, fnw_ref, o_ref, hbuf, abuf, *, final, nct, nt, tile_off):
    tile = pl.program_id(1) + tile_off
    first = (tile == 0) | (tile == nct)
    last = (tile == nct - 1) | (tile == nt - 1)
    a0 = SSD_WIDTH
    c0 = SSD_WIDTH + ATT_WIDTH
    u = _dot(os_ref[...], w_ref[0:a0, :]) + _dot(oa_ref[...], w_ref[a0:c0, :])
    oc = _conf_tile(gm_ref, gp_ref, gn_ref, cg_ref, dww_ref, dwb_ref, lnw_ref, lnb_ref, pww_ref, pwb_ref,
                    hbuf, abuf, first, last)
    u = u + _dot(oc, w_ref[c0:c0 + CONV_WIDTH, :])
    x = x_ref[...]
    if tile_off < nct:
        x = jnp.where(tile < nct, c_ref[...], x)
    xn = x + mod_ref[2:3, :] * u
    if final:
        xn = xn * lax.rsqrt(jnp.mean(xn * xn, axis=-1, keepdims=True) + EPS) * fnw_ref[...]
    o_ref[...] = xn


def _mixout(csrc, xsrc, lat_off, o_ssd, o_att, pf, dw_w, dw_b, ln_w, ln_b, pw_w16, pw_b, modsel, w_out16, layer,
            final_norm_w, ctx_len, final):
    bsz, t, _ = o_ssd.shape
    d = xsrc.shape[-1]
    tm = ROW_TILE
    nt = t // tm
    nct = ctx_len // tm
    tile_off = nct if final else 0
    hb = tm // CONF_HALO
    nhalo = t // CONF_HALO
    main = lambda b, i: (b, i + tile_off, 0)
    const2 = lambda b, i: (0, 0)
    glu_col = PF_GLU // (2 * CONV_WIDTH)
    in_specs = _token_specs(tm, d, nct, lat_off, tile_off) + [
        pl.BlockSpec((None, tm, SSD_WIDTH), main),
        pl.BlockSpec((None, tm, ATT_WIDTH), lambda b, i: (b, i, 0)),
        pl.BlockSpec((None, tm, 2 * CONV_WIDTH), lambda b, i: (b, i + tile_off, glu_col)),
        pl.BlockSpec((None, CONF_HALO, 2 * CONV_WIDTH),
                     lambda b, i: (b, jnp.maximum((i + tile_off) * hb - 1, 0), glu_col)),
        pl.BlockSpec((None, CONF_HALO, 2 * CONV_WIDTH),
                     lambda b, i: (b, jnp.minimum((i + tile_off) * hb + hb, nhalo - 1), glu_col)),
        pl.BlockSpec((None, tm, CONV_WIDTH), lambda b, i: (b, i + tile_off, PF_CG // CONV_WIDTH)),
        pl.BlockSpec((CONV_KERNEL, SUBLANES, CONV_WIDTH), lambda b, i: (0, 0, 0)),
        pl.BlockSpec((SUBLANES, CONV_WIDTH), const2),
        pl.BlockSpec((1, CONV_WIDTH), const2),
        pl.BlockSpec((1, CONV_WIDTH), const2),
        pl.BlockSpec((None, CONV_WIDTH, CONV_WIDTH), lambda b, i: (layer, 0, 0)),
        pl.BlockSpec((1, CONV_WIDTH), const2),
        pl.BlockSpec((None, None, 3, d), lambda b, i: (b, jnp.where(i + tile_off >= nct, 1, 0), 0, 0)),
        pl.BlockSpec((None,) + w_out16.shape[1:], lambda b, i: (layer, 0, 0)),
        pl.BlockSpec((1, d), const2),
    ]
    rows_out = t - tile_off * tm
    return pl.pallas_call(
        functools.partial(_mixout_kernel, final=final, nct=nct, nt=nt, tile_off=tile_off),
        grid=(bsz, rows_out // tm),
        in_specs=in_specs,
        out_specs=pl.BlockSpec((None, tm, d), lambda b, i: (b, i, 0)),
        out_shape=jax.ShapeDtypeStruct((bsz, rows_out, d), F32),
        scratch_shapes=[
            pltpu.VMEM((SUBLANES, tm + 2 * CONF_HALO, CONV_WIDTH), F32),
            pltpu.VMEM((tm, CONV_WIDTH), F32),
        ],
        compiler_params=_cparams("parallel", "arbitrary"),
        name="mixout_final" if final else "mixout",
    )(csrc, xsrc, o_ssd, o_att, pf, pf, pf, pf, dw_w, dw_b, ln_w, ln_b, pw_w16, pw_b, modsel, w_out16, final_norm_w)


def _rope_tables(seq, ctx_len):
    rows = seq // GRID_W
    row = jnp.repeat(jnp.arange(rows, dtype=F32), GRID_W)
    col = jnp.tile(jnp.arange(GRID_W, dtype=F32), rows)
    axis_dim = ATT_HEAD_DIM // 2
    inv_freq = ROPE_BASE ** (-jnp.arange(0, axis_dim, 2, dtype=F32) / axis_dim)
    ar = row[:, None] * inv_freq
    ac = col[:, None] * inv_freq
    cos = jnp.concatenate([jnp.cos(ar), jnp.cos(ar), jnp.cos(ac), jnp.cos(ac)], axis=1)
    sin = jnp.concatenate([-jnp.sin(ar), jnp.sin(ar), -jnp.sin(ac), jnp.sin(ac)], axis=1)
    reps = LANES // ATT_HEAD_DIM
    cos = jnp.concatenate([jnp.ones((ctx_len, LANES), F32), jnp.tile(cos, (1, reps))], axis=0)
    sin = jnp.concatenate([jnp.zeros((ctx_len, LANES), F32), jnp.tile(sin, (1, reps))], axis=0)
    return jnp.concatenate([cos, sin], axis=1)


def _head_lanes(a):
    rep = jnp.concatenate([a, a, a], axis=-1)
    return jnp.pad(rep, [(0, 0)] * (a.ndim - 1) + [(0, LANES - 3 * SSD_HEADS)])


def _pack_w_in(w_in):
    dt0 = SSD_XBC + SSD_WIDTH
    w16 = w_in.astype(BF16)
    dt_w = w16[..., dt0:dt0 + 2 * SSD_HEADS]
    packed = jnp.concatenate([w16[..., :dt0], w16[..., dt0 + 2 * SSD_HEADS:],
                              _head_lanes(dt_w[..., :SSD_HEADS]), _head_lanes(dt_w[..., SSD_HEADS:])], axis=-1)
    assert packed.shape[-1] == W_WIDTH
    return packed


def kernel(x, c, ctx, c_ctx, w_mod, b_mod, norm_w, w_in, ssd_conv_w, ssd_conv_b, ssd_dt_bias, ssd_a_log, ssd_d,
           ssd_norm_w, attn_sink, conv_dw_w, conv_dw_b, conv_ln_w, conv_ln_b, conv_pw_w, conv_pw_b, w_out,
           final_norm_w):
    bsz, seq, d = x.shape
    ctx_len = ctx.shape[1]
    depth = w_in.shape[0]
    assert ctx_len % ROW_TILE == 0 and seq % ROW_TILE == 0 and seq % GRID_W == 0
    assert ROW_TILE == ATT_TILE == CONF_TILE
    nct = ctx_len // ROW_TILE

    cond_rows = -(-(bsz + 1) // SUBLANES) * SUBLANES
    cond = jnp.zeros((cond_rows, d), F32).at[:bsz].set(c).at[bsz].set(c_ctx)
    mod = _modulation(cond, w_mod, b_mod)
    rope_t = _rope_tables(seq, ctx_len)
    csrc, xsrc, lat_off = ctx, x, 0
    w_in16 = _pack_w_in(w_in)
    w_out16 = w_out.astype(BF16)
    pw_w16 = conv_pw_w.astype(BF16)
    norm_w3 = norm_w[:, None, :]

    out = None
    for l in range(depth):
        final = l == depth - 1
        lat = mod[l, :bsz]
        ctxm = jnp.broadcast_to(mod[l, bsz], lat.shape)
        modsel = jnp.stack([ctxm, lat], axis=1).reshape(bsz, 2, 3, d)
        xbc, pf, pb = _inproj(csrc, xsrc, lat_off, ctx_len + seq, modsel, norm_w3, rope_t, w_in16, l, nct)

        o_ssd = _ssd(xbc, pf, ssd_conv_w[l], ssd_conv_b[l][None], _head_lanes(ssd_dt_bias[l])[:, None],
                     _head_lanes(ssd_a_log[l])[:, None], jnp.repeat(ssd_d[l], SSD_HEAD_DIM)[None],
                     ssd_norm_w[l][None], ctx_len)
        tile_off = nct if final else 0
        sink_tab = jnp.broadcast_to(attn_sink[l][:, None] * LOG2E, (ATT_HEADS, LANES))
        o_att = _attention(pf, pb, sink_tab, ctx_len, tile_off)
        dw_w = jnp.broadcast_to(conv_dw_w[l][:, None, :], (CONV_KERNEL, SUBLANES, CONV_WIDTH))
        dw_b = jnp.broadcast_to(conv_dw_b[l][None], (SUBLANES, CONV_WIDTH))
        out = _mixout(csrc, xsrc, lat_off, o_ssd, o_att, pf, dw_w, dw_b, conv_ln_w[l][None], conv_ln_b[l][None], pw_w16,
                      conv_pw_b[l][None], modsel, w_out16, l, final_norm_w[None], ctx_len, final)
        csrc, xsrc, lat_off = out, out, nct
    return out
```

```python
import functools

import jax
import jax.numpy as jnp
from jax import lax
from jax.experimental import pallas as pl
from jax.experimental.pallas import tpu as pltpu

F32 = jnp.float32
BF16 = jnp.bfloat16

GRID_W = 64
EPS = 1e-6

SSD_HEADS = 16
SSD_HEAD_DIM = 64
SSD_WIDTH = SSD_HEADS * SSD_HEAD_DIM
SSD_GROUPS = 2
SSD_STATE = 128
SSD_CONV = 5
SSD_CHUNK = 128
SSD_GN = SSD_GROUPS * SSD_STATE
SSD_XBC = SSD_WIDTH + 2 * SSD_GN
GROUP_COLS = SSD_WIDTH // SSD_GROUPS

ATT_HEADS = 8
ATT_KV_HEADS = 2
ATT_HEAD_DIM = 64
ATT_WIDTH = ATT_HEADS * ATT_HEAD_DIM
ATT_KV_WIDTH = ATT_KV_HEADS * ATT_HEAD_DIM
WINDOW = 128
ROPE_BASE = 10000.0
LOG2E = 1.4426950408889634
ATT_SCALE = ATT_HEAD_DIM ** -0.5 * LOG2E

CONV_WIDTH = 512
CONV_KERNEL = 31
CONV_PAD = (CONV_KERNEL - 1) // 2

LANES = 128

PF_Z = 0
PF_GLU = PF_Z + SSD_WIDTH
PF_AG = PF_GLU + 2 * CONV_WIDTH
PF_CG = PF_AG + ATT_WIDTH
PF_DT = PF_CG + CONV_WIDTH
PF_WIDTH = PF_DT + 2 * LANES
PB_Q = 0
PB_KV = PB_Q + ATT_WIDTH
PB_WIDTH = PB_KV + 2 * ATT_KV_WIDTH

W_XBC = 0
W_Z = W_XBC + SSD_XBC
W_Q = W_Z + SSD_WIDTH
W_K = W_Q + ATT_WIDTH
W_V = W_K + ATT_KV_WIDTH
W_AG = W_V + ATT_KV_WIDTH
W_GLU = W_AG + ATT_WIDTH
W_CG = W_GLU + 2 * CONV_WIDTH
W_DT = W_CG + CONV_WIDTH
W_WIDTH = W_DT + 2 * LANES
SUBLANES = 8
VMEM_LIMIT = 56 * 1024 * 1024

ROW_TILE = 256
ATT_TILE = 256
ATT_QBLOCK = 256
CONF_TILE = 256
CONF_HALO = 16
CONF_SUB = 32
SSD_HALO = SUBLANES
SSD_STEP = 2 * SSD_CHUNK


def _dot(a, b):
    return jnp.dot(a, b, preferred_element_type=F32)


def _dot_nt(a, b):
    return lax.dot_general(a, b, (((1,), (1,)), ((), ())), preferred_element_type=F32)


def _split3(x):
    hi = x.astype(BF16)
    r1 = x - hi.astype(F32)
    mid = r1.astype(BF16)
    lo = (r1 - mid.astype(F32)).astype(BF16)
    return hi, mid, lo


def _pack3(x, width):
    hi, mid, lo = _split3(x)
    lane = lax.broadcasted_iota(jnp.int32, x.shape, 1)
    return jnp.where(lane < width, hi, jnp.where(lane < 2 * width, mid, lo))


def _shift_rows(x, j):
    rows, w = x.shape
    g = x.reshape(rows // SUBLANES, SUBLANES, w)
    r = pltpu.roll(g, SUBLANES - j, 1)
    sub = lax.broadcasted_iota(jnp.int32, (1, SUBLANES, w), 1)
    return jnp.where(sub < SUBLANES - j, r[:-1], r[1:]).reshape(rows - SUBLANES, w)


def _cparams(*sem):
    return pltpu.CompilerParams(dimension_semantics=sem, vmem_limit_bytes=VMEM_LIMIT)


def _mod_kernel(c_ref, w_ref, b_ref, o_ref):
    a = jax.nn.silu(c_ref[...]).astype(BF16)
    o_ref[...] = _dot(a, w_ref[...].astype(BF16)) + b_ref[...]


def _modulation(cond, w_mod, b_mod):
    depth, d, n = w_mod.shape
    rows = cond.shape[0]
    tn = n // 2 if (n // 2) % LANES == 0 else n
    return pl.pallas_call(
        _mod_kernel,
        grid=(depth, n // tn),
        in_specs=[
            pl.BlockSpec((rows, d), lambda l, j: (0, 0)),
            pl.BlockSpec((None, d, tn), lambda l, j: (l, 0, j)),
            pl.BlockSpec((None, 1, tn), lambda l, j: (l, 0, j)),
        ],
        out_specs=pl.BlockSpec((None, rows, tn), lambda l, j: (l, 0, j)),
        out_shape=jax.ShapeDtypeStruct((depth, rows, n), F32),
        compiler_params=_cparams("arbitrary", "arbitrary"),
        name="modulation",
    )(cond, w_mod, b_mod.reshape(depth, 1, n))


def _rope(t, cos, sin):
    width = t.shape[-1]
    lane = lax.broadcasted_iota(jnp.int32, t.shape, 1)
    first = (lane % 32) < 16
    rot = jnp.where(first, pltpu.roll(t, width - 16, 1), pltpu.roll(t, 16, 1))
    return t * cos + rot * sin


def _inproj_kernel(c_ref, x_ref, mod_ref, nw_ref, rope_ref, w_ref, xbc_ref, pf_ref, pb_ref, *, nct):
    wxbc = w_ref.at[:, W_XBC:W_XBC + SSD_XBC]
    wz = w_ref.at[:, W_Z:W_Z + SSD_WIDTH]
    wq = w_ref.at[:, W_Q:W_Q + ATT_WIDTH]
    wk = w_ref.at[:, W_K:W_K + ATT_KV_WIDTH]
    wv = w_ref.at[:, W_V:W_V + ATT_KV_WIDTH]
    wag = w_ref.at[:, W_AG:W_AG + ATT_WIDTH]
    wglu = w_ref.at[:, W_GLU:W_GLU + 2 * CONV_WIDTH]
    wcg = w_ref.at[:, W_CG:W_CG + CONV_WIDTH]
    wdt = w_ref.at[:, W_DT:W_DT + 2 * LANES]
    x = jnp.where(pl.program_id(1) < nct, c_ref[...], x_ref[...])
    y = x * lax.rsqrt(jnp.mean(x * x, axis=-1, keepdims=True) + EPS) * nw_ref[...]
    h = (y * (1.0 + mod_ref[1:2, :]) + mod_ref[0:1, :]).astype(BF16)
    cos = rope_ref[:, :LANES]
    sin = rope_ref[:, LANES:]
    reps = ATT_WIDTH // LANES
    cos_q = jnp.concatenate([cos] * reps, axis=1)
    sin_q = jnp.concatenate([sin] * reps, axis=1)
    pb_ref[:, PB_Q:PB_Q + ATT_WIDTH] = (_rope(_dot(h, wq[...]), cos_q, sin_q) * ATT_SCALE).astype(BF16)
    pb_ref[:, PB_KV:PB_KV + ATT_KV_WIDTH] = _rope(_dot(h, wk[...]), cos, sin).astype(BF16)
    pb_ref[:, PB_KV + ATT_KV_WIDTH:PB_KV + 2 * ATT_KV_WIDTH] = _dot(h, wv[...]).astype(BF16)
    pf_ref[:, PF_Z:PF_Z + SSD_WIDTH] = jax.nn.silu(_dot(h, wz[...]))
    pf_ref[:, PF_AG:PF_AG + ATT_WIDTH] = jax.nn.silu(_dot(h, wag[...]))
    pf_ref[:, PF_CG:PF_CG + CONV_WIDTH] = jax.nn.silu(_dot(h, wcg[...]))
    glu = _dot(h, wglu[...])
    pf_ref[:, PF_GLU:PF_GLU + CONV_WIDTH] = glu[:, :CONV_WIDTH]
    pf_ref[:, PF_GLU + CONV_WIDTH:PF_GLU + 2 * CONV_WIDTH] = jax.nn.sigmoid(glu[:, CONV_WIDTH:])
    pf_ref[:, PF_DT:PF_DT + 2 * LANES] = _dot(h, wdt[...])
    xbc_ref[...] = _dot(h, wxbc[...])


def _token_specs(tm, d, nct, lat_off, tile_off):
    return [
        pl.BlockSpec((None, tm, d), lambda b, i: (b, jnp.minimum(i + tile_off, nct - 1), 0)),
        pl.BlockSpec((None, tm, d), lambda b, i: (b, jnp.maximum(i + tile_off - nct, 0) + lat_off, 0)),
    ]


def _inproj(csrc, xsrc, lat_off, t, modsel, norm_w, rope_t, w_packed, layer, nct):
    bsz, _, d = xsrc.shape
    tm = ROW_TILE
    row_map = lambda b, i: (b, i, 0)
    in_specs = _token_specs(tm, d, nct, lat_off, 0) + [
        pl.BlockSpec((None, None, 3, d), lambda b, i: (b, jnp.where(i >= nct, 1, 0), 0, 0)),
        pl.BlockSpec((None, 1, d), lambda b, i: (layer, 0, 0)),
        pl.BlockSpec((tm, 2 * LANES), lambda b, i: (i, 0)),
        pl.BlockSpec((None, d, W_WIDTH), lambda b, i: (layer, 0, 0)),
    ]
    widths = (SSD_XBC, PF_WIDTH, PB_WIDTH)
    return pl.pallas_call(
        functools.partial(_inproj_kernel, nct=nct),
        grid=(bsz, t // tm),
        in_specs=in_specs,
        out_specs=[pl.BlockSpec((None, tm, w), row_map) for w in widths],
        out_shape=[jax.ShapeDtypeStruct((bsz, t, w), dt) for w, dt in zip(widths, (F32, F32, BF16))],
        compiler_params=_cparams("parallel", "arbitrary"),
        name="inproj",
    )(csrc, xsrc, modsel, norm_w, rope_t, w_packed)


def _ssd_prep(xs, bc, dt_raw, dtb, alog, fwd):
    q = SSD_CHUNK
    bb = bc[:, :SSD_GN]
    cb16 = bc[:, SSD_GN:]

    groups = [slice(g * SSD_STATE, (g + 1) * SSD_STATE) for g in range(SSD_GROUPS)]
    scores_g = [_dot_nt(cb16[:, gs], bb[:, gs]) for gs in groups]

    lane = lax.broadcasted_iota(jnp.int32, (q, LANES), 1)
    head_lane = lane < 3 * SSD_HEADS
    dt = jnp.where(head_lane, jax.nn.softplus(dt_raw + dtb), 0.0)
    da = dt * (-jnp.exp(alog))
    li = lax.broadcasted_iota(jnp.int32, (q, q), 0)
    si = lax.broadcasted_iota(jnp.int32, (q, q), 1)
    causal = (si <= li) if fwd else (si >= li)
    ones = jnp.where(causal, 1.0, 0.0).astype(BF16)
    cs3 = _dot(ones, jnp.concatenate(_split3(da), axis=1))
    cs = cs3[:, :LANES] + cs3[:, LANES:2 * LANES] + cs3[:, 2 * LANES:]
    total = jnp.sum(da, axis=0, keepdims=True)
    to_end = jnp.exp(total - cs)
    from_start = jnp.where(head_lane, jnp.exp(cs), 0.0)
    chunk_decay = jnp.where(head_lane[:SUBLANES], jnp.exp(total), 0.0)
    cs_t = cs.T

    er = lax.broadcasted_iota(jnp.int32, (LANES, SSD_WIDTH), 0)
    ec = lax.broadcasted_iota(jnp.int32, (LANES, SSD_WIDTH), 1)
    spread = jnp.where((ec // SSD_HEAD_DIM == er % SSD_HEADS) & (er < 3 * SSD_HEADS), 1.0, 0.0).astype(BF16)
    dt_e = _dot(_pack3(dt, SSD_HEADS), spread)
    stacked = jnp.concatenate([dt * to_end, from_start, chunk_decay], axis=0)
    stacked_e = _dot(_pack3(stacked, SSD_HEADS), spread)
    w_e = stacked_e[:q]
    fs_e = stacked_e[q:2 * q]
    cd_e = stacked_e[2 * q:2 * q + 1]

    xd = (xs * dt_e).astype(BF16)
    xdw = (xs * w_e).astype(BF16)
    low = lax.broadcasted_iota(jnp.int32, (q, LANES), 1) < SSD_HEAD_DIM
    zero16 = jnp.zeros((q, LANES), BF16)
    y_diag, new_states = [], []
    for g in range(SSD_GROUPS):
        gs = groups[g]
        gc = slice(g * GROUP_COLS, (g + 1) * GROUP_COLS)
        scores = scores_g[g]
        new_states.append(_dot(bb[:, gs].astype(F32).T.astype(BF16), xdw[:, gc]))
        for pr in range(GROUP_COLS // LANES):
            h0 = (g * GROUP_COLS + pr * LANES) // SSD_HEAD_DIM
            cols = slice(h0 * SSD_HEAD_DIM, h0 * SSD_HEAD_DIM + LANES)
            xpair = xd[:, cols]
            m0 = (scores * jnp.exp(jnp.where(causal, cs[:, h0:h0 + 1] - cs_t[h0:h0 + 1, :], -jnp.inf))).astype(BF16)
            m1 = (scores * jnp.exp(jnp.where(causal, cs[:, h0 + 1:h0 + 2] - cs_t[h0 + 1:h0 + 2, :], -jnp.inf))).astype(BF16)
            block_diag = jnp.concatenate([jnp.where(low, xpair, zero16), jnp.where(low, zero16, xpair)], axis=0)
            y_diag.append(_dot(jnp.concatenate([m0, m1], axis=1), block_diag))
    return y_diag, new_states, cb16, fs_e, cd_e


def _ssd_finish(prep, st_ref):
    y_diag, new_states, cb16, fs_e, cd_e = prep
    ys = []
    for g in range(SSD_GROUPS):
        gs = slice(g * SSD_STATE, (g + 1) * SSD_STATE)
        gc = slice(g * GROUP_COLS, (g + 1) * GROUP_COLS)
        entering = st_ref[g]
        y_off = _dot(cb16[:, gs], entering.astype(BF16)) * fs_e[:, gc]
        st_ref[g] = entering * cd_e[:, gc] + new_states[g]
        per_group = GROUP_COLS // LANES
        for pr in range(per_group):
            ys.append(y_diag[g * per_group + pr] + y_off[:, pr * LANES:(pr + 1) * LANES])
    return jnp.concatenate(ys, axis=1)


def _ssd_bwd_block(k, ncb, nb):
    return jnp.where(k < ncb, ncb - 1 - k, nb + ncb - 1 - k)


def _ssd_kernel(xm_ref, xp_ref, xn_ref, dt_ref, z_ref, cw_ref, cb_ref, dtb_ref, alog_ref, dvec_ref, nw_ref,
                o_ref, yb_ref, st_ref, xs_ref, bc_ref, *, ncb, nb):
    q = SSD_CHUNK
    rows = SSD_STEP
    n_sub = rows // q
    s = pl.program_id(1)

    def conv_block(cb):
        first = (cb == 0) | (cb == ncb)
        last = (cb == ncb - 1) | (cb == nb - 1)
        xext = jnp.concatenate([jnp.where(first, 0.0, xp_ref[...]), xm_ref[...],
                                jnp.where(last, 0.0, xn_ref[...])], axis=0)
        acc = cb_ref[...]
        for k in range(SSD_CONV):
            d = k - (SSD_CONV - 1) // 2
            if d == 0:
                tap = xext[SSD_HALO:SSD_HALO + rows]
            elif d > 0:
                tap = _shift_rows(xext, d)[SSD_HALO:SSD_HALO + rows]
            else:
                tap = _shift_rows(xext, SSD_HALO + d)[:rows]
            acc = acc + cw_ref[k:k + 1, :] * tap
        u = jax.nn.silu(acc)
        dst = pl.ds(pl.multiple_of(cb * rows, rows), rows)
        xs_ref[dst, :] = u[:, :SSD_WIDTH]
        bc_ref[dst, :] = u[:, SSD_WIDTH:].astype(BF16)

    def chunk_rows(blk, j):
        return pl.ds(pl.multiple_of(blk * rows + j * q, q), q)

    @pl.when((s == 0) | (s == nb))
    def _():
        st_ref[...] = jnp.zeros_like(st_ref)

    @pl.when(s <= nb - 1)
    def _():
        conv_block(_ssd_bwd_block(s, ncb, nb))

    @pl.when(s < nb)
    def _():
        blk = _ssd_bwd_block(s, ncb, nb)
        order = tuple(reversed(range(n_sub)))
        preps = [_ssd_prep(xs_ref[chunk_rows(blk, j), :], bc_ref[chunk_rows(blk, j), :], dt_ref[j * q:(j + 1) * q, :],
                           dtb_ref[...], alog_ref[...], False) for j in order]
        for j, prep in zip(order, preps):
            yb_ref[chunk_rows(blk, j), :] = _ssd_finish(prep, st_ref)

    @pl.when(s >= nb)
    def _():
        blk = s - nb
        preps = [_ssd_prep(xs_ref[chunk_rows(blk, j), :], bc_ref[chunk_rows(blk, j), :], dt_ref[j * q:(j + 1) * q, :],
                           dtb_ref[...], alog_ref[...], True) for j in range(n_sub)]
        for j, prep in enumerate(preps):
            y = _ssd_finish(prep, st_ref)
            tot = y + yb_ref[chunk_rows(blk, j), :] + dvec_ref[...] * xs_ref[chunk_rows(blk, j), :]
            gated = tot * z_ref[j * q:(j + 1) * q, :]
            outs = []
            for g in range(SSD_GROUPS):
                gc = slice(g * GROUP_COLS, (g + 1) * GROUP_COLS)
                v = gated[:, gc]
                outs.append(v * lax.rsqrt(jnp.mean(v * v, axis=-1, keepdims=True) + EPS) * nw_ref[:, gc])
            o_ref[j * q:(j + 1) * q, :] = jnp.concatenate(outs, axis=1).astype(BF16)


def _ssd(xbc, pf, conv_w, conv_b, dt_bias, a_log, dvec, norm_w, ctx_len):
    bsz, t, _ = pf.shape
    rows = SSD_STEP
    nb = t // rows
    ncb = ctx_len // rows
    hb = rows // SSD_HALO
    nhalo = t // SSD_HALO
    const2 = lambda b, s: (0, 0)
    xblk = lambda s: _ssd_bwd_block(jnp.minimum(s, nb - 1), ncb, nb)
    cur = lambda s: jnp.where(s < nb, _ssd_bwd_block(s, ncb, nb), s - nb)
    phase = lambda s: jnp.where(s < nb, 1, 0)
    fwd_blk = lambda s: jnp.maximum(s - nb, 0)
    in_specs = [
        pl.BlockSpec((None, rows, SSD_XBC), lambda b, s: (b, xblk(s), 0)),
        pl.BlockSpec((None, SSD_HALO, SSD_XBC), lambda b, s: (b, jnp.maximum(xblk(s) * hb - 1, 0), 0)),
        pl.BlockSpec((None, SSD_HALO, SSD_XBC), lambda b, s: (b, jnp.minimum(xblk(s) * hb + hb, nhalo - 1), 0)),
        pl.BlockSpec((None, rows, LANES), lambda b, s: (b, cur(s), PF_DT // LANES + phase(s))),
        pl.BlockSpec((None, rows, SSD_WIDTH), lambda b, s: (b, fwd_blk(s), PF_Z // SSD_WIDTH)),
        pl.BlockSpec((SSD_CONV, SSD_XBC), const2),
        pl.BlockSpec((1, SSD_XBC), const2),
        pl.BlockSpec((None, 1, LANES), lambda b, s: (phase(s), 0, 0)),
        pl.BlockSpec((None, 1, LANES), lambda b, s: (phase(s), 0, 0)),
        pl.BlockSpec((1, SSD_WIDTH), const2),
        pl.BlockSpec((1, SSD_WIDTH), const2),
    ]
    return pl.pallas_call(
        functools.partial(_ssd_kernel, ncb=ncb, nb=nb),
        grid=(bsz, 2 * nb),
        in_specs=in_specs,
        out_specs=pl.BlockSpec((None, rows, SSD_WIDTH), lambda b, s: (b, fwd_blk(s), 0)),
        out_shape=jax.ShapeDtypeStruct((bsz, t, SSD_WIDTH), BF16),
        scratch_shapes=[
            pltpu.VMEM((t, SSD_WIDTH), F32),
            pltpu.VMEM((SSD_GROUPS, SSD_STATE, GROUP_COLS), F32),
            pltpu.VMEM((t, SSD_WIDTH), F32),
            pltpu.VMEM((t, 2 * SSD_GN), BF16),
        ],
        compiler_params=_cparams("parallel", "arbitrary"),
        name="ssd",
    )(xbc, xbc, xbc, pf, pf, conv_w, conv_b, dt_bias, a_log, dvec, norm_w)


def _swap_halves(blk):
    half = blk.shape[-1] // 2
    return jnp.concatenate([blk[:, half:], blk[:, :half]], axis=1)


def _head_pads(blk, g, low):
    zero = jnp.zeros_like(blk)
    sw = _swap_halves(blk)
    if g == 0:
        return jnp.where(low, blk, zero), jnp.where(low, zero, sw)
    return jnp.where(low, sw, zero), jnp.where(low, zero, blk)


def _attn_kernel(q_ref, kvm_ref, kvp_ref, kvn_ref, kvc_ref, ag_ref, sink_ref, o_ref, pad_ref,
                 *, tile_off, nct, seq, ctx_len):
    qb = ATT_TILE
    blk = ATT_QBLOCK
    wk = blk + 2 * WINDOW
    tile = pl.program_id(1) + tile_off
    lat0 = (tile - nct) * qb
    low_k = lax.broadcasted_iota(jnp.int32, (1, LANES), 1) < ATT_HEAD_DIM

    kv = jnp.concatenate([kvc_ref[...], kvp_ref[...], kvm_ref[...], kvn_ref[...]], axis=0)
    for g in range(ATT_KV_HEADS):
        ka, kb = _head_pads(kv[:, :ATT_KV_WIDTH], g, low_k)
        va, vb = _head_pads(kv[:, ATT_KV_WIDTH:], g, low_k)
        pad_ref[g, 0] = ka
        pad_ref[g, 1] = kb
        pad_ref[g, 2] = va
        pad_ref[g, 3] = vb

    r = lax.broadcasted_iota(jnp.int32, (blk, wk), 0)
    col = lax.broadcasted_iota(jnp.int32, (blk, wk), 1)
    in_band = jnp.abs(col - WINDOW - r) <= WINDOW
    neg = jnp.finfo(F32).min
    rep = ATT_HEADS // ATT_KV_HEADS
    pairs_per_group = rep * ATT_HEAD_DIM // LANES
    for a in range(qb // blk):
        rows = slice(a * blk, (a + 1) * blk)
        kpos = lat0 + a * blk - WINDOW + col
        valid = in_band & (kpos >= 0) & (kpos < seq) & (lat0 >= 0)
        win0 = ctx_len + a * blk
        def operand(g, slot):
            if a == 0:
                return pad_ref[g, slot, 0:win0 + wk, :]
            return jnp.concatenate([pad_ref[g, slot, 0:ctx_len, :], pad_ref[g, slot, win0:win0 + wk, :]], axis=0)

        for g in range(ATT_KV_HEADS):
            for pr in range(pairs_per_group):
                p = g * pairs_per_group + pr
                qp = q_ref[rows, p * LANES:(p + 1) * LANES]
                acc = jnp.zeros((blk, LANES), F32)
                invs = []
                for j in range(2):
                    head = 2 * p + j
                    sink = sink_ref[head:head + 1, 0:1]
                    s = _dot_nt(qp, operand(g, j))
                    s = jnp.concatenate([s[:, :ctx_len], jnp.where(valid, s[:, ctx_len:], neg)], axis=1)
                    m = jnp.maximum(jnp.max(s, axis=-1, keepdims=True), sink)
                    pexp = jnp.exp2(s - m)
                    den = jnp.sum(pexp, axis=-1, keepdims=True) + jnp.exp2(sink - m)
                    acc = acc + _dot(pexp.astype(BF16), operand(g, 2 + j))
                    invs.append(1.0 / den)
                gate = ag_ref[rows, p * LANES:(p + 1) * LANES]
                o_ref[rows, p * LANES:(p + 1) * LANES] = (acc * jnp.where(low_k, invs[0], invs[1]) * gate).astype(BF16)


def _attention(pf, pb, sink_tab, ctx_len, tile_off):
    bsz, t, _ = pf.shape
    qb = ATT_TILE
    nct = ctx_len // qb
    hb = qb // WINDOW
    ctx_hb = ctx_len // WINDOW
    n_hb = t // WINDOW
    ntiles = t // qb - tile_off
    kvw = 2 * ATT_KV_WIDTH
    kv_col = PB_KV // kvw
    in_specs = [
        pl.BlockSpec((None, qb, ATT_WIDTH), lambda b, i: (b, i + tile_off, PB_Q // ATT_WIDTH)),
        pl.BlockSpec((None, qb, kvw), lambda b, i: (b, i + tile_off, kv_col)),
        pl.BlockSpec((None, WINDOW, kvw), lambda b, i: (b, jnp.maximum((i + tile_off) * hb - 1, ctx_hb), kv_col)),
        pl.BlockSpec((None, WINDOW, kvw), lambda b, i: (b, jnp.minimum((i + tile_off) * hb + hb, n_hb - 1), kv_col)),
        pl.BlockSpec((None, ctx_len, kvw), lambda b, i: (b, 0, kv_col)),
        pl.BlockSpec((None, qb, ATT_WIDTH), lambda b, i: (b, i + tile_off, PF_AG // ATT_WIDTH)),
        pl.BlockSpec((ATT_HEADS, LANES), lambda b, i: (0, 0)),
    ]
    return pl.pallas_call(
        functools.partial(_attn_kernel, tile_off=tile_off, nct=nct, seq=t - ctx_len, ctx_len=ctx_len),
        grid=(bsz, ntiles),
        in_specs=in_specs,
        out_specs=pl.BlockSpec((None, qb, ATT_WIDTH), lambda b, i: (b, i, 0)),
        out_shape=jax.ShapeDtypeStruct((bsz, ntiles * qb, ATT_WIDTH), BF16),
        scratch_shapes=[pltpu.VMEM((ATT_KV_HEADS, 4, ctx_len + qb + 2 * WINDOW, LANES), BF16)],
        compiler_params=_cparams("parallel", "arbitrary"),
        name="attention",
    )(pb, pb, pb, pb, pb, pf, sink_tab)


def _conf_tile(gm_ref, gp_ref, gn_ref, cg_ref, dww_ref, dwb_ref, lnw_ref, lnb_ref, pww_ref, pwb_ref, hbuf, abuf,
               first, last):
    tc = CONF_TILE

    def glu(v):
        return v[:, :CONV_WIDTH] * v[:, CONV_WIDTH:]

    rows = tc + 2 * CONF_HALO
    ext = jnp.concatenate([jnp.where(first, 0.0, glu(gp_ref[...])), glu(gm_ref[...]),
                           jnp.where(last, 0.0, glu(gn_ref[...]))], axis=0)
    hbuf[0] = ext
    for j in range(1, SUBLANES):
        hbuf[j, 0:rows - SUBLANES, :] = _shift_rows(ext, j)
    base = CONF_HALO - CONV_PAD
    for r in range(tc // CONF_SUB):
        acc = jnp.broadcast_to(dwb_ref[...], (CONF_SUB // SUBLANES, SUBLANES, CONV_WIDTH))
        for k in range(CONV_KERNEL):
            blk, res = divmod(base + k, SUBLANES)
            strip = hbuf[res, pl.ds(r * CONF_SUB + blk * SUBLANES, CONF_SUB), :]
            acc = acc + dww_ref[k] * strip.reshape(CONF_SUB // SUBLANES, SUBLANES, CONV_WIDTH)
        acc = acc.reshape(CONF_SUB, CONV_WIDTH)
        xc = acc - jnp.mean(acc, axis=-1, keepdims=True)
        var = jnp.mean(xc * xc, axis=-1, keepdims=True)
        ln = xc * lax.rsqrt(var + EPS) * lnw_ref[...] + lnb_ref[...]
        abuf[pl.ds(r * CONF_SUB, CONF_SUB), :] = jax.nn.silu(ln)
    out = _dot(abuf[...].astype(BF16), pww_ref[...]) + pwb_ref[...]
    return (out * cg_ref[...]).astype(BF16)


def _mixout_kernel(c_ref, x_ref, os_ref, oa_ref, gm_ref, gp_ref, gn_ref, cg_ref, dww_ref, dwb_ref, lnw_ref, lnb_ref,
                   pww_ref, pwb_ref, mod_ref, w_ref, fnw_ref, o_ref, hbuf, abuf, *, final, nct, nt, tile_off):
    tile = pl.program_id(1) + tile_off
    first = (tile == 0) | (tile == nct)
    last = (tile == nct - 1) | (tile == nt - 1)
    a0 = SSD_WIDTH
    c0 = SSD_WIDTH + ATT_WIDTH
    u = _dot(os_ref[...], w_ref[0:a0, :]) + _dot(oa_ref[...], w_ref[a0:c0, :])
    oc = _conf_tile(gm_ref, gp_ref, gn_ref, cg_ref, dww_ref, dwb_ref, lnw_ref, lnb_ref, pww_ref, pwb_ref,
                    hbuf, abuf, first, last)
    u = u + _dot(oc, w_ref[c0:c0 + CONV_WIDTH, :])
    x = x_ref[...]
    if tile_off < nct:
        x = jnp.where(tile < nct, c_ref[...], x)
    xn = x + mod_ref[2:3, :] * u
    if final:
        xn = xn * lax.rsqrt(jnp.mean(xn * xn, axis=-1, keepdims=True) + EPS) * fnw_ref[...]
    o_ref[...] = xn


def _mixout(csrc, xsrc, lat_off, o_ssd, o_att, pf, dw_w, dw_b, ln_w, ln_b, pw_w16, pw_b, modsel, w_out16, layer,
            final_norm_w, ctx_len, final):
    bsz, t, _ = o_ssd.shape
    d = xsrc.shape[-1]
    tm = ROW_TILE
    nt = t // tm
    nct = ctx_len // tm
    tile_off = nct if final else 0
    hb = tm // CONF_HALO
    nhalo = t // CONF_HALO
    main = lambda b, i: (b, i + tile_off, 0)
    const2 = lambda b, i: (0, 0)
    glu_col = PF_GLU // (2 * CONV_WIDTH)
    in_specs = _token_specs(tm, d, nct, lat_off, tile_off) + [
        pl.BlockSpec((None, tm, SSD_WIDTH), main),
        pl.BlockSpec((None, tm, ATT_WIDTH), lambda b, i: (b, i, 0)),
        pl.BlockSpec((None, tm, 2 * CONV_WIDTH), lambda b, i: (b, i + tile_off, glu_col)),
        pl.BlockSpec((None, CONF_HALO, 2 * CONV_WIDTH),
                     lambda b, i: (b, jnp.maximum((i + tile_off) * hb - 1, 0), glu_col)),
        pl.BlockSpec((None, CONF_HALO, 2 * CONV_WIDTH),
                     lambda b, i: (b, jnp.minimum((i + tile_off) * hb + hb, nhalo - 1), glu_col)),
        pl.BlockSpec((None, tm, CONV_WIDTH), lambda b, i: (b, i + tile_off, PF_CG // CONV_WIDTH)),
        pl.BlockSpec((CONV_KERNEL, SUBLANES, CONV_WIDTH), lambda b, i: (0, 0, 0)),
        pl.BlockSpec((SUBLANES, CONV_WIDTH), const2),
        pl.BlockSpec((1, CONV_WIDTH), const2),
        pl.BlockSpec((1, CONV_WIDTH), const2),
        pl.BlockSpec((None, CONV_WIDTH, CONV_WIDTH), lambda b, i: (layer, 0, 0)),
        pl.BlockSpec((1, CONV_WIDTH), const2),
        pl.BlockSpec((None, None, 3, d), lambda b, i: (b, jnp.where(i + tile_off >= nct, 1, 0), 0, 0)),
        pl.BlockSpec((None,) + w_out16.shape[1:], lambda b, i: (layer, 0, 0)),
        pl.BlockSpec((1, d), const2),
    ]
    rows_out = t - tile_off * tm
    return pl.pallas_call(
        functools.partial(_mixout_kernel, final=final, nct=nct, nt=nt, tile_off=tile_off),
        grid=(bsz, rows_out // tm),
        in_specs=in_specs,
        out_specs=pl.BlockSpec((None, tm, d), lambda b, i: (b, i, 0)),
        out_shape=jax.ShapeDtypeStruct((bsz, rows_out, d), F32),
        scratch_shapes=[
            pltpu.VMEM((SUBLANES, tm + 2 * CONF_HALO, CONV_WIDTH), F32),
            pltpu.VMEM((tm, CONV_WIDTH), F32),
        ],
        compiler_params=_cparams("parallel", "arbitrary"),
        name="mixout_final" if final else "mixout",
    )(csrc, xsrc, o_ssd, o_att, pf, pf, pf, pf, dw_w, dw_b, ln_w, ln_b, pw_w16, pw_b, modsel, w_out16, final_norm_w)


def _rope_tables(seq, ctx_len):
    rows = seq // GRID_W
    row = jnp.repeat(jnp.arange(rows, dtype=F32), GRID_W)
    col = jnp.tile(jnp.arange(GRID_W, dtype=F32), rows)
    axis_dim = ATT_HEAD_DIM // 2
    inv_freq = ROPE_BASE ** (-jnp.arange(0, axis_dim, 2, dtype=F32) / axis_dim)
    ar = row[:, None] * inv_freq
    ac = col[:, None] * inv_freq
    cos = jnp.concatenate([jnp.cos(ar), jnp.cos(ar), jnp.cos(ac), jnp.cos(ac)], axis=1)
    sin = jnp.concatenate([-jnp.sin(ar), jnp.sin(ar), -jnp.sin(ac), jnp.sin(ac)], axis=1)
    reps = LANES // ATT_HEAD_DIM
    cos = jnp.concatenate([jnp.ones((ctx_len, LANES), F32), jnp.tile(cos, (1, reps))], axis=0)
    sin = jnp.concatenate([jnp.zeros((ctx_len, LANES), F32), jnp.tile(sin, (1, reps))], axis=0)
    return jnp.concatenate([cos, sin], axis=1)


def _head_lanes(a):
    rep = jnp.concatenate([a, a, a], axis=-1)
    return jnp.pad(rep, [(0, 0)] * (a.ndim - 1) + [(0, LANES - 3 * SSD_HEADS)])


def _pack_w_in(w_in):
    dt0 = SSD_XBC + SSD_WIDTH
    dt_w = w_in[..., dt0:dt0 + 2 * SSD_HEADS]
    packed = jnp.concatenate([w_in[..., :dt0], w_in[..., dt0 + 2 * SSD_HEADS:],
                              _head_lanes(dt_w[..., :SSD_HEADS]), _head_lanes(dt_w[..., SSD_HEADS:])], axis=-1)
    assert packed.shape[-1] == W_WIDTH
    return packed.astype(BF16)


def kernel(x, c, ctx, c_ctx, w_mod, b_mod, norm_w, w_in, ssd_conv_w, ssd_conv_b, ssd_dt_bias, ssd_a_log, ssd_d,
           ssd_norm_w, attn_sink, conv_dw_w, conv_dw_b, conv_ln_w, conv_ln_b, conv_pw_w, conv_pw_b, w_out,
           final_norm_w):
    bsz, seq, d = x.shape
    ctx_len = ctx.shape[1]
    depth = w_in.shape[0]
    assert ctx_len % ROW_TILE == 0 and seq % ROW_TILE == 0 and seq % GRID_W == 0
    assert ROW_TILE == ATT_TILE == CONF_TILE
    nct = ctx_len // ROW_TILE

    cond_rows = -(-(bsz + 1) // SUBLANES) * SUBLANES
    cond = jnp.zeros((cond_rows, d), F32).at[:bsz].set(c).at[bsz].set(c_ctx)
    mod = _modulation(cond, w_mod, b_mod)
    rope_t = _rope_tables(seq, ctx_len)
    csrc, xsrc, lat_off = ctx, x, 0
    w_in16 = _pack_w_in(w_in)
    w_out16 = w_out.astype(BF16)
    pw_w16 = conv_pw_w.astype(BF16)
    norm_w3 = norm_w[:, None, :]

    out = None
    for l in range(depth):
        final = l == depth - 1
        lat = mod[l, :bsz]
        ctxm = jnp.broadcast_to(mod[l, bsz], lat.shape)
        modsel = jnp.stack([ctxm, lat], axis=1).reshape(bsz, 2, 3, d)
        xbc, pf, pb = _inproj(csrc, xsrc, lat_off, ctx_len + seq, modsel, norm_w3, rope_t, w_in16, l, nct)

        o_ssd = _ssd(xbc, pf, ssd_conv_w[l], ssd_conv_b[l][None], _head_lanes(ssd_dt_bias[l])[:, None],
                     _head_lanes(ssd_a_log[l])[:, None], jnp.repeat(ssd_d[l], SSD_HEAD_DIM)[None],
                     ssd_norm_w[l][None], ctx_len)
        tile_off = nct if final else 0
        sink_tab = jnp.broadcast_to(attn_sink[l][:, None] * LOG2E, (ATT_HEADS, LANES))
        o_att = _attention(pf, pb, sink_tab, ctx_len, tile_off)
        dw_w = jnp.broadcast_to(conv_dw_w[l][:, None, :], (CONV_KERNEL, SUBLANES, CONV_WIDTH))
        dw_b = jnp.broadcast_to(conv_dw_b[l][None], (SUBLANES, CONV_WIDTH))
        out = _mixout(csrc, xsrc, lat_off, o_ssd, o_att, pf, dw_w, dw_b, conv_ln_w[l][None], conv_ln_b[l][None], pw_w16,
                      conv_pw_b[l][None], modsel, w_out16, l, final_norm_w[None], ctx_len, final)
        csrc, xsrc, lat_off = out, out, nct
    return out
```

```python
import functools

import jax
import jax.numpy as jnp
from jax import lax
from jax.experimental import pallas as pl
from jax.experimental.pallas import tpu as pltpu

F32 = jnp.float32
BF16 = jnp.bfloat16

GRID_W = 64
EPS = 1e-6

SSD_HEADS = 16
SSD_HEAD_DIM = 64
SSD_WIDTH = SSD_HEADS * SSD_HEAD_DIM
SSD_GROUPS = 2
SSD_STATE = 128
SSD_CONV = 5
SSD_CHUNK = 128
SSD_GN = SSD_GROUPS * SSD_STATE
SSD_XBC = SSD_WIDTH + 2 * SSD_GN
GROUP_COLS = SSD_WIDTH // SSD_GROUPS

ATT_HEADS = 8
ATT_KV_HEADS = 2
ATT_HEAD_DIM = 64
ATT_WIDTH = ATT_HEADS * ATT_HEAD_DIM
ATT_KV_WIDTH = ATT_KV_HEADS * ATT_HEAD_DIM
WINDOW = 128
ROPE_BASE = 10000.0
LOG2E = 1.4426950408889634
ATT_SCALE = ATT_HEAD_DIM ** -0.5 * LOG2E

CONV_WIDTH = 512
CONV_KERNEL = 31
CONV_PAD = (CONV_KERNEL - 1) // 2

LANES = 128

PF_Z = 0
PF_GLU = PF_Z + SSD_WIDTH
PF_AG = PF_GLU + 2 * CONV_WIDTH
PF_CG = PF_AG + ATT_WIDTH
PF_DT = PF_CG + CONV_WIDTH
PF_WIDTH = PF_DT + 2 * LANES
PB_Q = 0
PB_KV = PB_Q + ATT_WIDTH
PB_WIDTH = PB_KV + 2 * ATT_KV_WIDTH

W_XBC = 0
W_Z = W_XBC + SSD_XBC
W_Q = W_Z + SSD_WIDTH
W_K = W_Q + ATT_WIDTH
W_V = W_K + ATT_KV_WIDTH
W_AG = W_V + ATT_KV_WIDTH
W_GLU = W_AG + ATT_WIDTH
W_CG = W_GLU + 2 * CONV_WIDTH
W_DT = W_CG + CONV_WIDTH
W_WIDTH = W_DT + 2 * LANES
SUBLANES = 8
VMEM_LIMIT = 56 * 1024 * 1024

ROW_TILE = 256
ATT_TILE = 256
ATT_QBLOCK = 256
ATT_HALF = 128
CONF_TILE = 256
CONF_HALO = 16
CONF_SUB = 32
SSD_HALO = SUBLANES
SSD_STEP = 2 * SSD_CHUNK


def _dot(a, b):
    return jnp.dot(a, b, preferred_element_type=F32)


def _dot_nt(a, b):
    return lax.dot_general(a, b, (((1,), (1,)), ((), ())), preferred_element_type=F32)


def _split3(x):
    hi = x.astype(BF16)
    r1 = x - hi.astype(F32)
    mid = r1.astype(BF16)
    lo = (r1 - mid.astype(F32)).astype(BF16)
    return hi, mid, lo


def _pack3(x, width):
    hi, mid, lo = _split3(x)
    lane = lax.broadcasted_iota(jnp.int32, x.shape, 1)
    return jnp.where(lane < width, hi, jnp.where(lane < 2 * width, mid, lo))


def _shift_rows(x, j):
    rows, w = x.shape
    g = x.reshape(rows // SUBLANES, SUBLANES, w)
    r = pltpu.roll(g, SUBLANES - j, 1)
    sub = lax.broadcasted_iota(jnp.int32, (1, SUBLANES, w), 1)
    return jnp.where(sub < SUBLANES - j, r[:-1], r[1:]).reshape(rows - SUBLANES, w)


def _cparams(*sem):
    return pltpu.CompilerParams(dimension_semantics=sem, vmem_limit_bytes=VMEM_LIMIT)


def _mod_kernel(c_ref, w_ref, b_ref, o_ref):
    a = jax.nn.silu(c_ref[...]).astype(BF16)
    o_ref[...] = _dot(a, w_ref[...].astype(BF16)) + b_ref[...]


def _modulation(cond, w_mod, b_mod):
    depth, d, n = w_mod.shape
    rows = cond.shape[0]
    tn = n // 2 if (n // 2) % LANES == 0 else n
    return pl.pallas_call(
        _mod_kernel,
        grid=(depth, n // tn),
        in_specs=[
            pl.BlockSpec((rows, d), lambda l, j: (0, 0)),
            pl.BlockSpec((None, d, tn), lambda l, j: (l, 0, j)),
            pl.BlockSpec((None, 1, tn), lambda l, j: (l, 0, j)),
        ],
        out_specs=pl.BlockSpec((None, rows, tn), lambda l, j: (l, 0, j)),
        out_shape=jax.ShapeDtypeStruct((depth, rows, n), F32),
        compiler_params=_cparams("arbitrary", "arbitrary"),
        name="modulation",
    )(cond, w_mod, b_mod.reshape(depth, 1, n))


def _rope(t, cos, sin):
    width = t.shape[-1]
    lane = lax.broadcasted_iota(jnp.int32, t.shape, 1)
    first = (lane % 32) < 16
    rot = jnp.where(first, pltpu.roll(t, width - 16, 1), pltpu.roll(t, 16, 1))
    return t * cos + rot * sin


def _inproj_kernel(c_ref, x_ref, mod_ref, nw_ref, rope_ref, w_ref, xbc_ref, pf_ref, pb_ref, *, nct):
    wxbc = w_ref.at[:, W_XBC:W_XBC + SSD_XBC]
    wz = w_ref.at[:, W_Z:W_Z + SSD_WIDTH]
    wq = w_ref.at[:, W_Q:W_Q + ATT_WIDTH]
    wk = w_ref.at[:, W_K:W_K + ATT_KV_WIDTH]
    wv = w_ref.at[:, W_V:W_V + ATT_KV_WIDTH]
    wag = w_ref.at[:, W_AG:W_AG + ATT_WIDTH]
    wglu = w_ref.at[:, W_GLU:W_GLU + 2 * CONV_WIDTH]
    wcg = w_ref.at[:, W_CG:W_CG + CONV_WIDTH]
    wdt = w_ref.at[:, W_DT:W_DT + 2 * LANES]
    x = jnp.where(pl.program_id(1) < nct, c_ref[...], x_ref[...])
    y = x * lax.rsqrt(jnp.mean(x * x, axis=-1, keepdims=True) + EPS) * nw_ref[...]
    h = (y * (1.0 + mod_ref[1:2, :]) + mod_ref[0:1, :]).astype(BF16)
    cos = rope_ref[:, :LANES]
    sin = rope_ref[:, LANES:]
    reps = ATT_WIDTH // LANES
    cos_q = jnp.concatenate([cos] * reps, axis=1)
    sin_q = jnp.concatenate([sin] * reps, axis=1)
    pb_ref[:, PB_Q:PB_Q + ATT_WIDTH] = (_rope(_dot(h, wq[...]), cos_q, sin_q) * ATT_SCALE).astype(BF16)
    pb_ref[:, PB_KV:PB_KV + ATT_KV_WIDTH] = _rope(_dot(h, wk[...]), cos, sin).astype(BF16)
    pb_ref[:, PB_KV + ATT_KV_WIDTH:PB_KV + 2 * ATT_KV_WIDTH] = _dot(h, wv[...]).astype(BF16)
    pf_ref[:, PF_Z:PF_Z + SSD_WIDTH] = jax.nn.silu(_dot(h, wz[...]))
    pf_ref[:, PF_AG:PF_AG + ATT_WIDTH] = jax.nn.silu(_dot(h, wag[...]))
    pf_ref[:, PF_CG:PF_CG + CONV_WIDTH] = jax.nn.silu(_dot(h, wcg[...]))
    glu = _dot(h, wglu[...])
    pf_ref[:, PF_GLU:PF_GLU + CONV_WIDTH] = glu[:, :CONV_WIDTH]
    pf_ref[:, PF_GLU + CONV_WIDTH:PF_GLU + 2 * CONV_WIDTH] = jax.nn.sigmoid(glu[:, CONV_WIDTH:])
    pf_ref[:, PF_DT:PF_DT + 2 * LANES] = _dot(h, wdt[...])
    xbc_ref[...] = _dot(h, wxbc[...])


def _token_specs(tm, d, nct, lat_off, tile_off):
    return [
        pl.BlockSpec((None, tm, d), lambda b, i: (b, jnp.minimum(i + tile_off, nct - 1), 0)),
        pl.BlockSpec((None, tm, d), lambda b, i: (b, jnp.maximum(i + tile_off - nct, 0) + lat_off, 0)),
    ]


def _inproj(csrc, xsrc, lat_off, t, modsel, norm_w, rope_t, w_packed, layer, nct):
    bsz, _, d = xsrc.shape
    tm = ROW_TILE
    row_map = lambda b, i: (b, i, 0)
    in_specs = _token_specs(tm, d, nct, lat_off, 0) + [
        pl.BlockSpec((None, None, 3, d), lambda b, i: (b, jnp.where(i >= nct, 1, 0), 0, 0)),
        pl.BlockSpec((None, 1, d), lambda b, i: (layer, 0, 0)),
        pl.BlockSpec((tm, 2 * LANES), lambda b, i: (i, 0)),
        pl.BlockSpec((None, d, W_WIDTH), lambda b, i: (layer, 0, 0)),
    ]
    widths = (SSD_XBC, PF_WIDTH, PB_WIDTH)
    return pl.pallas_call(
        functools.partial(_inproj_kernel, nct=nct),
        grid=(bsz, t // tm),
        in_specs=in_specs,
        out_specs=[pl.BlockSpec((None, tm, w), row_map) for w in widths],
        out_shape=[jax.ShapeDtypeStruct((bsz, t, w), dt) for w, dt in zip(widths, (F32, F32, BF16))],
        compiler_params=_cparams("parallel", "arbitrary"),
        name="inproj",
    )(csrc, xsrc, modsel, norm_w, rope_t, w_packed)


def _ssd_prep(xs, bc, dt_raw, dtb, alog, fwd):
    q = SSD_CHUNK
    bb = bc[:, :SSD_GN]
    cb16 = bc[:, SSD_GN:]

    groups = [slice(g * SSD_STATE, (g + 1) * SSD_STATE) for g in range(SSD_GROUPS)]
    scores_g = [_dot_nt(cb16[:, gs], bb[:, gs]) for gs in groups]

    lane = lax.broadcasted_iota(jnp.int32, (q, LANES), 1)
    head_lane = lane < 3 * SSD_HEADS
    dt = jnp.where(head_lane, jax.nn.softplus(dt_raw + dtb), 0.0)
    da = dt * (-jnp.exp(alog))
    li = lax.broadcasted_iota(jnp.int32, (q, q), 0)
    si = lax.broadcasted_iota(jnp.int32, (q, q), 1)
    causal = (si <= li) if fwd else (si >= li)
    ones = jnp.where(causal, 1.0, 0.0).astype(BF16)
    cs3 = _dot(ones, jnp.concatenate(_split3(da), axis=1))
    cs = cs3[:, :LANES] + cs3[:, LANES:2 * LANES] + cs3[:, 2 * LANES:]
    total = jnp.sum(da, axis=0, keepdims=True)
    to_end = jnp.exp(total - cs)
    from_start = jnp.where(head_lane, jnp.exp(cs), 0.0)
    chunk_decay = jnp.where(head_lane[:SUBLANES], jnp.exp(total), 0.0)
    cs_t = cs.T

    er = lax.broadcasted_iota(jnp.int32, (LANES, SSD_WIDTH), 0)
    ec = lax.broadcasted_iota(jnp.int32, (LANES, SSD_WIDTH), 1)
    spread = jnp.where((ec // SSD_HEAD_DIM == er % SSD_HEADS) & (er < 3 * SSD_HEADS), 1.0, 0.0).astype(BF16)
    dt_e = _dot(_pack3(dt, SSD_HEADS), spread)
    stacked = jnp.concatenate([dt * to_end, from_start, chunk_decay], axis=0)
    stacked_e = _dot(_pack3(stacked, SSD_HEADS), spread)
    w_e = stacked_e[:q]
    fs_e = stacked_e[q:2 * q]
    cd_e = stacked_e[2 * q:2 * q + 1]

    xd = (xs * dt_e).astype(BF16)
    xdw = (xs * w_e).astype(BF16)
    low = lax.broadcasted_iota(jnp.int32, (q, LANES), 1) < SSD_HEAD_DIM
    zero16 = jnp.zeros((q, LANES), BF16)
    y_diag, new_states = [], []
    for g in range(SSD_GROUPS):
        gs = groups[g]
        gc = slice(g * GROUP_COLS, (g + 1) * GROUP_COLS)
        scores = scores_g[g]
        new_states.append(_dot(bb[:, gs].astype(F32).T.astype(BF16), xdw[:, gc]))
        for pr in range(GROUP_COLS // LANES):
            h0 = (g * GROUP_COLS + pr * LANES) // SSD_HEAD_DIM
            cols = slice(h0 * SSD_HEAD_DIM, h0 * SSD_HEAD_DIM + LANES)
            xpair = xd[:, cols]
            m0 = (scores * jnp.exp(jnp.where(causal, cs[:, h0:h0 + 1] - cs_t[h0:h0 + 1, :], -jnp.inf))).astype(BF16)
            m1 = (scores * jnp.exp(jnp.where(causal, cs[:, h0 + 1:h0 + 2] - cs_t[h0 + 1:h0 + 2, :], -jnp.inf))).astype(BF16)
            block_diag = jnp.concatenate([jnp.where(low, xpair, zero16), jnp.where(low, zero16, xpair)], axis=0)
            y_diag.append(_dot(jnp.concatenate([m0, m1], axis=1), block_diag))
    return y_diag, new_states, cb16, fs_e, cd_e


def _ssd_finish(prep, st_ref):
    y_diag, new_states, cb16, fs_e, cd_e = prep
    ys = []
    for g in range(SSD_GROUPS):
        gs = slice(g * SSD_STATE, (g + 1) * SSD_STATE)
        gc = slice(g * GROUP_COLS, (g + 1) * GROUP_COLS)
        entering = st_ref[g]
        y_off = _dot(cb16[:, gs], entering.astype(BF16)) * fs_e[:, gc]
        st_ref[g] = entering * cd_e[:, gc] + new_states[g]
        per_group = GROUP_COLS // LANES
        for pr in range(per_group):
            ys.append(y_diag[g * per_group + pr] + y_off[:, pr * LANES:(pr + 1) * LANES])
    return jnp.concatenate(ys, axis=1)


def _ssd_bwd_block(k, ncb, nb):
    return jnp.where(k < ncb, ncb - 1 - k, nb + ncb - 1 - k)


def _ssd_kernel(xm_ref, xp_ref, xn_ref, dt_ref, z_ref, cw_ref, cb_ref, dtb_ref, alog_ref, dvec_ref, nw_ref,
                o_ref, yb_ref, st_ref, xs_ref, bc_ref, *, ncb, nb):
    q = SSD_CHUNK
    rows = SSD_STEP
    n_sub = rows // q
    s = pl.program_id(1)

    def conv_block(cb):
        first = (cb == 0) | (cb == ncb)
        last = (cb == ncb - 1) | (cb == nb - 1)
        xext = jnp.concatenate([jnp.where(first, 0.0, xp_ref[...]), xm_ref[...],
                                jnp.where(last, 0.0, xn_ref[...])], axis=0)
        acc = cb_ref[...]
        for k in range(SSD_CONV):
            d = k - (SSD_CONV - 1) // 2
            if d == 0:
                tap = xext[SSD_HALO:SSD_HALO + rows]
            elif d > 0:
                tap = _shift_rows(xext, d)[SSD_HALO:SSD_HALO + rows]
            else:
                tap = _shift_rows(xext, SSD_HALO + d)[:rows]
            acc = acc + cw_ref[k:k + 1, :] * tap
        u = jax.nn.silu(acc)
        dst = pl.ds(pl.multiple_of(cb * rows, rows), rows)
        xs_ref[dst, :] = u[:, :SSD_WIDTH]
        bc_ref[dst, :] = u[:, SSD_WIDTH:].astype(BF16)

    def chunk_rows(blk, j):
        return pl.ds(pl.multiple_of(blk * rows + j * q, q), q)

    @pl.when((s == 0) | (s == nb))
    def _():
        st_ref[...] = jnp.zeros_like(st_ref)

    @pl.when(s <= nb - 1)
    def _():
        conv_block(_ssd_bwd_block(s, ncb, nb))

    @pl.when(s < nb)
    def _():
        blk = _ssd_bwd_block(s, ncb, nb)
        order = tuple(reversed(range(n_sub)))
        preps = [_ssd_prep(xs_ref[chunk_rows(blk, j), :], bc_ref[chunk_rows(blk, j), :], dt_ref[j * q:(j + 1) * q, :],
                           dtb_ref[...], alog_ref[...], False) for j in order]
        for j, prep in zip(order, preps):
            yb_ref[chunk_rows(blk, j), :] = _ssd_finish(prep, st_ref)

    @pl.when(s >= nb)
    def _():
        blk = s - nb
        preps = [_ssd_prep(xs_ref[chunk_rows(blk, j), :], bc_ref[chunk_rows(blk, j), :], dt_ref[j * q:(j + 1) * q, :],
                           dtb_ref[...], alog_ref[...], True) for j in range(n_sub)]
        for j, prep in enumerate(preps):
            y = _ssd_finish(prep, st_ref)
            tot = y + yb_ref[chunk_rows(blk, j), :] + dvec_ref[...] * xs_ref[chunk_rows(blk, j), :]
            gated = tot * z_ref[j * q:(j + 1) * q, :]
            outs = []
            for g in range(SSD_GROUPS):
                gc = slice(g * GROUP_COLS, (g + 1) * GROUP_COLS)
                v = gated[:, gc]
                outs.append(v * lax.rsqrt(jnp.mean(v * v, axis=-1, keepdims=True) + EPS) * nw_ref[:, gc])
            o_ref[j * q:(j + 1) * q, :] = jnp.concatenate(outs, axis=1).astype(BF16)


def _ssd(xbc, pf, conv_w, conv_b, dt_bias, a_log, dvec, norm_w, ctx_len):
    bsz, t, _ = pf.shape
    rows = SSD_STEP
    nb = t // rows
    ncb = ctx_len // rows
    hb = rows // SSD_HALO
    nhalo = t // SSD_HALO
    const2 = lambda b, s: (0, 0)
    xblk = lambda s: _ssd_bwd_block(jnp.minimum(s, nb - 1), ncb, nb)
    cur = lambda s: jnp.where(s < nb, _ssd_bwd_block(s, ncb, nb), s - nb)
    phase = lambda s: jnp.where(s < nb, 1, 0)
    fwd_blk = lambda s: jnp.maximum(s - nb, 0)
    in_specs = [
        pl.BlockSpec((None, rows, SSD_XBC), lambda b, s: (b, xblk(s), 0)),
        pl.BlockSpec((None, SSD_HALO, SSD_XBC), lambda b, s: (b, jnp.maximum(xblk(s) * hb - 1, 0), 0)),
        pl.BlockSpec((None, SSD_HALO, SSD_XBC), lambda b, s: (b, jnp.minimum(xblk(s) * hb + hb, nhalo - 1), 0)),
        pl.BlockSpec((None, rows, LANES), lambda b, s: (b, cur(s), PF_DT // LANES + phase(s))),
        pl.BlockSpec((None, rows, SSD_WIDTH), lambda b, s: (b, fwd_blk(s), PF_Z // SSD_WIDTH)),
        pl.BlockSpec((SSD_CONV, SSD_XBC), const2),
        pl.BlockSpec((1, SSD_XBC), const2),
        pl.BlockSpec((None, 1, LANES), lambda b, s: (phase(s), 0, 0)),
        pl.BlockSpec((None, 1, LANES), lambda b, s: (phase(s), 0, 0)),
        pl.BlockSpec((1, SSD_WIDTH), const2),
        pl.BlockSpec((1, SSD_WIDTH), const2),
    ]
    return pl.pallas_call(
        functools.partial(_ssd_kernel, ncb=ncb, nb=nb),
        grid=(bsz, 2 * nb),
        in_specs=in_specs,
        out_specs=pl.BlockSpec((None, rows, SSD_WIDTH), lambda b, s: (b, fwd_blk(s), 0)),
        out_shape=jax.ShapeDtypeStruct((bsz, t, SSD_WIDTH), BF16),
        scratch_shapes=[
            pltpu.VMEM((t, SSD_WIDTH), F32),
            pltpu.VMEM((SSD_GROUPS, SSD_STATE, GROUP_COLS), F32),
            pltpu.VMEM((t, SSD_WIDTH), F32),
            pltpu.VMEM((t, 2 * SSD_GN), BF16),
        ],
        compiler_params=_cparams("parallel", "arbitrary"),
        name="ssd",
    )(xbc, xbc, xbc, pf, pf, conv_w, conv_b, dt_bias, a_log, dvec, norm_w)


def _swap_halves(blk):
    half = blk.shape[-1] // 2
    return jnp.concatenate([blk[:, half:], blk[:, :half]], axis=1)


def _head_pads(blk, g, low):
    zero = jnp.zeros_like(blk)
    sw = _swap_halves(blk)
    if g == 0:
        return jnp.where(low, blk, zero), jnp.where(low, zero, sw)
    return jnp.where(low, sw, zero), jnp.where(low, zero, blk)


def _attn_kernel(q_ref, kvm_ref, kvp_ref, kvn_ref, kvc_ref, ag_ref, sink_ref, o_ref, pad_ref,
                 *, tile_off, nct, seq, ctx_len):
    qb = ATT_TILE
    blk = ATT_QBLOCK
    wk = blk + 2 * WINDOW
    tile = pl.program_id(1) + tile_off
    lat0 = (tile - nct) * qb
    low_k = lax.broadcasted_iota(jnp.int32, (1, LANES), 1) < ATT_HEAD_DIM

    kv = jnp.concatenate([kvc_ref[...], kvp_ref[...], kvm_ref[...], kvn_ref[...]], axis=0)
    for g in range(ATT_KV_HEADS):
        ka, kb = _head_pads(kv[:, :ATT_KV_WIDTH], g, low_k)
        va, vb = _head_pads(kv[:, ATT_KV_WIDTH:], g, low_k)
        pad_ref[g, 0] = ka
        pad_ref[g, 1] = kb
        pad_ref[g, 2] = va
        pad_ref[g, 3] = vb

    r = lax.broadcasted_iota(jnp.int32, (blk, wk), 0)
    col = lax.broadcasted_iota(jnp.int32, (blk, wk), 1)
    in_band = jnp.abs(col - WINDOW - r) <= WINDOW
    neg = jnp.finfo(F32).min
    rep = ATT_HEADS // ATT_KV_HEADS
    pairs_per_group = rep * ATT_HEAD_DIM // LANES
    for a in range(qb // blk):
        rows = slice(a * blk, (a + 1) * blk)
        kpos = lat0 + a * blk - WINDOW + col
        valid = in_band & (kpos >= 0) & (kpos < seq) & (lat0 >= 0)
        win0 = ctx_len + a * blk
        def operand(g, slot):
            if a == 0:
                return pad_ref[g, slot, 0:win0 + wk, :]
            return jnp.concatenate([pad_ref[g, slot, 0:ctx_len, :], pad_ref[g, slot, win0:win0 + wk, :]], axis=0)

        for g in range(ATT_KV_HEADS):
            for pr in range(pairs_per_group):
                p = g * pairs_per_group + pr
                qp = q_ref[rows, p * LANES:(p + 1) * LANES]
                acc = jnp.zeros((blk, LANES), F32)
                invs = []
                for j in range(2):
                    head = 2 * p + j
                    sink = sink_ref[head:head + 1, 0:1]
                    s_all = _dot_nt(qp, operand(g, j))
                    ps, dens = [], []
                    for r0 in range(0, blk, ATT_HALF):
                        s = s_all[r0:r0 + ATT_HALF]
                        s = jnp.concatenate([s[:, :ctx_len],
                                             jnp.where(valid[r0:r0 + ATT_HALF], s[:, ctx_len:], neg)], axis=1)
                        m = jnp.maximum(jnp.max(s, axis=-1, keepdims=True), sink)
                        pexp = jnp.exp2(s - m)
                        dens.append(jnp.sum(pexp, axis=-1, keepdims=True) + jnp.exp2(sink - m))
                        ps.append(pexp.astype(BF16))
                    acc = acc + _dot(jnp.concatenate(ps, axis=0), operand(g, 2 + j))
                    invs.append(1.0 / jnp.concatenate(dens, axis=0))
                gate = ag_ref[rows, p * LANES:(p + 1) * LANES]
                o_ref[rows, p * LANES:(p + 1) * LANES] = (acc * jnp.where(low_k, invs[0], invs[1]) * gate).astype(BF16)


def _attention(pf, pb, sink_tab, ctx_len, tile_off):
    bsz, t, _ = pf.shape
    qb = ATT_TILE
    nct = ctx_len // qb
    hb = qb // WINDOW
    ctx_hb = ctx_len // WINDOW
    n_hb = t // WINDOW
    ntiles = t // qb - tile_off
    kvw = 2 * ATT_KV_WIDTH
    kv_col = PB_KV // kvw
    in_specs = [
        pl.BlockSpec((None, qb, ATT_WIDTH), lambda b, i: (b, i + tile_off, PB_Q // ATT_WIDTH)),
        pl.BlockSpec((None, qb, kvw), lambda b, i: (b, i + tile_off, kv_col)),
        pl.BlockSpec((None, WINDOW, kvw), lambda b, i: (b, jnp.maximum((i + tile_off) * hb - 1, ctx_hb), kv_col)),
        pl.BlockSpec((None, WINDOW, kvw), lambda b, i: (b, jnp.minimum((i + tile_off) * hb + hb, n_hb - 1), kv_col)),
        pl.BlockSpec((None, ctx_len, kvw), lambda b, i: (b, 0, kv_col)),
        pl.BlockSpec((None, qb, ATT_WIDTH), lambda b, i: (b, i + tile_off, PF_AG // ATT_WIDTH)),
        pl.BlockSpec((ATT_HEADS, LANES), lambda b, i: (0, 0)),
    ]
    return pl.pallas_call(
        functools.partial(_attn_kernel, tile_off=tile_off, nct=nct, seq=t - ctx_len, ctx_len=ctx_len),
        grid=(bsz, ntiles),
        in_specs=in_specs,
        out_specs=pl.BlockSpec((None, qb, ATT_WIDTH), lambda b, i: (b, i, 0)),
        out_shape=jax.ShapeDtypeStruct((bsz, ntiles * qb, ATT_WIDTH), BF16),
        scratch_shapes=[pltpu.VMEM((ATT_KV_HEADS, 4, ctx_len + qb + 2 * WINDOW, LANES), BF16)],
        compiler_params=_cparams("parallel", "arbitrary"),
        name="attention",
    )(pb, pb, pb, pb, pb, pf, sink_tab)


def _conf_tile(gm_ref, gp_ref, gn_ref, cg_ref, dww_ref, dwb_ref, lnw_ref, lnb_ref, pww_ref, pwb_ref, hbuf, abuf,
               first, last):
    tc = CONF_TILE

    def glu(v):
        return v[:, :CONV_WIDTH] * v[:, CONV_WIDTH:]

    rows = tc + 2 * CONF_HALO
    ext = jnp.concatenate([jnp.where(first, 0.0, glu(gp_ref[...])), glu(gm_ref[...]),
                           jnp.where(last, 0.0, glu(gn_ref[...]))], axis=0)
    hbuf[0] = ext
    for j in range(1, SUBLANES):
        hbuf[j, 0:rows - SUBLANES, :] = _shift_rows(ext, j)
    base = CONF_HALO - CONV_PAD
    for r in range(tc // CONF_SUB):
        acc = jnp.broadcast_to(dwb_ref[...], (CONF_SUB // SUBLANES, SUBLANES, CONV_WIDTH))
        for k in range(CONV_KERNEL):
            blk, res = divmod(base + k, SUBLANES)
            strip = hbuf[res, pl.ds(r * CONF_SUB + blk * SUBLANES, CONF_SUB), :]
            acc = acc + dww_ref[k] * strip.reshape(CONF_SUB // SUBLANES, SUBLANES, CONV_WIDTH)
        acc = acc.reshape(CONF_SUB, CONV_WIDTH)
        xc = acc - jnp.mean(acc, axis=-1, keepdims=True)
        var = jnp.mean(xc * xc, axis=-1, keepdims=True)
        ln = xc * lax.rsqrt(var + EPS) * lnw_ref[...] + lnb_ref[...]
        abuf[pl.ds(r * CONF_SUB, CONF_SUB), :] = jax.nn.silu(ln)
    out = _dot(abuf[...].astype(BF16), pww_ref[...]) + pwb_ref[...]
    return (out * cg_ref[...]).astype(BF16)


def _mixout_kernel(c_ref, x_ref, os_ref, oa_ref, gm_ref, gp_ref, gn_ref, cg_ref, dww_ref, dwb_ref, lnw_ref, lnb_ref,
                   pww_ref, pwb_ref, mod_ref, w_ref, fnw_ref, o_ref, hbuf, abuf, *, final, nct, nt, tile_off):
    tile = pl.program_id(1) + tile_off
    first = (tile == 0) | (tile == nct)
    last = (tile == nct - 1) | (tile == nt - 1)
    a0 = SSD_WIDTH
    c0 = SSD_WIDTH + ATT_WIDTH
    u = _dot(os_ref[...], w_ref[0:a0, :]) + _dot(oa_ref[...], w_ref[a0:c0, :])
    oc = _conf_tile(gm_ref, gp_ref, gn_ref, cg_ref, dww_ref, dwb_ref, lnw_ref, lnb_ref, pww_ref, pwb_ref,
                    hbuf, abuf, first, last)
    u = u + _dot(oc, w_ref[c0:c0 + CONV_WIDTH, :])
    x = x_ref[...]
    if tile_off < nct:
        x = jnp.where(tile < nct, c_ref[...], x)
    xn = x + mod_ref[2:3, :] * u
    if final:
        xn = xn * lax.rsqrt(jnp.mean(xn * xn, axis=-1, keepdims=True) + EPS) * fnw_ref[...]
    o_ref[...] = xn


def _mixout(csrc, xsrc, lat_off, o_ssd, o_att, pf, dw_w, dw_b, ln_w, ln_b, pw_w16, pw_b, modsel, w_out16, layer,
            final_norm_w, ctx_len, final):
    bsz, t, _ = o_ssd.shape
    d = xsrc.shape[-1]
    tm = ROW_TILE
    nt = t // tm
    nct = ctx_len // tm
    tile_off = nct if final else 0
    hb = tm // CONF_HALO
    nhalo = t // CONF_HALO
    main = lambda b, i: (b, i + tile_off, 0)
    const2 = lambda b, i: (0, 0)
    glu_col = PF_GLU // (2 * CONV_WIDTH)
    in_specs = _token_specs(tm, d, nct, lat_off, tile_off) + [
        pl.BlockSpec((None, tm, SSD_WIDTH), main),
        pl.BlockSpec((None, tm, ATT_WIDTH), lambda b, i: (b, i, 0)),
        pl.BlockSpec((None, tm, 2 * CONV_WIDTH), lambda b, i: (b, i + tile_off, glu_col)),
        pl.BlockSpec((None, CONF_HALO, 2 * CONV_WIDTH),
                     lambda b, i: (b, jnp.maximum((i + tile_off) * hb - 1, 0), glu_col)),
        pl.BlockSpec((None, CONF_HALO, 2 * CONV_WIDTH),
                     lambda b, i: (b, jnp.minimum((i + tile_off) * hb + hb, nhalo - 1), glu_col)),
        pl.BlockSpec((None, tm, CONV_WIDTH), lambda b, i: (b, i + tile_off, PF_CG // CONV_WIDTH)),
        pl.BlockSpec((CONV_KERNEL, SUBLANES, CONV_WIDTH), lambda b, i: (0, 0, 0)),
        pl.BlockSpec((SUBLANES, CONV_WIDTH), const2),
        pl.BlockSpec((1, CONV_WIDTH), const2),
        pl.BlockSpec((1, CONV_WIDTH), const2),
        pl.BlockSpec((None, CONV_WIDTH, CONV_WIDTH), lambda b, i: (layer, 0, 0)),
        pl.BlockSpec((1, CONV_WIDTH), const2),
        pl.BlockSpec((None, None, 3, d), lambda b, i: (b, jnp.where(i + tile_off >= nct, 1, 0), 0, 0)),
        pl.BlockSpec((None,) + w_out16.shape[1:], lambda b, i: (layer, 0, 0)),
        pl.BlockSpec((1, d), const2),
    ]
    rows_out = t - tile_off * tm
    return pl.pallas_call(
        functools.partial(_mixout_kernel, final=final, nct=nct, nt=nt, tile_off=tile_off),
        grid=(bsz, rows_out // tm),
        in_specs=in_specs,
        out_specs=pl.BlockSpec((None, tm, d), lambda b, i: (b, i, 0)),
        out_shape=jax.ShapeDtypeStruct((bsz, rows_out, d), F32),
        scratch_shapes=[
            pltpu.VMEM((SUBLANES, tm + 2 * CONF_HALO, CONV_WIDTH), F32),
            pltpu.VMEM((tm, CONV_WIDTH), F32),
        ],
        compiler_params=_cparams("parallel", "arbitrary"),
        name="mixout_final" if final else "mixout",
    )(csrc, xsrc, o_ssd, o_att, pf, pf, pf, pf, dw_w, dw_b, ln_w, ln_b, pw_w16, pw_b, modsel, w_out16, final_norm_w)


def _rope_tables(seq, ctx_len):
    rows = seq // GRID_W
    row = jnp.repeat(jnp.arange(rows, dtype=F32), GRID_W)
    col = jnp.tile(jnp.arange(GRID_W, dtype=F32), rows)
    axis_dim = ATT_HEAD_DIM // 2
    inv_freq = ROPE_BASE ** (-jnp.arange(0, axis_dim, 2, dtype=F32) / axis_dim)
    ar = row[:, None] * inv_freq
    ac = col[:, None] * inv_freq
    cos = jnp.concatenate([jnp.cos(ar), jnp.cos(ar), jnp.cos(ac), jnp.cos(ac)], axis=1)
    sin = jnp.concatenate([-jnp.sin(ar), jnp.sin(ar), -jnp.sin(ac), jnp.sin(ac)], axis=1)
    reps = LANES // ATT_HEAD_DIM
    cos = jnp.concatenate([jnp.ones((ctx_len, LANES), F32), jnp.tile(cos, (1, reps))], axis=0)
    sin = jnp.concatenate([jnp.zeros((ctx_len, LANES), F32), jnp.tile(sin, (1, reps))], axis=0)
    return jnp.concatenate([cos, sin], axis=1)


def _head_lanes(a):
    rep = jnp.concatenate([a, a, a], axis=-1)
    return jnp.pad(rep, [(0, 0)] * (a.ndim - 1) + [(0, LANES - 3 * SSD_HEADS)])


def _pack_w_in(w_in):
    dt0 = SSD_XBC + SSD_WIDTH
    dt_w = w_in[..., dt0:dt0 + 2 * SSD_HEADS]
    packed = jnp.concatenate([w_in[..., :dt0], w_in[..., dt0 + 2 * SSD_HEADS:],
                              _head_lanes(dt_w[..., :SSD_HEADS]), _head_lanes(dt_w[..., SSD_HEADS:])], axis=-1)
    assert packed.shape[-1] == W_WIDTH
    return packed.astype(BF16)


def kernel(x, c, ctx, c_ctx, w_mod, b_mod, norm_w, w_in, ssd_conv_w, ssd_conv_b, ssd_dt_bias, ssd_a_log, ssd_d,
           ssd_norm_w, attn_sink, conv_dw_w, conv_dw_b, conv_ln_w, conv_ln_b, conv_pw_w, conv_pw_b, w_out,
           final_norm_w):
    bsz, seq, d = x.shape
    ctx_len = ctx.shape[1]
    depth = w_in.shape[0]
    assert ctx_len % ROW_TILE == 0 and seq % ROW_TILE == 0 and seq % GRID_W == 0
    assert ROW_TILE == ATT_TILE == CONF_TILE
    nct = ctx_len // ROW_TILE

    cond_rows = -(-(bsz + 1) // SUBLANES) * SUBLANES
    cond = jnp.zeros((cond_rows, d), F32).at[:bsz].set(c).at[bsz].set(c_ctx)
    mod = _modulation(cond, w_mod, b_mod)
    rope_t = _rope_tables(seq, ctx_len)
    csrc, xsrc, lat_off = ctx, x, 0
    w_in16 = _pack_w_in(w_in)
    w_out16 = w_out.astype(BF16)
    pw_w16 = conv_pw_w.astype(BF16)
    norm_w3 = norm_w[:, None, :]

    out = None
    for l in range(depth):
        final = l == depth - 1
        lat = mod[l, :bsz]
        ctxm = jnp.broadcast_to(mod[l, bsz], lat.shape)
        modsel = jnp.stack([ctxm, lat], axis=1).reshape(bsz, 2, 3, d)
        xbc, pf, pb = _inproj(csrc, xsrc, lat_off, ctx_len + seq, modsel, norm_w3, rope_t, w_in16, l, nct)

        o_ssd = _ssd(xbc, pf, ssd_conv_w[l], ssd_conv_b[l][None], _head_lanes(ssd_dt_bias[l])[:, None],
                     _head_lanes(ssd_a_log[l])[:, None], jnp.repeat(ssd_d[l], SSD_HEAD_DIM)[None],
                     ssd_norm_w[l][None], ctx_len)
        tile_off = nct if final else 0
        sink_tab = jnp.broadcast_to(attn_sink[l][:, None] * LOG2E, (ATT_HEADS, LANES))
        o_att = _attention(pf, pb, sink_tab, ctx_len, tile_off)
        dw_w = jnp.broadcast_to(conv_dw_w[l][:, None, :], (CONV_KERNEL, SUBLANES, CONV_WIDTH))
        dw_b = jnp.broadcast_to(conv_dw_b[l][None], (SUBLANES, CONV_WIDTH))
        out = _mixout(csrc, xsrc, lat_off, o_ssd, o_att, pf, dw_w, dw_b, conv_ln_w[l][None], conv_ln_b[l][None], pw_w16,
                      conv_pw_b[l][None], modsel, w_out16, l, final_norm_w[None], ctx_len, final)
        csrc, xsrc, lat_off = out, out, nct
    return out
```
